```python
import math
import jax, jax.numpy as jnp
from jax import lax
import numpy as np

D_MODEL = 1024
BATCH = 2
SEQ = 8192
DEPTH = 4
DEC_BATCH = 128
DEC_SEQ = 1
PAST_LEN = 2048
PAGE_SIZE = 128

D_MIX = D_MODEL
H_M = 4
DK_M = D_MIX // 2 // H_M
DV_M = D_MIX // 2 // H_M
H_F = 8
HD_F = D_MIX // 2 // H_F
MLSTM_CHUNK = 64
FOX_BLOCK = 128
D_FF = 2816
N_EXPERTS = 8
TOP_K = 2
D_FF_E = 1408
N_DENSE = (DEPTH + 1) // 2
N_MOE = DEPTH // 2
EPS = 1e-6
NEG_INIT = -1e30
IN_SIZES = (H_M * DK_M, H_M * DK_M, H_M * DV_M, H_M * DV_M, H_M, H_M,
            H_F * HD_F, H_F * HD_F, H_F * HD_F, H_F)
D_IN = sum(IN_SIZES)

kernel_name = 'hymba_mlstm_fox_moe_step'


def rms_norm(x, g):
    xf = x.astype(jnp.float32)
    y = xf * lax.rsqrt(jnp.mean(xf * xf, axis=-1, keepdims=True) + EPS)
    return (y * g.astype(jnp.float32)).astype(x.dtype)


def head_rms_norm(h, g):
    B, T, H, Dh = h.shape
    return rms_norm(h, g.reshape(H, Dh)).reshape(B, T, H * Dh)


def in_projection(hn, w_in, b_ig, b_fg, b_ff):
    B, T, _ = hn.shape
    f32 = jnp.float32
    z = hn @ w_in
    bounds = np.cumsum(IN_SIZES)[:-1].tolist()
    mq, mk, mv, mo, mi, mf, fq, fk, fv, ff = jnp.split(z, bounds, axis=-1)
    q_m = mq.reshape(B, T, H_M, DK_M)
    k_m = mk.reshape(B, T, H_M, DK_M) * (DK_M ** -0.5)
    v_m = mv.reshape(B, T, H_M, DV_M)
    ig = mi.astype(f32) + b_ig.astype(f32)
    lf_m = jax.nn.log_sigmoid(mf.astype(f32) + b_fg.astype(f32))
    q_f = fq.reshape(B, T, H_F, HD_F)
    k_f = fk.reshape(B, T, H_F, HD_F)
    v_f = fv.reshape(B, T, H_F, HD_F)
    lf_f = jax.nn.log_sigmoid(ff.astype(f32) + b_ff.astype(f32))
    return (q_m, k_m, v_m, mo, ig, lf_m), (q_f, k_f, v_f, lf_f)


def mlstm_chunkwise(q, k, v, ig, lf, C0, n0, m0):
    f32 = jnp.float32
    B, T, H, DK = q.shape
    L = math.gcd(T, MLSTM_CHUNK)
    NC = T // L

    def to_chunks(a):
        a = a.astype(f32).reshape((B, NC, L, H) + a.shape[3:])
        return jnp.moveaxis(a, (1, 3), (0, 2))

    causal = jnp.tril(jnp.ones((L, L), dtype=bool))

    def step(carry, inp):
        C, n, m = carry
        qc, kc, vc, ic, fc = inp
        b = jnp.cumsum(fc, axis=-1)
        dmat = b[..., :, None] - b[..., None, :] + ic[..., None, :]
        dmat = jnp.where(causal, dmat, -jnp.inf)
        inter = b + m[..., None]
        m_row = jnp.maximum(inter, jnp.max(dmat, axis=-1))
        w_intra = jnp.exp(dmat - m_row[..., None])
        w_inter = jnp.exp(inter - m_row)
        s = jnp.einsum('bhtd,bhsd->bhts', qc, kc) * w_intra
        num = (w_inter[..., None] * jnp.einsum('bhtd,bhde->bhte', qc, C)
               + jnp.einsum('bhts,bhse->bhte', s, vc))
        den = w_inter * jnp.einsum('bhtd,bhd->bht', qc, n) + jnp.sum(s, axis=-1)
        h = num / jnp.maximum(jnp.abs(den), jnp.exp(-m_row))[..., None]
        m_new = m_row[..., -1]
        decay = jnp.exp(inter[..., -1] - m_new)
        w_k = jnp.exp(b[..., -1:] - b + ic - m_new[..., None])
        C_new = decay[..., None, None] * C + jnp.einsum('bhs,bhsd,bhse->bhde', w_k, kc, vc)
        n_new = decay[..., None] * n + jnp.einsum('bhs,bhsd->bhd', w_k, kc)
        return (C_new, n_new, m_new), h

    carry0 = (C0.astype(f32), n0.astype(f32), m0.astype(f32))
    (C, n, m), h = lax.scan(step, carry0, (to_chunks(q), to_chunks(k), to_chunks(v),
                                           to_chunks(ig), to_chunks(lf)))
    h = jnp.transpose(h, (1, 0, 3, 2, 4)).reshape(B, T, H, v.shape[-1])
    return h, C, n, m


def fox_attend(q, F_q, q_pos, k, v, F_k, k_pos):
    s = jnp.einsum('bqhd,bkhd->bhqk', q, k).astype(jnp.float32) * (HD_F ** -0.5)
    s = s + jnp.swapaxes(F_q, 1, 2)[..., :, None] - jnp.swapaxes(F_k, 1, 2)[..., None, :]
    s = jnp.where(k_pos[None, :] <= q_pos[:, None], s, -jnp.inf)
    p = jax.nn.softmax(s, axis=-1)
    return jnp.einsum('bhqk,bkhd->bqhd', p.astype(v.dtype), v)


def merge_heads(h_m, o_m, h_f, g_m, g_f, w_o, dtype):
    hm = head_rms_norm(h_m.astype(dtype), g_m) * jax.nn.sigmoid(o_m)
    hf = head_rms_norm(h_f.astype(dtype), g_f)
    return jnp.concatenate([hm, hf], axis=-1) @ w_o


def swiglu(x, wg, wu, wd):
    return (jax.nn.silu(x @ wg) * (x @ wu)) @ wd


def moe_ffn(x, w_r, b_r, wg, wu, wd):
    logits = (x @ w_r).astype(jnp.float32) + b_r.astype(jnp.float32)
    top_v, top_i = lax.top_k(logits, TOP_K)
    gates = jax.nn.softmax(top_v, axis=-1)
    combine = jnp.sum(jax.nn.one_hot(top_i, N_EXPERTS, dtype=jnp.float32) * gates[..., None], axis=-2)
    y = jnp.zeros_like(x)
    for e in range(N_EXPERTS):
        y = y + combine[..., e:e + 1].astype(x.dtype) * swiglu(x, wg[e], wu[e], wd[e])
    return y


def channel_layer(l, x, g_ffn_norm, w_ffn_gate, w_ffn_up, w_ffn_down,
                  w_router, b_router, w_moe_gate, w_moe_up, w_moe_down):
    hn = rms_norm(x, g_ffn_norm[l])
    j = l // 2
    if l % 2 == 0:
        y = swiglu(hn, w_ffn_gate[j], w_ffn_up[j], w_ffn_down[j])
    else:
        y = moe_ffn(hn, w_router[j], b_router[j], w_moe_gate[j], w_moe_up[j], w_moe_down[j])
    return x + y


def setup_inputs(seed: int = 0) -> dict:
    key = jax.random.key(seed)
    ks = jax.random.split(key, 32)
    f32 = jnp.float32
    n_pages = PAST_LEN // PAGE_SIZE
    n_pool = (DEC_BATCH * n_pages * 5) // 4
    nrm = lambda k, shape, s: jax.random.normal(k, shape, f32) * s
    x_prompt = nrm(ks[0], (BATCH, SEQ, D_MODEL), 1.0)
    x_sample = nrm(ks[1], (DEC_BATCH, DEC_SEQ, D_MODEL), 1.0)
    state_C = nrm(ks[2], (DEPTH, DEC_BATCH, H_M, DK_M, DV_M), 0.5)
    state_n = nrm(ks[3], (DEPTH, DEC_BATCH, H_M, DK_M), 0.5)
    state_m = nrm(ks[4], (DEPTH, DEC_BATCH, H_M), 1.0)
    cache_k = nrm(ks[5], (DEPTH, n_pool, PAGE_SIZE, H_F, HD_F), 1.0)
    cache_v = nrm(ks[6], (DEPTH, n_pool, PAGE_SIZE, H_F, HD_F), 1.0)
    cache_logf = jax.nn.log_sigmoid(3.0 + nrm(ks[7], (DEPTH, n_pool, PAGE_SIZE, H_F), 0.5))
    page_table = jax.random.permutation(ks[8], n_pool)[:DEC_BATCH * n_pages]
    page_table = page_table.reshape(DEC_BATCH, n_pages).astype(jnp.int32)
    g_mix_norm = 1.0 + nrm(ks[9], (DEPTH, D_MODEL), 0.02)
    w_in = nrm(ks[10], (DEPTH, D_MODEL, D_IN), D_MODEL ** -0.5)
    b_ig = nrm(ks[11], (DEPTH, H_M), 0.1)
    b_fg = 3.0 + nrm(ks[12], (DEPTH, H_M), 0.5)
    b_ff = 3.0 + nrm(ks[13], (DEPTH, H_F), 0.5)
    g_out_m = 1.0 + nrm(ks[14], (DEPTH, H_M * DV_M), 0.02)
    g_out_f = 1.0 + nrm(ks[15], (DEPTH, H_F * HD_F), 0.02)
    w_out = nrm(ks[16], (DEPTH, D_MIX, D_MODEL), D_MIX ** -0.5)
    g_ffn_norm = 1.0 + nrm(ks[17], (DEPTH, D_MODEL), 0.02)
    w_ffn_gate = nrm(ks[18], (N_DENSE, D_MODEL, D_FF), D_MODEL ** -0.5)
    w_ffn_up = nrm(ks[19], (N_DENSE, D_MODEL, D_FF), D_MODEL ** -0.5)
    w_ffn_down = nrm(ks[20], (N_DENSE, D_FF, D_MODEL), D_FF ** -0.5)
    w_router = nrm(ks[21], (N_MOE, D_MODEL, N_EXPERTS), D_MODEL ** -0.5)
    b_router = nrm(ks[22], (N_MOE, N_EXPERTS), 0.01)
    w_moe_gate = nrm(ks[23], (N_MOE, N_EXPERTS, D_MODEL, D_FF_E), D_MODEL ** -0.5)
    w_moe_up = nrm(ks[24], (N_MOE, N_EXPERTS, D_MODEL, D_FF_E), D_MODEL ** -0.5)
    w_moe_down = nrm(ks[25], (N_MOE, N_EXPERTS, D_FF_E, D_MODEL), D_FF_E ** -0.5)
    g_final = 1.0 + nrm(ks[26], (D_MODEL,), 0.02)
    return {'x_prompt': x_prompt, 'x_sample': x_sample,
            'state_C': state_C, 'state_n': state_n, 'state_m': state_m,
            'cache_k': cache_k, 'cache_v': cache_v, 'cache_logf': cache_logf,
            'page_table': page_table,
            'g_mix_norm': g_mix_norm, 'w_in': w_in, 'b_ig': b_ig, 'b_fg': b_fg, 'b_ff': b_ff,
            'g_out_m': g_out_m, 'g_out_f': g_out_f, 'w_out': w_out,
            'g_ffn_norm': g_ffn_norm, 'w_ffn_gate': w_ffn_gate, 'w_ffn_up': w_ffn_up,
            'w_ffn_down': w_ffn_down, 'w_router': w_router, 'b_router': b_router,
            'w_moe_gate': w_moe_gate, 'w_moe_up': w_moe_up, 'w_moe_down': w_moe_down,
            'g_final': g_final}


def reference(x_prompt, x_sample, state_C, state_n, state_m, cache_k, cache_v, cache_logf,
              page_table, g_mix_norm, w_in, b_ig, b_fg, b_ff, g_out_m, g_out_f, w_out,
              g_ffn_norm, w_ffn_gate, w_ffn_up, w_ffn_down, w_router, b_router,
              w_moe_gate, w_moe_up, w_moe_down, g_final):
    f32 = jnp.float32
    B, T, _ = x_prompt.shape
    DB, TS, _ = x_sample.shape
    past = page_table.shape[1] * PAGE_SIZE
    pos_p = jnp.arange(T)
    QB = math.gcd(T, FOX_BLOCK)
    q_pos_s = past + jnp.arange(TS)
    k_pos_s = jnp.arange(past + TS)

    xp, xs = x_prompt, x_sample
    pC, pn, pm, pk, pv, plf = [], [], [], [], [], []
    sC, sn, sm, sk, sv, slf = [], [], [], [], [], []
    for l in range(DEPTH):
        hn = rms_norm(xp, g_mix_norm[l])
        (q_m, k_m, v_m, o_m, ig, lf_m), (q_f, k_f, v_f, lf_f) = in_projection(
            hn, w_in[l], b_ig[l], b_fg[l], b_ff[l])
        C0 = jnp.zeros((B, H_M, DK_M, DV_M), f32)
        n0 = jnp.zeros((B, H_M, DK_M), f32)
        m0 = jnp.full((B, H_M), NEG_INIT, f32)
        h_m, C1, n1, m1 = mlstm_chunkwise(q_m, k_m, v_m, ig, lf_m, C0, n0, m0)
        F = jnp.cumsum(lf_f, axis=1)

        def q_block(i):
            start = i * QB
            qb = lax.dynamic_slice_in_dim(q_f, start, QB, axis=1)
            Fb = lax.dynamic_slice_in_dim(F, start, QB, axis=1)
            return fox_attend(qb, Fb, start + jnp.arange(QB), k_f, v_f, F, pos_p)

        h_f = lax.map(q_block, jnp.arange(T // QB))
        h_f = jnp.transpose(h_f, (1, 0, 2, 3, 4)).reshape(B, T, H_F, HD_F)
        xp = xp + merge_heads(h_m, o_m, h_f, g_out_m[l], g_out_f[l], w_out[l], xp.dtype)
        pC.append(C1); pn.append(n1); pm.append(m1)
        pk.append(k_f); pv.append(v_f); plf.append(lf_f)

        hn = rms_norm(xs, g_mix_norm[l])
        (q_m, k_m, v_m, o_m, ig, lf_m), (q_f, k_f, v_f, lf_f) = in_projection(
            hn, w_in[l], b_ig[l], b_fg[l], b_ff[l])
        h_m, C1, n1, m1 = mlstm_chunkwise(q_m, k_m, v_m, ig, lf_m,
                                          state_C[l], state_n[l], state_m[l])
        k_past = cache_k[l][page_table].reshape(DB, past, H_F, HD_F).astype(k_f.dtype)
        v_past = cache_v[l][page_table].reshape(DB, past, H_F, HD_F).astype(v_f.dtype)
        lf_past = cache_logf[l][page_table].reshape(DB, past, H_F).astype(f32)
        k_all = jnp.concatenate([k_past, k_f], axis=1)
        v_all = jnp.concatenate([v_past, v_f], axis=1)
        F_all = jnp.cumsum(jnp.concatenate([lf_past, lf_f], axis=1), axis=1)
        h_f = fox_attend(q_f, F_all[:, past:], q_pos_s, k_all, v_all, F_all, k_pos_s)
        xs = xs + merge_heads(h_m, o_m, h_f, g_out_m[l], g_out_f[l], w_out[l], xs.dtype)
        sC.append(C1); sn.append(n1); sm.append(m1)
        sk.append(k_f); sv.append(v_f); slf.append(lf_f)

        xp = channel_layer(l, xp, g_ffn_norm, w_ffn_gate, w_ffn_up, w_ffn_down,
                           w_router, b_router, w_moe_gate, w_moe_up, w_moe_down)
        xs = channel_layer(l, xs, g_ffn_norm, w_ffn_gate, w_ffn_up, w_ffn_down,
                           w_router, b_router, w_moe_gate, w_moe_up, w_moe_down)

    y_prompt = rms_norm(xp, g_final)
    y_sample = rms_norm(xs, g_final)
    return (y_prompt, y_sample,
            jnp.stack(pC), jnp.stack(pn), jnp.stack(pm),
            jnp.stack(pk), jnp.stack(pv), jnp.stack(plf),
            jnp.stack(sC), jnp.stack(sn), jnp.stack(sm),
            jnp.stack(sk), jnp.stack(sv), jnp.stack(slf))
```

```python
import functools

import numpy as np
import jax
import jax.numpy as jnp
from jax import lax
from jax.experimental import pallas as pl
from jax.experimental.pallas import tpu as pltpu

F32 = jnp.float32
BF16 = jnp.bfloat16
EPS = 1e-6
NEG_INIT = -1e30
TOP_K = 2
LANES = 128
GATE_ROWS = 16
VMEM_LIMIT = 56 * 1024 * 1024


def _cparams(sem):
    return pltpu.CompilerParams(dimension_semantics=sem, vmem_limit_bytes=VMEM_LIMIT)


def _rms(x, g):
    var = jnp.mean(x * x, axis=-1, keepdims=True)
    return x * lax.rsqrt(var + EPS) * g


def _log_sigmoid(x):
    return jnp.minimum(x, 0.0) - jnp.log(1.0 + jnp.exp(-jnp.abs(x)))


def _sigmoid(x):
    return 1.0 / (1.0 + jnp.exp(-x))


def _cumsum_rows(a):
    n = a.shape[0]
    row = lax.broadcasted_iota(jnp.int32, a.shape, 0)
    k = 1
    while k < n:
        a = a + jnp.where(row >= k, pltpu.roll(a, k, 0), 0.0)
        k *= 2
    return a


def _inproj_kernel(x_ref, g_ref, wb_ref, wg_ref, bg_ref, *refs, dq, dv, df, hf, hm, kscale, qscale,
                   prompt, tiles_per_seq):
    if prompt:
        (qm_ref, vm_ref, kT_ref, om_ref, qf_ref, kf_ref, vf_ref, kf16_ref, vf16_ref,
         G_ref, Gc_ref, GT_ref, GcT_ref, carry_ref) = refs
    else:
        (qm_ref, km_ref, vm_ref, om_ref, qf_ref, kf_ref, vf_ref, G_ref) = refs
    hn = _rms(x_ref[...], g_ref[...]).astype(BF16)

    def proj(off, width):
        return jnp.dot(hn, wb_ref[:, off:off + width], preferred_element_type=F32)

    o_mq, o_mk, o_mv, o_mo = 0, dq, 2 * dq, 2 * dq + dv
    o_fq = 2 * dq + 2 * dv
    o_fk, o_fv = o_fq + df, o_fq + 2 * df

    qm_ref[...] = proj(o_mq, dq).astype(qm_ref.dtype)
    km = proj(o_mk, dq) * kscale
    if prompt:
        kT_ref[...] = jnp.transpose(km).astype(kT_ref.dtype)
    else:
        km_ref[...] = km
    vm_ref[...] = proj(o_mv, dv).astype(vm_ref.dtype)
    om_ref[...] = proj(o_mo, dv).astype(om_ref.dtype)
    qf_ref[...] = (proj(o_fq, df) * qscale).astype(qf_ref.dtype)
    kf = proj(o_fk, df)
    vf = proj(o_fv, df)
    kf_ref[...] = kf
    vf_ref[...] = vf
    if prompt:
        kf16_ref[...] = kf.astype(BF16)
        vf16_ref[...] = vf.astype(BF16)

    zg = jnp.dot(hn, wg_ref[...], preferred_element_type=F32) + bg_ref[...]
    col = lax.broadcasted_iota(jnp.int32, zg.shape, 1)
    is_log = (col < hf) | ((col >= hf + hm) & (col < hf + 2 * hm))
    is_lin = (col >= hf) & (col < hf + hm)
    G = jnp.where(is_log, _log_sigmoid(zg), jnp.where(is_lin, zg, 0.0))
    G_ref[...] = G
    if prompt:
        i = pl.program_id(0)

        @pl.when(i % tiles_per_seq == 0)
        def _():
            carry_ref[...] = jnp.zeros_like(carry_ref)

        Gc = _cumsum_rows(G) + carry_ref[0:1, :]
        carry_ref[0:1, :] = Gc[-1:, :]
        Gc_ref[...] = Gc
        GT_ref[...] = jnp.transpose(G)[:GATE_ROWS, :]
        GcT_ref[...] = jnp.transpose(Gc)[:GATE_ROWS, :]


def _inproj(x, g, w_big, w_gate, b_gate, dims, *, prompt, tm, seq_len):
    M, D = x.shape
    dq, dv, df, hf, hm, dk, hd = dims
    nb = w_big.shape[1]
    tm = min(tm, M)
    assert M % tm == 0
    if prompt:
        assert seq_len % tm == 0
    row = lambda i: (i, 0)
    colb = lambda i: (0, i)
    const = lambda i: (0, 0)
    in_specs = [pl.BlockSpec((tm, D), row), pl.BlockSpec((1, D), const),
                pl.BlockSpec((D, nb), const), pl.BlockSpec((D, LANES), const),
                pl.BlockSpec((1, LANES), const)]
    if prompt:
        out_shape = [jax.ShapeDtypeStruct((M, dq), BF16), jax.ShapeDtypeStruct((M, dv), BF16),
                     jax.ShapeDtypeStruct((dq, M), BF16), jax.ShapeDtypeStruct((M, dv), BF16),
                     jax.ShapeDtypeStruct((M, df), BF16),
                     jax.ShapeDtypeStruct((M, df), F32), jax.ShapeDtypeStruct((M, df), F32),
                     jax.ShapeDtypeStruct((M, df), BF16), jax.ShapeDtypeStruct((M, df), BF16),
                     jax.ShapeDtypeStruct((M, LANES), F32), jax.ShapeDtypeStruct((M, LANES), F32),
                     jax.ShapeDtypeStruct((GATE_ROWS, M), F32), jax.ShapeDtypeStruct((GATE_ROWS, M), F32)]
        out_specs = [pl.BlockSpec((tm, dq), row), pl.BlockSpec((tm, dv), row),
                     pl.BlockSpec((dq, tm), colb), pl.BlockSpec((tm, dv), row),
                     pl.BlockSpec((tm, df), row),
                     pl.BlockSpec((tm, df), row), pl.BlockSpec((tm, df), row),
                     pl.BlockSpec((tm, df), row), pl.BlockSpec((tm, df), row),
                     pl.BlockSpec((tm, LANES), row), pl.BlockSpec((tm, LANES), row),
                     pl.BlockSpec((GATE_ROWS, tm), colb), pl.BlockSpec((GATE_ROWS, tm), colb)]
        scratch = [pltpu.VMEM((8, LANES), F32)]
    else:
        out_shape = ([jax.ShapeDtypeStruct((M, dq), F32)] * 2 + [jax.ShapeDtypeStruct((M, dv), F32)] * 2
                     + [jax.ShapeDtypeStruct((M, df), F32)] * 3 + [jax.ShapeDtypeStruct((M, LANES), F32)])
        out_specs = ([pl.BlockSpec((tm, dq), row)] * 2 + [pl.BlockSpec((tm, dv), row)] * 2
                     + [pl.BlockSpec((tm, df), row)] * 3 + [pl.BlockSpec((tm, LANES), row)])
        scratch = []
    kern = functools.partial(_inproj_kernel, dq=dq, dv=dv, df=df, hf=hf, hm=hm,
                             kscale=float(dk) ** -0.5, qscale=float(hd) ** -0.5,
                             prompt=prompt, tiles_per_seq=max(seq_len // tm, 1))
    return pl.pallas_call(
        kern, grid=(M // tm,), in_specs=in_specs, out_specs=out_specs, out_shape=out_shape,
        scratch_shapes=scratch, compiler_params=_cparams(("arbitrary",)),
        name="inproj_prompt" if prompt else "inproj_sample")(x, g, w_big, w_gate, b_gate)


def _mlstm_prompt_kernel(q_ref, v_ref, kT_ref, o_ref, G_ref, Gc_ref, GT_ref, GcT_ref, gm_ref,
                         hm_ref, caug_ref, mout_ref, bprev_ref, m_ref, *, H, DK, DV, L, c_ig, c_lf):
    c = pl.program_id(1)

    @pl.when(c == 0)
    def _():
        caug_ref[...] = jnp.zeros_like(caug_ref)
        m_ref[...] = jnp.full_like(m_ref, NEG_INIT)
        bprev_ref[...] = jnp.zeros_like(bprev_ref)

    row = lax.broadcasted_iota(jnp.int32, (L, L), 0)
    colL = lax.broadcasted_iota(jnp.int32, (L, L), 1)
    causal = row >= colL
    lane = lax.broadcasted_iota(jnp.int32, (L, LANES), 1)
    ones_col = jnp.where(lane == 0, 1.0, 0.0)
    G = G_ref[...]
    Gc = Gc_ref[...]
    GT = GT_ref[...]
    GcT = GcT_ref[...]
    for h in range(H):
        bp = bprev_ref[0:1, c_lf + h:c_lf + h + 1]
        b_col = Gc[:, c_lf + h:c_lf + h + 1] - bp
        b_row = GcT[c_lf + h:c_lf + h + 1, :] - bp
        i_row = GT[c_ig + h:c_ig + h + 1, :]
        i_col = G[:, c_ig + h:c_ig + h + 1]
        m_prev = m_ref[h:h + 1, 0:1]
        dm = jnp.where(causal, b_col - b_row + i_row, -jnp.inf)
        inter = b_col + m_prev
        m_row = jnp.maximum(inter, jnp.max(dm, axis=-1, keepdims=True))
        w_intra = jnp.exp(dm - m_row)
        w_inter = jnp.exp(inter - m_row)
        qh = q_ref[:, h * DK:(h + 1) * DK]
        kTh = kT_ref[h * DK:(h + 1) * DK, :]
        vh = v_ref[:, h * DV:(h + 1) * DV].astype(F32)
        vaug = jnp.concatenate([vh, ones_col], axis=1)
        s = jnp.dot(qh, kTh, preferred_element_type=F32) * w_intra
        intra = jnp.dot(s.astype(BF16), vaug.astype(BF16), preferred_element_type=F32)
        cst = caug_ref[h]
        inter_qc = jnp.dot(qh, cst.astype(BF16), preferred_element_type=F32)
        r = w_inter * inter_qc + intra
        num = r[:, :DV]
        den = r[:, DV:DV + 1]
        hval = num / jnp.maximum(jnp.abs(den), jnp.exp(-m_row))
        m_new = m_row[L - 1:L, :]
        decay = jnp.exp(inter[L - 1:L, :] - m_new)
        w_k = jnp.exp(b_col[L - 1:L, :] - b_col + i_col - m_new)
        upd = jnp.dot(kTh, (vaug * w_k).astype(BF16), preferred_element_type=F32)
        caug_ref[h] = decay * cst + upd
        m_ref[h:h + 1, :] = jnp.broadcast_to(m_new, (1, LANES))
        gate = _sigmoid(o_ref[:, h * DV:(h + 1) * DV].astype(F32))
        hm_ref[:, h * DV:(h + 1) * DV] = (_rms(hval, gm_ref[:, h * DV:(h + 1) * DV]) * gate).astype(hm_ref.dtype)
    bprev_ref[0:1, :] = Gc[L - 1:L, :]

    @pl.when(c == pl.num_programs(1) - 1)
    def _():
        mout_ref[...] = m_ref[...]


def _mlstm_prompt(qm, vm, kT, om, G, Gc, GT, GcT, g_m, *, B, T, H, DK, DV, L, c_ig, c_lf):
    M = B * T
    NC = T // L
    assert T % L == 0
    rowb = lambda b, c: (b * NC + c, 0)
    colb = lambda b, c: (0, b * NC + c)
    kern = functools.partial(_mlstm_prompt_kernel, H=H, DK=DK, DV=DV, L=L, c_ig=c_ig, c_lf=c_lf)
    return pl.pallas_call(
        kern, grid=(B, NC),
        in_specs=[pl.BlockSpec((L, H * DK), rowb), pl.BlockSpec((L, H * DV), rowb),
                  pl.BlockSpec((H * DK, L), colb), pl.BlockSpec((L, H * DV), rowb),
                  pl.BlockSpec((L, LANES), rowb), pl.BlockSpec((L, LANES), rowb),
                  pl.BlockSpec((GATE_ROWS, L), colb), pl.BlockSpec((GATE_ROWS, L), colb),
                  pl.BlockSpec((1, H * DV), lambda b, c: (0, 0))],
        out_specs=[pl.BlockSpec((L, H * DV), rowb),
                   pl.BlockSpec((None, H, DK, DV + LANES), lambda b, c: (b, 0, 0, 0)),
                   pl.BlockSpec((None, 8, LANES), lambda b, c: (b, 0, 0))],
        out_shape=[jax.ShapeDtypeStruct((M, H * DV), BF16),
                   jax.ShapeDtypeStruct((B, H, DK, DV + LANES), F32),
                   jax.ShapeDtypeStruct((B, 8, LANES), F32)],
        scratch_shapes=[pltpu.VMEM((8, LANES), F32), pltpu.VMEM((8, LANES), F32)],
        compiler_params=_cparams(("parallel", "arbitrary")),
        name="mlstm_prompt")(qm, vm, kT, om, G, Gc, GT, GcT, g_m)


def _fox_prompt_kernel(ii_ref, jj_ref, q_ref, k_ref, v_ref, Gc_ref, GcT_ref, gf_ref, o_ref,
                       fq_sc, m_sc, l_sc, acc_sc, *, TB, HD):
    hp = pl.program_id(1)
    p = pl.program_id(2)
    i = ii_ref[p]
    j = jj_ref[p]
    nh = LANES // HD
    lane = lax.broadcasted_iota(jnp.int32, (TB, LANES), 1)

    @pl.when(j == 0)
    def _():
        Gc = Gc_ref[...]
        for hh in range(nh):
            fq = jnp.sum(jnp.where(lane == hp * nh + hh, Gc, 0.0), axis=1, keepdims=True)
            fq_sc[hh] = jnp.broadcast_to(fq, (TB, LANES))
        m_sc[...] = jnp.full_like(m_sc, -jnp.inf)
        l_sc[...] = jnp.zeros_like(l_sc)
        acc_sc[...] = jnp.zeros_like(acc_sc)

    def update(masked):
        q = q_ref[...]
        k = k_ref[...]
        v = v_ref[...]
        for hh in range(nh):
            qh = jnp.where(lane // HD == hh, q, jnp.zeros_like(q))
            s = lax.dot_general(qh, k, (((1,), (1,)), ((), ())), preferred_element_type=F32)
            fk = GcT_ref[pl.ds(hp * nh + hh, 1), :]
            s = s + (fq_sc[hh][:, 0:1] - fk)
            if masked:
                r = lax.broadcasted_iota(jnp.int32, (TB, TB), 0)
                cc = lax.broadcasted_iota(jnp.int32, (TB, TB), 1)
                s = jnp.where(r >= cc, s, -jnp.inf)
            m_old = m_sc[hh]
            m_new = jnp.maximum(m_old, jnp.max(s, axis=1, keepdims=True))
            alpha = jnp.exp(m_old - m_new)
            pr = jnp.exp(s - m_new[:, 0:1])
            l_sc[hh] = alpha * l_sc[hh] + jnp.sum(pr, axis=1, keepdims=True)
            acc_sc[hh] = alpha * acc_sc[hh] + jnp.dot(pr.astype(BF16), v, preferred_element_type=F32)
            m_sc[hh] = m_new

    @pl.when(j < i)
    def _():
        update(False)

    @pl.when(j == i)
    def _():
        update(True)
        out = jnp.zeros((TB, LANES), F32)
        for hh in range(nh):
            sel = lane // HD == hh
            oh = acc_sc[hh] / l_sc[hh]
            ms = jnp.sum(jnp.where(sel, oh * oh, 0.0), axis=1, keepdims=True) * (1.0 / HD)
            out = jnp.where(sel, oh * lax.rsqrt(ms + EPS), out)
        o_ref[...] = (out * gf_ref[...]).astype(o_ref.dtype)


def _fox_prompt(qf, kf16, vf16, Gc, GcT, g_f, *, B, T, HF, HD, TB):
    M = B * T
    TB = min(TB, T)
    assert T % TB == 0
    nq = T // TB
    nh = LANES // HD
    HP = HF // nh
    ii = np.concatenate([np.full(i + 1, i, np.int32) for i in range(nq)])
    jj = np.concatenate([np.arange(i + 1, dtype=np.int32) for i in range(nq)])
    qmap = lambda b, hp, p, ii, jj: (b * nq + ii[p], hp)
    kmap = lambda b, hp, p, ii, jj: (b * nq + jj[p], hp)
    kern = functools.partial(_fox_prompt_kernel, TB=TB, HD=HD)
    grid_spec = pltpu.PrefetchScalarGridSpec(
        num_scalar_prefetch=2, grid=(B, HP, len(ii)),
        in_specs=[pl.BlockSpec((TB, LANES), qmap), pl.BlockSpec((TB, LANES), kmap),
                  pl.BlockSpec((TB, LANES), kmap),
                  pl.BlockSpec((TB, LANES), lambda b, hp, p, ii, jj: (b * nq + ii[p], 0)),
                  pl.BlockSpec((GATE_ROWS, TB), lambda b, hp, p, ii, jj: (0, b * nq + jj[p])),
                  pl.BlockSpec((1, LANES), lambda b, hp, p, ii, jj: (0, hp))],
        out_specs=pl.BlockSpec((TB, LANES), qmap),
        scratch_shapes=[pltpu.VMEM((nh, TB, LANES), F32)] * 4)
    return pl.pallas_call(
        kern, grid_spec=grid_spec, out_shape=jax.ShapeDtypeStruct((M, HF * HD), BF16),
        compiler_params=_cparams(("parallel", "parallel", "arbitrary")),
        name="fox_prompt")(jnp.asarray(ii), jnp.asarray(jj), qf, kf16, vf16, Gc, GcT, g_f)


def _outproj_kernel(x_ref, hm_ref, hf_ref, wm_ref, wf_ref, o_ref):
    y = jnp.dot(hm_ref[...].astype(BF16), wm_ref[...], preferred_element_type=F32)
    y = y + jnp.dot(hf_ref[...].astype(BF16), wf_ref[...], preferred_element_type=F32)
    o_ref[...] = x_ref[...] + y


def _outproj(x, hm, hf, w_m, w_f, *, tm):
    M, D = x.shape
    tm = min(tm, M)
    assert M % tm == 0
    row = lambda i: (i, 0)
    const = lambda i: (0, 0)
    return pl.pallas_call(
        _outproj_kernel, grid=(M // tm,),
        in_specs=[pl.BlockSpec((tm, D), row), pl.BlockSpec((tm, hm.shape[1]), row),
                  pl.BlockSpec((tm, hf.shape[1]), row),
                  pl.BlockSpec(w_m.shape, const), pl.BlockSpec(w_f.shape, const)],
        out_specs=pl.BlockSpec((tm, D), row), out_shape=jax.ShapeDtypeStruct((M, D), F32),
        compiler_params=_cparams(("parallel",)), name="outproj")(x, hm, hf, w_m, w_f)


def _ffn_kernel(x_ref, g_ref, wg_ref, wu_ref, wd_ref, *refs, moe, n_experts, final):
    refs = list(refs)
    if moe:
        wr_ref, br_ref = refs[:2]
        refs = refs[2:]
    if final:
        gfin_ref = refs[0]
        refs = refs[1:]
    o_ref, hn_sc, acc_sc, comb_sc = refs
    e = pl.program_id(1)

    @pl.when(e == 0)
    def _():
        hn = _rms(x_ref[...], g_ref[...])
        hn_sc[...] = hn.astype(BF16)
        acc_sc[...] = jnp.zeros_like(acc_sc)
        if moe:
            lane = lax.broadcasted_iota(jnp.int32, (hn.shape[0], LANES), 1)
            lanef = lane.astype(F32)
            lg = jnp.dot(hn, wr_ref[...], preferred_element_type=F32,
                         precision=lax.Precision.HIGHEST) + br_ref[...]
            lg = jnp.where(lane < n_experts, lg, -jnp.inf)
            m1 = jnp.max(lg, axis=1, keepdims=True)
            i1 = jnp.min(jnp.where(lg == m1, lanef, float(LANES)), axis=1, keepdims=True)
            lg2 = jnp.where(lanef == i1, -jnp.inf, lg)
            m2 = jnp.max(lg2, axis=1, keepdims=True)
            i2 = jnp.min(jnp.where(lg2 == m2, lanef, float(LANES)), axis=1, keepdims=True)
            e2 = jnp.exp(m2 - m1)
            g1 = 1.0 / (1.0 + e2)
            g2 = e2 / (1.0 + e2)
            comb_sc[...] = jnp.where(lanef == i1, g1, 0.0) + jnp.where(lanef == i2, g2, 0.0)

    hn = hn_sc[...]
    a = jnp.dot(hn, wg_ref[...], preferred_element_type=F32)
    u = jnp.dot(hn, wu_ref[...], preferred_element_type=F32)
    act = (a * _sigmoid(a) * u).astype(BF16)
    y = jnp.dot(act, wd_ref[...], preferred_element_type=F32)
    if moe:
        lane = lax.broadcasted_iota(jnp.int32, comb_sc.shape, 1)
        ce = jnp.sum(jnp.where(lane == e, comb_sc[...], 0.0), axis=1, keepdims=True)
        y = ce * y
    acc_sc[...] += y

    @pl.when(e == pl.num_programs(1) - 1)
    def _():
        xo = x_ref[...] + acc_sc[...]
        if final:
            xo = _rms(xo, gfin_ref[...])
        o_ref[...] = xo


def _ffn(x, g, wg, wu, wd, router, g_final, *, tm, chunk):
    M, D = x.shape
    tm = min(tm, M)
    assert M % tm == 0
    moe = router is not None
    final = g_final is not None
    row = lambda i, e: (i, 0)
    const = lambda i, e: (0, 0)
    if moe:
        E = wg.shape[0]
        w_specs = [pl.BlockSpec((None, D, chunk), lambda i, e: (e, 0, 0)),
                   pl.BlockSpec((None, D, chunk), lambda i, e: (e, 0, 0)),
                   pl.BlockSpec((None, chunk, D), lambda i, e: (e, 0, 0))]
    else:
        assert wg.shape[1] % chunk == 0
        E = wg.shape[1] // chunk
        w_specs = [pl.BlockSpec((D, chunk), lambda i, e: (0, e)),
                   pl.BlockSpec((D, chunk), lambda i, e: (0, e)),
                   pl.BlockSpec((chunk, D), lambda i, e: (e, 0))]
    args = [x, g, wg, wu, wd]
    in_specs = [pl.BlockSpec((tm, D), row), pl.BlockSpec((1, D), const)] + w_specs
    if moe:
        args += list(router)
        in_specs += [pl.BlockSpec((D, LANES), const), pl.BlockSpec((1, LANES), const)]
    if final:
        args.append(g_final)
        in_specs.append(pl.BlockSpec((1, D), const))
    kern = functools.partial(_ffn_kernel, moe=moe, n_experts=E, final=final)
    return pl.pallas_call(
        kern, grid=(M // tm, E), in_specs=in_specs,
        out_specs=pl.BlockSpec((tm, D), row), out_shape=jax.ShapeDtypeStruct((M, D), F32),
        scratch_shapes=[pltpu.VMEM((tm, D), BF16), pltpu.VMEM((tm, D), F32), pltpu.VMEM((tm, LANES), F32)],
        compiler_params=_cparams(("parallel", "arbitrary")),
        name="ffn_moe" if moe else "ffn_dense")(*args)


def _mlstm_step_kernel(q_ref, k_ref, v_ref, o_ref, qc_ref, kc_ref, G_ref, C_ref, n_ref, m_ref, gm_ref,
                       hm_ref, Cn_ref, nn_ref, mn_ref, *, NB, H, DK, DV, c_ig, c_lf):
    lane = lax.broadcasted_iota(jnp.int32, (1, LANES), 1)
    for bb in range(NB):
        m_row = jnp.zeros((1, LANES), F32)
        for h in range(H):
            C = C_ref[bb, h]
            qc = qc_ref[h * DK:(h + 1) * DK, bb:bb + 1]
            kc = kc_ref[h * DK:(h + 1) * DK, bb:bb + 1]
            qr = q_ref[bb:bb + 1, h * DK:(h + 1) * DK]
            kr = k_ref[bb:bb + 1, h * DK:(h + 1) * DK]
            vr = v_ref[bb:bb + 1, h * DV:(h + 1) * DV]
            nr = n_ref[bb:bb + 1, h * DK:(h + 1) * DK]
            ig = G_ref[bb:bb + 1, c_ig + h:c_ig + h + 1]
            lf = G_ref[bb:bb + 1, c_lf + h:c_lf + h + 1]
            m0 = m_ref[bb:bb + 1, h:h + 1]
            inter = lf + m0
            m_new = jnp.maximum(inter, ig)
            w_inter = jnp.exp(inter - m_new)
            w_intra = jnp.exp(ig - m_new)
            s = jnp.sum(qr * kr, axis=1, keepdims=True) * w_intra
            num = w_inter * jnp.sum(qc * C, axis=0, keepdims=True) + s * vr
            den = w_inter * jnp.sum(qr * nr, axis=1, keepdims=True) + s
            hval = num / jnp.maximum(jnp.abs(den), jnp.exp(-m_new))
            Cn_ref[bb, h] = w_inter * C + (w_intra * kc) * vr
            nn_ref[bb:bb + 1, h * DK:(h + 1) * DK] = w_inter * nr + w_intra * kr
            m_row = jnp.where(lane == h, m_new, m_row)
            gate = _sigmoid(o_ref[bb:bb + 1, h * DV:(h + 1) * DV])
            hm_ref[bb:bb + 1, h * DV:(h + 1) * DV] = _rms(hval, gm_ref[:, h * DV:(h + 1) * DV]) * gate
        mn_ref[bb:bb + 1, :] = m_row


def _mlstm_step(q, k, v, o, G, state_C, state_n, state_m, layer, g_m, *, NB, H, DK, DV, c_ig, c_lf):
    DB = q.shape[0]
    NB = min(NB, DB)
    assert DB % NB == 0
    S = DB // NB
    qc = q.reshape(S, NB, H * DK).transpose(0, 2, 1)
    kc = k.reshape(S, NB, H * DK).transpose(0, 2, 1)
    row = lambda s: (s, 0)
    kern = functools.partial(_mlstm_step_kernel, NB=NB, H=H, DK=DK, DV=DV, c_ig=c_ig, c_lf=c_lf)
    return pl.pallas_call(
        kern, grid=(S,),
        in_specs=[pl.BlockSpec((NB, H * DK), row), pl.BlockSpec((NB, H * DK), row),
                  pl.BlockSpec((NB, H * DV), row), pl.BlockSpec((NB, H * DV), row),
                  pl.BlockSpec((None, H * DK, NB), lambda s: (s, 0, 0)),
                  pl.BlockSpec((None, H * DK, NB), lambda s: (s, 0, 0)),
                  pl.BlockSpec((NB, LANES), row),
                  pl.BlockSpec((None, NB, H, DK, DV), lambda s: (layer, s, 0, 0, 0)),
                  pl.BlockSpec((None, NB, H * DK), lambda s: (layer, s, 0)),
                  pl.BlockSpec((None, NB, H), lambda s: (layer, s, 0)),
                  pl.BlockSpec((1, H * DV), lambda s: (0, 0))],
        out_specs=[pl.BlockSpec((NB, H * DV), row),
                   pl.BlockSpec((NB, H, DK, DV), lambda s: (s, 0, 0, 0)),
                   pl.BlockSpec((NB, H * DK), row), pl.BlockSpec((NB, LANES), row)],
        out_shape=[jax.ShapeDtypeStruct((DB, H * DV), F32), jax.ShapeDtypeStruct((DB, H, DK, DV), F32),
                   jax.ShapeDtypeStruct((DB, H * DK), F32), jax.ShapeDtypeStruct((DB, LANES), F32)],
        compiler_params=_cparams(("parallel",)),
        name="mlstm_step")(q, k, v, o, qc, kc, G, state_C, state_n, state_m, g_m)


def _fox_step_kernel(pt_ref, q_ref, kc_ref, vc_ref, G_ref, gf_ref, *refs, NP, HF, HD, PAGE):
    k_refs = refs[:NP]
    v_refs = refs[NP:2 * NP]
    lf_refs = refs[2 * NP:3 * NP]
    o_ref = refs[3 * NP]
    DF = HF * HD
    sub = lax.broadcasted_iota(jnp.int32, (HF, DF), 0)
    lane = lax.broadcasted_iota(jnp.int32, (HF, DF), 1)
    diag = lane // HD == sub
    q = q_ref[...]
    qbd = jnp.where(diag, jnp.broadcast_to(q, (HF, DF)), 0.0)
    qbd16 = qbd.astype(BF16)
    sub_g = lax.broadcasted_iota(jnp.int32, (HF, LANES), 0)
    lane_g = lax.broadcasted_iota(jnp.int32, (HF, LANES), 1)
    carry = jnp.sum(jnp.where(lane_g == sub_g, jnp.broadcast_to(G_ref[...], (HF, LANES)), 0.0),
                    axis=1, keepdims=True)
    ti = lax.broadcasted_iota(jnp.int32, (PAGE, PAGE), 0)
    tj = lax.broadcasted_iota(jnp.int32, (PAGE, PAGE), 1)
    later = jnp.where(ti > tj, 1.0, 0.0)
    s_cur = jnp.sum(qbd * kc_ref[...], axis=1, keepdims=True)
    scores = [None] * NP
    mx = s_cur
    for p in range(NP - 1, -1, -1):
        lfT = lf_refs[p][...]
        suf = jnp.dot(lfT, later, preferred_element_type=F32, precision=lax.Precision.HIGHEST) + carry
        carry = carry + jnp.sum(lfT, axis=1, keepdims=True)
        kp = k_refs[p][...].astype(BF16)
        s = lax.dot_general(qbd16, kp, (((1,), (1,)), ((), ())), preferred_element_type=F32) + suf
        scores[p] = s
        mx = jnp.maximum(mx, jnp.max(s, axis=1, keepdims=True))
    p_cur = jnp.exp(s_cur - mx)
    l = p_cur
    acc = p_cur * vc_ref[...]
    for p in range(NP):
        pr = jnp.exp(scores[p] - mx)
        l = l + jnp.sum(pr, axis=1, keepdims=True)
        acc = acc + jnp.dot(pr.astype(BF16), v_refs[p][...].astype(BF16), preferred_element_type=F32)
    o = acc / l
    ms = jnp.sum(jnp.where(diag, o * o, 0.0), axis=1, keepdims=True) * (1.0 / HD)
    on = jnp.where(diag, o * lax.rsqrt(ms + EPS), 0.0)
    o_ref[...] = jnp.sum(on, axis=0, keepdims=True) * gf_ref[...]


def _fox_step(q, k_cur, v_cur, lf_cur, cache_k, cache_v, cache_lfT, page_table, layer, g_f, *, HF, HD):
    DB = q.shape[0]
    NP = page_table.shape[1]
    PAGE = cache_k.shape[2]
    DF = HF * HD
    pt = page_table.reshape(-1)
    r3 = lambda a: a.reshape(DB, 1, a.shape[-1])
    cur = lambda b, pt: (b, 0, 0)

    def page_map(p):
        return lambda b, pt: (layer, pt[b * NP + p], 0, 0)

    kv_specs = [pl.BlockSpec((None, None, PAGE, DF), page_map(p)) for p in range(NP)]
    lf_specs = [pl.BlockSpec((None, None, HF, PAGE), page_map(p)) for p in range(NP)]
    grid_spec = pltpu.PrefetchScalarGridSpec(
        num_scalar_prefetch=1, grid=(DB,),
        in_specs=[pl.BlockSpec((None, 1, DF), cur)] * 3 + [pl.BlockSpec((None, 1, LANES), cur),
                                                           pl.BlockSpec((1, DF), lambda b, pt: (0, 0))]
                 + kv_specs + kv_specs + lf_specs,
        out_specs=pl.BlockSpec((None, 1, DF), cur))
    kern = functools.partial(_fox_step_kernel, NP=NP, HF=HF, HD=HD, PAGE=PAGE)
    out = pl.pallas_call(
        kern, grid_spec=grid_spec, out_shape=jax.ShapeDtypeStruct((DB, 1, DF), F32),
        compiler_params=_cparams(("arbitrary",)),
        name="fox_step")(pt, r3(q), r3(k_cur), r3(v_cur), r3(lf_cur), g_f,
                         *([cache_k] * NP), *([cache_v] * NP), *([cache_lfT] * NP))
    return out.reshape(DB, DF)


def kernel(x_prompt, x_sample, state_C, state_n, state_m, cache_k, cache_v, cache_logf, page_table, g_mix_norm, w_in, b_ig, b_fg, b_ff, g_out_m, g_out_f, w_out, g_ffn_norm, w_ffn_gate, w_ffn_up, w_ffn_down, w_router, b_router, w_moe_gate, w_moe_up, w_moe_down, g_final):
    B, T, D = x_prompt.shape
    DB = x_sample.shape[0]
    assert x_sample.shape[1] == 1
    depth = w_in.shape[0]
    HM, DK, DV = state_C.shape[2:]
    PAGE, HF, HD = cache_k.shape[2:]
    n_pool = cache_k.shape[1]
    dq, dv, df = HM * DK, HM * DV, HF * HD
    assert dq % LANES == 0 and dv % LANES == 0 and df % LANES == 0 and LANES % HD == 0
    assert HF + 2 * HM <= GATE_ROWS
    dims = (dq, dv, df, HF, HM, DK, HD)
    c_ig, c_lf = HF, HF + HM
    E = w_router.shape[-1]
    chunk = w_moe_gate.shape[-1]
    M = B * T

    o = np.cumsum([0, dq, dq, dv, dv, HM, HM, df, df, df, HF]).tolist()
    w_big = jnp.concatenate([w_in[:, :, o[0]:o[4]], w_in[:, :, o[6]:o[9]]], axis=-1).astype(BF16)
    n_gate = HF + 2 * HM
    w_gate = jnp.concatenate([w_in[:, :, o[9]:o[10]], w_in[:, :, o[4]:o[6]],
                              jnp.zeros((depth, D, LANES - n_gate), w_in.dtype)], axis=-1).astype(BF16)
    b_gate = jnp.concatenate([b_ff, b_ig, b_fg, jnp.zeros((depth, LANES - n_gate), F32)],
                             axis=-1).astype(F32).reshape(depth, 1, LANES)
    w_out16 = w_out.astype(BF16)
    wfg, wfu, wfd = w_ffn_gate.astype(BF16), w_ffn_up.astype(BF16), w_ffn_down.astype(BF16)
    wmg, wmu, wmd = w_moe_gate.astype(BF16), w_moe_up.astype(BF16), w_moe_down.astype(BF16)
    w_r = jnp.concatenate([w_router, jnp.zeros(w_router.shape[:2] + (LANES - E,), F32)], axis=-1)
    b_r = jnp.concatenate([b_router, jnp.zeros((b_router.shape[0], LANES - E), F32)],
                          axis=-1).reshape(-1, 1, LANES)
    ck = cache_k.reshape(depth, n_pool, PAGE, df)
    cv = cache_v.reshape(depth, n_pool, PAGE, df)
    clfT = jnp.swapaxes(cache_logf, 2, 3)
    sn = state_n.reshape(depth, DB, dq)

    xp = x_prompt.reshape(M, D)
    xs = x_sample.reshape(DB, D)
    pC, pn, pm, pk, pv, plf = [], [], [], [], [], []
    sC, sn_o, sm, sk, sv, slf = [], [], [], [], [], []
    for l in range(depth):
        gmix = g_mix_norm[l].reshape(1, D)
        gm = g_out_m[l].reshape(1, dv)
        gf = g_out_f[l].reshape(1, df)
        w_om, w_of = w_out16[l, :dv], w_out16[l, dv:]
        (qm, vm, kT, om, qf, kf, vf, kf16, vf16, G, Gc, GT, GcT) = _inproj(
            xp, gmix, w_big[l], w_gate[l], b_gate[l], dims, prompt=True, tm=512, seq_len=T)
        hm, caug, mout = _mlstm_prompt(qm, vm, kT, om, G, Gc, GT, GcT, gm, B=B, T=T, H=HM, DK=DK, DV=DV,
                                       L=min(256, T), c_ig=c_ig, c_lf=c_lf)
        hf = _fox_prompt(qf, kf16, vf16, Gc, GcT, gf, B=B, T=T, HF=HF, HD=HD, TB=512)
        xp = _outproj(xp, hm, hf, w_om, w_of, tm=512)
        pC.append(caug[..., :DV]); pn.append(caug[..., DV]); pm.append(mout[:, :HM, 0])
        pk.append(kf.reshape(B, T, HF, HD)); pv.append(vf.reshape(B, T, HF, HD))
        plf.append(G[:, :HF].reshape(B, T, HF))
        (qm, km, vm, om, qf, kf, vf, G) = _inproj(
            xs, gmix, w_big[l], w_gate[l], b_gate[l], dims, prompt=False, tm=DB, seq_len=1)
        hm, Cn, nn, mn = _mlstm_step(qm, km, vm, om, G, state_C, sn, state_m, l, gm,
                                     NB=8, H=HM, DK=DK, DV=DV, c_ig=c_ig, c_lf=c_lf)
        hf = _fox_step(qf, kf, vf, G, ck, cv, clfT, page_table, l, gf, HF=HF, HD=HD)
        xs = _outproj(xs, hm, hf, w_om, w_of, tm=DB)
        sC.append(Cn); sn_o.append(nn.reshape(DB, HM, DK)); sm.append(mn[:, :HM])
        sk.append(kf.reshape(DB, 1, HF, HD)); sv.append(vf.reshape(DB, 1, HF, HD))
        slf.append(G[:, :HF].reshape(DB, 1, HF))
        gffn = g_ffn_norm[l].reshape(1, D)
        gfin = g_final.reshape(1, D) if l == depth - 1 else None
        j = l // 2
        if l % 2 == 0:
            args = (wfg[j], wfu[j], wfd[j], None)
        else:
            args = (wmg[j], wmu[j], wmd[j], (w_r[j], b_r[j]))
        xp = _ffn(xp, gffn, *args, gfin, tm=512, chunk=chunk)
        xs = _ffn(xs, gffn, *args, gfin, tm=DB, chunk=chunk)

    return (xp.reshape(B, T, D), xs.reshape(DB, 1, D),
            jnp.stack(pC), jnp.stack(pn), jnp.stack(pm),
            jnp.stack(pk), jnp.stack(pv), jnp.stack(plf),
            jnp.stack(sC), jnp.stack(sn_o), jnp.stack(sm),
            jnp.stack(sk), jnp.stack(sv), jnp.stack(slf))
```

```python
import functools

import numpy as np
import jax
import jax.numpy as jnp
from jax import lax
from jax.experimental import pallas as pl
from jax.experimental.pallas import tpu as pltpu

F32 = jnp.float32
BF16 = jnp.bfloat16
EPS = 1e-6
NEG_INIT = -1e30
TOP_K = 2
LANES = 128
GATE_ROWS = 16
VMEM_LIMIT = 56 * 1024 * 1024
LOG2E = 1.4426950408889634
N_SPLIT = 3


def _cparams(sem):
    return pltpu.CompilerParams(dimension_semantics=sem, vmem_limit_bytes=VMEM_LIMIT)


def _rms(x, g):
    var = jnp.mean(x * x, axis=-1, keepdims=True)
    return x * lax.rsqrt(var + EPS) * g


def _log_sigmoid(x):
    return jnp.minimum(x, 0.0) - jnp.log(1.0 + jnp.exp(-jnp.abs(x)))


def _sigmoid(x):
    return 1.0 / (1.0 + jnp.exp(-x))


def _cumsum_rows(a):
    n = a.shape[0]
    row = lax.broadcasted_iota(jnp.int32, a.shape, 0)
    k = 1
    while k < n:
        a = a + jnp.where(row >= k, pltpu.roll(a, k, 0), 0.0)
        k *= 2
    return a


def _merge_head_pairs(z, hd):
    n = z.shape[1] // (2 * LANES)
    return jnp.concatenate(
        [z[:, 2 * p * LANES:(2 * p + 1) * LANES] + pltpu.roll(z[:, (2 * p + 1) * LANES:(2 * p + 2) * LANES], hd, 1)
         for p in range(n)], axis=1)


def _inproj_kernel(x_ref, g_ref, wm_ref, wf_ref, wg_ref, bg_ref, *refs, dq, dv, df, hf, hm, hd, kscale, qscale,
                   prompt, tiles_per_seq):
    if prompt:
        (scat_ref, qm_ref, vm_ref, kT_ref, om_ref, qp_ref, kp_ref, vp_ref, kf_ref, vf_ref,
         G_ref, Gc_ref, GT_ref, GcT_ref, carry_ref) = refs
    else:
        (qm_ref, km_ref, vm_ref, om_ref, qf_ref, kf_ref, vf_ref, G_ref) = refs
    hn = _rms(x_ref[...], g_ref[...]).astype(BF16)

    def proj(w_ref, off, width):
        return jnp.dot(hn, w_ref[:, off:off + width], preferred_element_type=F32)

    zg = jnp.dot(hn, wg_ref[...], preferred_element_type=F32) + bg_ref[...]
    col = lax.broadcasted_iota(jnp.int32, zg.shape, 1)
    is_log = (col < hf) | ((col >= hf + hm) & (col < hf + 2 * hm))
    is_lin = (col >= hf) & (col < hf + hm)
    G = jnp.where(is_log, _log_sigmoid(zg), jnp.where(is_lin, zg, 0.0))
    G_ref[...] = G

    qm_ref[...] = proj(wm_ref, 0, dq).astype(qm_ref.dtype)
    km = proj(wm_ref, dq, dq) * kscale
    if prompt:
        kT_ref[...] = jnp.transpose(km).astype(kT_ref.dtype)
    else:
        km_ref[...] = km
    vm_ref[...] = proj(wm_ref, 2 * dq, dv).astype(vm_ref.dtype)
    om_ref[...] = proj(wm_ref, 2 * dq + dv, dv).astype(om_ref.dtype)
    if not prompt:
        qf_ref[...] = proj(wf_ref, 0, df) * qscale
        kf_ref[...] = proj(wf_ref, df, df)
        vf_ref[...] = proj(wf_ref, 2 * df, df)
        return

    i = pl.program_id(0)

    @pl.when(i % tiles_per_seq == 0)
    def _():
        carry_ref[...] = jnp.zeros_like(carry_ref)

    Gc = _cumsum_rows(G) + carry_ref[0:1, :]
    carry_ref[0:1, :] = Gc[-1:, :]
    Gc_ref[...] = Gc
    GT_ref[...] = jnp.transpose(G)[:GATE_ROWS, :]
    GcT_ref[...] = jnp.transpose(Gc)[:GATE_ROWS, :]

    dfp = hf * LANES
    lane = lax.broadcasted_iota(jnp.int32, (1, dfp), 1) % LANES
    zq = proj(wf_ref, 0, dfp) * (qscale * LOG2E)
    qp_ref[...] = (zq + jnp.where((lane >= hd) & (lane < hd + N_SPLIT), 1.0, 0.0)).astype(BF16)
    zk = proj(wf_ref, dfp, dfp)
    zv = proj(wf_ref, 2 * dfp, dfp)
    kf_ref[...] = _merge_head_pairs(zk, hd)
    vf_ref[...] = _merge_head_pairs(zv, hd)
    rem = Gc * (-LOG2E)
    bias = jnp.zeros_like(zk)
    for t in range(N_SPLIT):
        piece = rem.astype(BF16)
        rem = rem - piece.astype(F32)
        bias = bias + jnp.dot(piece, scat_ref[t], preferred_element_type=F32)
    kp_ref[...] = (zk + bias).astype(BF16)
    vp_ref[...] = (zv + jnp.where(lane == hd, 1.0, 0.0)).astype(BF16)


def _inproj(x, g, w_m, w_f, w_gate, b_gate, scat, dims, *, prompt, tm, seq_len):
    M, D = x.shape
    dq, dv, df, hf, hm, dk, hd = dims
    dfp = hf * LANES
    tm = min(tm, M)
    assert M % tm == 0
    if prompt:
        assert seq_len % tm == 0
    row = lambda i: (i, 0)
    colb = lambda i: (0, i)
    const = lambda i: (0, 0)
    args = [x, g, w_m, w_f, w_gate, b_gate]
    in_specs = [pl.BlockSpec((tm, D), row), pl.BlockSpec((1, D), const),
                pl.BlockSpec(w_m.shape, const), pl.BlockSpec(w_f.shape, const),
                pl.BlockSpec((D, LANES), const), pl.BlockSpec((1, LANES), const)]
    if prompt:
        args.append(scat)
        in_specs.append(pl.BlockSpec(scat.shape, lambda i: (0, 0, 0)))
        out_shape = [jax.ShapeDtypeStruct((M, dq), BF16), jax.ShapeDtypeStruct((M, dv), BF16),
                     jax.ShapeDtypeStruct((dq, M), BF16), jax.ShapeDtypeStruct((M, dv), BF16),
                     jax.ShapeDtypeStruct((M, dfp), BF16), jax.ShapeDtypeStruct((M, dfp), BF16),
                     jax.ShapeDtypeStruct((M, dfp), BF16),
                     jax.ShapeDtypeStruct((M, df), F32), jax.ShapeDtypeStruct((M, df), F32),
                     jax.ShapeDtypeStruct((M, LANES), F32), jax.ShapeDtypeStruct((M, LANES), F32),
                     jax.ShapeDtypeStruct((GATE_ROWS, M), F32), jax.ShapeDtypeStruct((GATE_ROWS, M), F32)]
        out_specs = [pl.BlockSpec((tm, dq), row), pl.BlockSpec((tm, dv), row),
                     pl.BlockSpec((dq, tm), colb), pl.BlockSpec((tm, dv), row),
                     pl.BlockSpec((tm, dfp), row), pl.BlockSpec((tm, dfp), row), pl.BlockSpec((tm, dfp), row),
                     pl.BlockSpec((tm, df), row), pl.BlockSpec((tm, df), row),
                     pl.BlockSpec((tm, LANES), row), pl.BlockSpec((tm, LANES), row),
                     pl.BlockSpec((GATE_ROWS, tm), colb), pl.BlockSpec((GATE_ROWS, tm), colb)]
        scratch = [pltpu.VMEM((8, LANES), F32)]
    else:
        out_shape = ([jax.ShapeDtypeStruct((M, dq), F32)] * 2 + [jax.ShapeDtypeStruct((M, dv), F32)] * 2
                     + [jax.ShapeDtypeStruct((M, df), F32)] * 3 + [jax.ShapeDtypeStruct((M, LANES), F32)])
        out_specs = ([pl.BlockSpec((tm, dq), row)] * 2 + [pl.BlockSpec((tm, dv), row)] * 2
                     + [pl.BlockSpec((tm, df), row)] * 3 + [pl.BlockSpec((tm, LANES), row)])
        scratch = []
    kern = functools.partial(_inproj_kernel, dq=dq, dv=dv, df=df, hf=hf, hm=hm, hd=hd,
                             kscale=float(dk) ** -0.5, qscale=float(hd) ** -0.5,
                             prompt=prompt, tiles_per_seq=max(seq_len // tm, 1))
    return pl.pallas_call(
        kern, grid=(M // tm,), in_specs=in_specs, out_specs=out_specs, out_shape=out_shape,
        scratch_shapes=scratch, compiler_params=_cparams(("arbitrary",)),
        name="inproj_prompt" if prompt else "inproj_sample")(*args)


def _mlstm_prompt_kernel(q_ref, v_ref, kT_ref, o_ref, G_ref, Gc_ref, GT_ref, GcT_ref, gm_ref,
                         hm_ref, caug_ref, mout_ref, bprev_ref, m_ref, *, H, DK, DV, L, c_ig, c_lf):
    c = pl.program_id(1)

    @pl.when(c == 0)
    def _():
        caug_ref[...] = jnp.zeros_like(caug_ref)
        m_ref[...] = jnp.full_like(m_ref, NEG_INIT)
        bprev_ref[...] = jnp.zeros_like(bprev_ref)

    row = lax.broadcasted_iota(jnp.int32, (L, L), 0)
    colL = lax.broadcasted_iota(jnp.int32, (L, L), 1)
    causal = row >= colL
    lane = lax.broadcasted_iota(jnp.int32, (L, LANES), 1)
    ones_col = jnp.where(lane == 0, 1.0, 0.0)
    G = G_ref[...]
    Gc = Gc_ref[...]
    GT = GT_ref[...]
    GcT = GcT_ref[...]
    for h in range(H):
        bp = bprev_ref[0:1, c_lf + h:c_lf + h + 1]
        b_col = Gc[:, c_lf + h:c_lf + h + 1] - bp
        b_row = GcT[c_lf + h:c_lf + h + 1, :] - bp
        i_row = GT[c_ig + h:c_ig + h + 1, :]
        i_col = G[:, c_ig + h:c_ig + h + 1]
        m_prev = m_ref[h:h + 1, 0:1]
        dm = jnp.where(causal, b_col - b_row + i_row, -jnp.inf)
        inter = b_col + m_prev
        m_row = jnp.maximum(inter, jnp.max(dm, axis=-1, keepdims=True))
        w_intra = jnp.exp(dm - m_row)
        w_inter = jnp.exp(inter - m_row)
        qh = q_ref[:, h * DK:(h + 1) * DK]
        kTh = kT_ref[h * DK:(h + 1) * DK, :]
        vh = v_ref[:, h * DV:(h + 1) * DV].astype(F32)
        vaug = jnp.concatenate([vh, ones_col], axis=1)
        s = jnp.dot(qh, kTh, preferred_element_type=F32) * w_intra
        intra = jnp.dot(s.astype(BF16), vaug.astype(BF16), preferred_element_type=F32)
        cst = caug_ref[h]
        inter_qc = jnp.dot(qh, cst.astype(BF16), preferred_element_type=F32)
        r = w_inter * inter_qc + intra
        num = r[:, :DV]
        den = r[:, DV:DV + 1]
        hval = num / jnp.maximum(jnp.abs(den), jnp.exp(-m_row))
        m_new = m_row[L - 1:L, :]
        decay = jnp.exp(inter[L - 1:L, :] - m_new)
        w_k = jnp.exp(b_col[L - 1:L, :] - b_col + i_col - m_new)
        upd = jnp.dot(kTh, (vaug * w_k).astype(BF16), preferred_element_type=F32)
        caug_ref[h] = decay * cst + upd
        m_ref[h:h + 1, :] = jnp.broadcast_to(m_new, (1, LANES))
        gate = _sigmoid(o_ref[:, h * DV:(h + 1) * DV].astype(F32))
        hm_ref[:, h * DV:(h + 1) * DV] = (_rms(hval, gm_ref[:, h * DV:(h + 1) * DV]) * gate).astype(hm_ref.dtype)
    bprev_ref[0:1, :] = Gc[L - 1:L, :]

    @pl.when(c == pl.num_programs(1) - 1)
    def _():
        mout_ref[...] = m_ref[...]


def _mlstm_prompt(qm, vm, kT, om, G, Gc, GT, GcT, g_m, *, B, T, H, DK, DV, L, c_ig, c_lf):
    M = B * T
    NC = T // L
    assert T % L == 0
    rowb = lambda b, c: (b * NC + c, 0)
    colb = lambda b, c: (0, b * NC + c)
    kern = functools.partial(_mlstm_prompt_kernel, H=H, DK=DK, DV=DV, L=L, c_ig=c_ig, c_lf=c_lf)
    return pl.pallas_call(
        kern, grid=(B, NC),
        in_specs=[pl.BlockSpec((L, H * DK), rowb), pl.BlockSpec((L, H * DV), rowb),
                  pl.BlockSpec((H * DK, L), colb), pl.BlockSpec((L, H * DV), rowb),
                  pl.BlockSpec((L, LANES), rowb), pl.BlockSpec((L, LANES), rowb),
                  pl.BlockSpec((GATE_ROWS, L), colb), pl.BlockSpec((GATE_ROWS, L), colb),
                  pl.BlockSpec((1, H * DV), lambda b, c: (0, 0))],
        out_specs=[pl.BlockSpec((L, H * DV), rowb),
                   pl.BlockSpec((None, H, DK, DV + LANES), lambda b, c: (b, 0, 0, 0)),
                   pl.BlockSpec((None, 8, LANES), lambda b, c: (b, 0, 0))],
        out_shape=[jax.ShapeDtypeStruct((M, H * DV), BF16),
                   jax.ShapeDtypeStruct((B, H, DK, DV + LANES), F32),
                   jax.ShapeDtypeStruct((B, 8, LANES), F32)],
        scratch_shapes=[pltpu.VMEM((8, LANES), F32), pltpu.VMEM((8, LANES), F32)],
        compiler_params=_cparams(("parallel", "arbitrary")),
        name="mlstm_prompt")(qm, vm, kT, om, G, Gc, GT, GcT, g_m)


def _fox_prompt_kernel(ii_ref, jj_ref, q_ref, k_ref, v_ref, gf_ref, o_ref, m_sc, acc_sc, *, TB, HD, G, RQ):
    p = pl.program_id(2)
    i = ii_ref[p]
    j = jj_ref[p]

    @pl.when(j == 0)
    def _():
        m_sc[...] = jnp.full_like(m_sc, -jnp.inf)
        acc_sc[...] = jnp.zeros_like(acc_sc)

    def update(masked):
        items = [(g, r) for g in range(G) for r in range(TB // RQ)]

        def qk(g, r):
            q = q_ref[r * RQ:(r + 1) * RQ, g * LANES:(g + 1) * LANES]
            k = k_ref[:, g * LANES:(g + 1) * LANES]
            return lax.dot_general(q, k, (((1,), (1,)), ((), ())), preferred_element_type=F32)

        s_next = qk(*items[0])
        for n, (g, r) in enumerate(items):
            s = s_next
            if n + 1 < len(items):
                s_next = qk(*items[n + 1])
            rows = slice(r * RQ, (r + 1) * RQ)
            v = v_ref[:, g * LANES:(g + 1) * LANES]
            if masked:
                rr = lax.broadcasted_iota(jnp.int32, (RQ, TB), 0) + r * RQ
                cc = lax.broadcasted_iota(jnp.int32, (RQ, TB), 1)
                s = jnp.where(rr >= cc, s, -jnp.inf)
            m_old = m_sc[g, rows, :]
            m_new = jnp.maximum(m_old, jnp.max(s, axis=1, keepdims=True))
            alpha = jnp.exp2(m_old - m_new)
            pr = jnp.concatenate([jnp.exp2(s[:, c * LANES:(c + 1) * LANES] - m_new).astype(BF16)
                                  for c in range(TB // LANES)], axis=1)
            acc_sc[g, rows, :] = alpha * acc_sc[g, rows, :] + jnp.dot(pr, v, preferred_element_type=F32)
            m_sc[g, rows, :] = m_new

    @pl.when(j < i)
    def _():
        update(False)

    @pl.when(j == i)
    def _():
        update(True)
        lane = lax.broadcasted_iota(jnp.int32, (1, LANES), 1)
        w = jnp.where(lane < HD, 1.0 / HD, jnp.where(lane == HD, EPS, 0.0))
        for gp in range(G // 2):
            outs = []
            for g in (2 * gp, 2 * gp + 1):
                a = acc_sc[g]
                outs.append(a * lax.rsqrt(jnp.sum(a * a * w, axis=1, keepdims=True)))
            out = jnp.where(lane < HD, outs[0], pltpu.roll(outs[1], HD, 1))
            o_ref[:, gp * LANES:(gp + 1) * LANES] = (out * gf_ref[:, gp * LANES:(gp + 1) * LANES]).astype(o_ref.dtype)


def _fox_prompt(qp, kp, vp, g_f, *, B, T, HF, HD, TB, G, RQ):
    M = B * T
    TB = min(TB, T)
    RQ = min(RQ, TB)
    assert T % TB == 0 and TB % RQ == 0 and HF % G == 0 and G % 2 == 0 and 2 * HD == LANES
    nq = T // TB
    ii = np.concatenate([np.full(i + 1, i, np.int32) for i in range(nq)])
    jj = np.concatenate([np.arange(i + 1, dtype=np.int32) for i in range(nq)])
    qmap = lambda b, hg, p, ii, jj: (b * nq + ii[p], hg)
    kmap = lambda b, hg, p, ii, jj: (b * nq + jj[p], hg)
    kern = functools.partial(_fox_prompt_kernel, TB=TB, HD=HD, G=G, RQ=RQ)
    grid_spec = pltpu.PrefetchScalarGridSpec(
        num_scalar_prefetch=2, grid=(B, HF // G, len(ii)),
        in_specs=[pl.BlockSpec((TB, G * LANES), qmap), pl.BlockSpec((TB, G * LANES), kmap),
                  pl.BlockSpec((TB, G * LANES), kmap),
                  pl.BlockSpec((1, G * HD), lambda b, hg, p, ii, jj: (0, hg))],
        out_specs=pl.BlockSpec((TB, G * HD), qmap),
        scratch_shapes=[pltpu.VMEM((G, TB, LANES), F32)] * 2)
    return pl.pallas_call(
        kern, grid_spec=grid_spec, out_shape=jax.ShapeDtypeStruct((M, HF * HD), BF16),
        compiler_params=_cparams(("parallel", "parallel", "arbitrary")),
        name="fox_prompt")(jnp.asarray(ii), jnp.asarray(jj), qp, kp, vp, g_f)


def _outproj_kernel(x_ref, hm_ref, hf_ref, wm_ref, wf_ref, o_ref):
    y = jnp.dot(hm_ref[...].astype(BF16), wm_ref[...], preferred_element_type=F32)
    y = y + jnp.dot(hf_ref[...].astype(BF16), wf_ref[...], preferred_element_type=F32)
    o_ref[...] = x_ref[...] + y


def _outproj(x, hm, hf, w_m, w_f, *, tm):
    M, D = x.shape
    tm = min(tm, M)
    assert M % tm == 0
    row = lambda i: (i, 0)
    const = lambda i: (0, 0)
    return pl.pallas_call(
        _outproj_kernel, grid=(M // tm,),
        in_specs=[pl.BlockSpec((tm, D), row), pl.BlockSpec((tm, hm.shape[1]), row),
                  pl.BlockSpec((tm, hf.shape[1]), row),
                  pl.BlockSpec(w_m.shape, const), pl.BlockSpec(w_f.shape, const)],
        out_specs=pl.BlockSpec((tm, D), row), out_shape=jax.ShapeDtypeStruct((M, D), F32),
        compiler_params=_cparams(("parallel",)), name="outproj")(x, hm, hf, w_m, w_f)


def _ffn_kernel(x_ref, g_ref, wg_ref, wu_ref, wd_ref, *refs, moe, n_experts, final):
    refs = list(refs)
    if moe:
        wr_ref, br_ref = refs[:2]
        refs = refs[2:]
    if final:
        gfin_ref = refs[0]
        refs = refs[1:]
    o_ref, hn_sc, acc_sc, comb_sc = refs
    e = pl.program_id(1)

    @pl.when(e == 0)
    def _():
        hn = _rms(x_ref[...], g_ref[...])
        hn_sc[...] = hn.astype(BF16)
        acc_sc[...] = jnp.zeros_like(acc_sc)
        if moe:
            lane = lax.broadcasted_iota(jnp.int32, (hn.shape[0], LANES), 1)
            lanef = lane.astype(F32)
            lg = jnp.dot(hn, wr_ref[...], preferred_element_type=F32,
                         precision=lax.Precision.HIGHEST) + br_ref[...]
            lg = jnp.where(lane < n_experts, lg, -jnp.inf)
            m1 = jnp.max(lg, axis=1, keepdims=True)
            i1 = jnp.min(jnp.where(lg == m1, lanef, float(LANES)), axis=1, keepdims=True)
            lg2 = jnp.where(lanef == i1, -jnp.inf, lg)
            m2 = jnp.max(lg2, axis=1, keepdims=True)
            i2 = jnp.min(jnp.where(lg2 == m2, lanef, float(LANES)), axis=1, keepdims=True)
            e2 = jnp.exp(m2 - m1)
            g1 = 1.0 / (1.0 + e2)
            g2 = e2 / (1.0 + e2)
            comb_sc[...] = jnp.where(lanef == i1, g1, 0.0) + jnp.where(lanef == i2, g2, 0.0)

    hn = hn_sc[...]
    a = jnp.dot(hn, wg_ref[...], preferred_element_type=F32)
    u = jnp.dot(hn, wu_ref[...], preferred_element_type=F32)
    act = (a * _sigmoid(a) * u).astype(BF16)
    y = jnp.dot(act, wd_ref[...], preferred_element_type=F32)
    if moe:
        lane = lax.broadcasted_iota(jnp.int32, comb_sc.shape, 1)
        ce = jnp.sum(jnp.where(lane == e, comb_sc[...], 0.0), axis=1, keepdims=True)
        y = ce * y
    acc_sc[...] += y

    @pl.when(e == pl.num_programs(1) - 1)
    def _():
        xo = x_ref[...] + acc_sc[...]
        if final:
            xo = _rms(xo, gfin_ref[...])
        o_ref[...] = xo


def _ffn(x, g, wg, wu, wd, router, g_final, *, tm, chunk):
    M, D = x.shape
    tm = min(tm, M)
    assert M % tm == 0
    moe = router is not None
    final = g_final is not None
    row = lambda i, e: (i, 0)
    const = lambda i, e: (0, 0)
    if moe:
        E = wg.shape[0]
        w_specs = [pl.BlockSpec((None, D, chunk), lambda i, e: (e, 0, 0)),
                   pl.BlockSpec((None, D, chunk), lambda i, e: (e, 0, 0)),
                   pl.BlockSpec((None, chunk, D), lambda i, e: (e, 0, 0))]
    else:
        assert wg.shape[1] % chunk == 0
        E = wg.shape[1] // chunk
        w_specs = [pl.BlockSpec((D, chunk), lambda i, e: (0, e)),
                   pl.BlockSpec((D, chunk), lambda i, e: (0, e)),
                   pl.BlockSpec((chunk, D), lambda i, e: (e, 0))]
    args = [x, g, wg, wu, wd]
    in_specs = [pl.BlockSpec((tm, D), row), pl.BlockSpec((1, D), const)] + w_specs
    if moe:
        args += list(router)
        in_specs += [pl.BlockSpec((D, LANES), const), pl.BlockSpec((1, LANES), const)]
    if final:
        args.append(g_final)
        in_specs.append(pl.BlockSpec((1, D), const))
    kern = functools.partial(_ffn_kernel, moe=moe, n_experts=E, final=final)
    return pl.pallas_call(
        kern, grid=(M // tm, E), in_specs=in_specs,
        out_specs=pl.BlockSpec((tm, D), row), out_shape=jax.ShapeDtypeStruct((M, D), F32),
        scratch_shapes=[pltpu.VMEM((tm, D), BF16), pltpu.VMEM((tm, D), F32), pltpu.VMEM((tm, LANES), F32)],
        compiler_params=_cparams(("parallel", "arbitrary")),
        name="ffn_moe" if moe else "ffn_dense")(*args)


def _mlstm_step_kernel(q_ref, k_ref, v_ref, o_ref, qc_ref, kc_ref, G_ref, C_ref, n_ref, m_ref, gm_ref,
                       hm_ref, Cn_ref, nn_ref, mn_ref, *, NB, H, DK, DV, c_ig, c_lf):
    lane = lax.broadcasted_iota(jnp.int32, (1, LANES), 1)
    for bb in range(NB):
        m_row = jnp.zeros((1, LANES), F32)
        for h in range(H):
            C = C_ref[bb, h]
            qc = qc_ref[h * DK:(h + 1) * DK, bb:bb + 1]
            kc = kc_ref[h * DK:(h + 1) * DK, bb:bb + 1]
            qr = q_ref[bb:bb + 1, h * DK:(h + 1) * DK]
            kr = k_ref[bb:bb + 1, h * DK:(h + 1) * DK]
            vr = v_ref[bb:bb + 1, h * DV:(h + 1) * DV]
            nr = n_ref[bb:bb + 1, h * DK:(h + 1) * DK]
            ig = G_ref[bb:bb + 1, c_ig + h:c_ig + h + 1]
            lf = G_ref[bb:bb + 1, c_lf + h:c_lf + h + 1]
            m0 = m_ref[bb:bb + 1, h:h + 1]
            inter = lf + m0
            m_new = jnp.maximum(inter, ig)
            w_inter = jnp.exp(inter - m_new)
            w_intra = jnp.exp(ig - m_new)
            s = jnp.sum(qr * kr, axis=1, keepdims=True) * w_intra
            num = w_inter * jnp.sum(qc * C, axis=0, keepdims=True) + s * vr
            den = w_inter * jnp.sum(qr * nr, axis=1, keepdims=True) + s
            hval = num / jnp.maximum(jnp.abs(den), jnp.exp(-m_new))
            Cn_ref[bb, h] = w_inter * C + (w_intra * kc) * vr
            nn_ref[bb:bb + 1, h * DK:(h + 1) * DK] = w_inter * nr + w_intra * kr
            m_row = jnp.where(lane == h, m_new, m_row)
            gate = _sigmoid(o_ref[bb:bb + 1, h * DV:(h + 1) * DV])
            hm_ref[bb:bb + 1, h * DV:(h + 1) * DV] = _rms(hval, gm_ref[:, h * DV:(h + 1) * DV]) * gate
        mn_ref[bb:bb + 1, :] = m_row


def _mlstm_step(q, k, v, o, G, state_C, state_n, state_m, layer, g_m, *, NB, H, DK, DV, c_ig, c_lf):
    DB = q.shape[0]
    NB = min(NB, DB)
    assert DB % NB == 0
    S = DB // NB
    qc = q.reshape(S, NB, H * DK).transpose(0, 2, 1)
    kc = k.reshape(S, NB, H * DK).transpose(0, 2, 1)
    row = lambda s: (s, 0)
    kern = functools.partial(_mlstm_step_kernel, NB=NB, H=H, DK=DK, DV=DV, c_ig=c_ig, c_lf=c_lf)
    return pl.pallas_call(
        kern, grid=(S,),
        in_specs=[pl.BlockSpec((NB, H * DK), row), pl.BlockSpec((NB, H * DK), row),
                  pl.BlockSpec((NB, H * DV), row), pl.BlockSpec((NB, H * DV), row),
                  pl.BlockSpec((None, H * DK, NB), lambda s: (s, 0, 0)),
                  pl.BlockSpec((None, H * DK, NB), lambda s: (s, 0, 0)),
                  pl.BlockSpec((NB, LANES), row),
                  pl.BlockSpec((None, NB, H, DK, DV), lambda s: (layer, s, 0, 0, 0)),
                  pl.BlockSpec((None, NB, H * DK), lambda s: (layer, s, 0)),
                  pl.BlockSpec((None, NB, H), lambda s: (layer, s, 0)),
                  pl.BlockSpec((1, H * DV), lambda s: (0, 0))],
        out_specs=[pl.BlockSpec((NB, H * DV), row),
                   pl.BlockSpec((NB, H, DK, DV), lambda s: (s, 0, 0, 0)),
                   pl.BlockSpec((NB, H * DK), row), pl.BlockSpec((NB, LANES), row)],
        out_shape=[jax.ShapeDtypeStruct((DB, H * DV), F32), jax.ShapeDtypeStruct((DB, H, DK, DV), F32),
                   jax.ShapeDtypeStruct((DB, H * DK), F32), jax.ShapeDtypeStruct((DB, LANES), F32)],
        compiler_params=_cparams(("parallel",)),
        name="mlstm_step")(q, k, v, o, qc, kc, G, state_C, state_n, state_m, g_m)


def _fox_step_kernel(pt_ref, q_ref, kc_ref, vc_ref, lfc_ref, gf_ref, *refs, NP, HF, HD, PAGE):
    k_refs = refs[:NP]
    v_refs = refs[NP:2 * NP]
    lf_refs = refs[2 * NP:3 * NP]
    o_ref = refs[3 * NP]
    q = q_ref[...] * LOG2E
    ones = jnp.ones((HD, LANES), F32)
    ti = lax.broadcasted_iota(jnp.int32, (PAGE, PAGE), 0)
    tj = lax.broadcasted_iota(jnp.int32, (PAGE, PAGE), 1)
    later = jnp.where(ti > tj, 1.0, 0.0)
    carry = lfc_ref[...]
    s_cur = jnp.sum(q * kc_ref[...], axis=1, keepdims=True)
    m = jnp.broadcast_to(s_cur, (HF, LANES))
    l = jnp.ones((HF, LANES), F32)
    acc = vc_ref[...]
    for p in range(NP - 1, -1, -1):
        lfT = lf_refs[p][...]
        suf = (jnp.dot(lfT, later, preferred_element_type=F32, precision=lax.Precision.HIGHEST) + carry) * LOG2E
        carry = carry + jnp.sum(lfT, axis=1, keepdims=True)
        prod = (k_refs[p][...] * q[None]).reshape(PAGE * HF, HD)
        s = jnp.dot(prod, ones, preferred_element_type=F32).reshape(PAGE, HF, LANES)
        sj = [s[j] + jnp.broadcast_to(suf[:, j:j + 1], (HF, LANES)) for j in range(PAGE)]
        pm = sj[0]
        for j in range(1, PAGE):
            pm = jnp.maximum(pm, sj[j])
        m_new = jnp.maximum(m, pm)
        alpha = jnp.exp2(m - m_new)
        l = alpha * l
        acc = alpha[:, :HD] * acc
        v = v_refs[p]
        for j in range(PAGE):
            pj = jnp.exp2(sj[j] - m_new)
            l = l + pj
            acc = acc + pj[:, :HD] * v[j]
        m = m_new
    o = acc / l[:, :HD]
    ms = jnp.mean(o * o, axis=1, keepdims=True)
    o_ref[...] = o * lax.rsqrt(ms + EPS) * gf_ref[...]


def _fox_step(q, k_cur, v_cur, lf_cur, cache_k, cache_v, cache_lfT, page_table, layer, g_f, *, HF, HD):
    DB = q.shape[0]
    NP = page_table.shape[1]
    PAGE = cache_k.shape[2]
    pt = page_table.reshape(-1)
    cur = lambda b, pt: (b, 0, 0)

    def page_map(p, nd):
        return lambda b, pt: (layer, pt[b * NP + p]) + (0,) * nd

    kv_specs = [pl.BlockSpec((None, None, PAGE, HF, HD), page_map(p, 3)) for p in range(NP)]
    lf_specs = [pl.BlockSpec((None, None, HF, PAGE), page_map(p, 2)) for p in range(NP)]
    grid_spec = pltpu.PrefetchScalarGridSpec(
        num_scalar_prefetch=1, grid=(DB,),
        in_specs=[pl.BlockSpec((None, HF, HD), cur)] * 3 + [pl.BlockSpec((None, HF, LANES), cur),
                                                            pl.BlockSpec((HF, HD), lambda b, pt: (0, 0))]
                 + kv_specs + kv_specs + lf_specs,
        out_specs=pl.BlockSpec((None, HF, HD), cur))
    kern = functools.partial(_fox_step_kernel, NP=NP, HF=HF, HD=HD, PAGE=PAGE)
    return pl.pallas_call(
        kern, grid_spec=grid_spec, out_shape=jax.ShapeDtypeStruct((DB, HF, HD), F32),
        compiler_params=_cparams(("arbitrary",)),
        name="fox_step")(pt, q, k_cur, v_cur, lf_cur, g_f,
                         *([cache_k] * NP), *([cache_v] * NP), *([cache_lfT] * NP))


def kernel(x_prompt, x_sample, state_C, state_n, state_m, cache_k, cache_v, cache_logf, page_table, g_mix_norm, w_in, b_ig, b_fg, b_ff, g_out_m, g_out_f, w_out, g_ffn_norm, w_ffn_gate, w_ffn_up, w_ffn_down, w_router, b_router, w_moe_gate, w_moe_up, w_moe_down, g_final):
    B, T, D = x_prompt.shape
    DB = x_sample.shape[0]
    assert x_sample.shape[1] == 1
    depth = w_in.shape[0]
    HM, DK, DV = state_C.shape[2:]
    PAGE, HF, HD = cache_k.shape[2:]
    n_pool = cache_k.shape[1]
    dq, dv, df = HM * DK, HM * DV, HF * HD
    assert dq % LANES == 0 and dv % LANES == 0 and df % LANES == 0 and LANES % HD == 0
    assert HF + 2 * HM <= GATE_ROWS
    dims = (dq, dv, df, HF, HM, DK, HD)
    c_ig, c_lf = HF, HF + HM
    E = w_router.shape[-1]
    chunk = w_moe_gate.shape[-1]
    M = B * T

    o = np.cumsum([0, dq, dq, dv, dv, HM, HM, df, df, df, HF]).tolist()
    w_m = w_in[:, :, o[0]:o[4]].astype(BF16)
    w_fs = w_in[:, :, o[6]:o[9]].astype(BF16)
    w_fp = jnp.pad(w_fs.reshape(depth, D, 3 * HF, HD),
                   ((0, 0), (0, 0), (0, 0), (0, LANES - HD))).reshape(depth, D, 3 * HF * LANES)
    scat = np.zeros((N_SPLIT, LANES, HF * LANES), np.float32)
    for t in range(N_SPLIT):
        scat[t, np.arange(HF), np.arange(HF) * LANES + HD + t] = 1.0
    scat = jnp.asarray(scat, BF16)
    n_gate = HF + 2 * HM
    w_gate = jnp.concatenate([w_in[:, :, o[9]:o[10]], w_in[:, :, o[4]:o[6]],
                              jnp.zeros((depth, D, LANES - n_gate), w_in.dtype)], axis=-1).astype(BF16)
    b_gate = jnp.concatenate([b_ff, b_ig, b_fg, jnp.zeros((depth, LANES - n_gate), F32)],
                             axis=-1).astype(F32).reshape(depth, 1, LANES)
    w_out16 = w_out.astype(BF16)
    wfg, wfu, wfd = w_ffn_gate.astype(BF16), w_ffn_up.astype(BF16), w_ffn_down.astype(BF16)
    wmg, wmu, wmd = w_moe_gate.astype(BF16), w_moe_up.astype(BF16), w_moe_down.astype(BF16)
    w_r = jnp.concatenate([w_router, jnp.zeros(w_router.shape[:2] + (LANES - E,), F32)], axis=-1)
    b_r = jnp.concatenate([b_router, jnp.zeros((b_router.shape[0], LANES - E), F32)],
                          axis=-1).reshape(-1, 1, LANES)
    clfT = jnp.swapaxes(cache_logf, 2, 3)
    sn = state_n.reshape(depth, DB, dq)

    xp = x_prompt.reshape(M, D)
    xs = x_sample.reshape(DB, D)
    pC, pn, pm, pk, pv, plf = [], [], [], [], [], []
    sC, sn_o, sm, sk, sv, slf = [], [], [], [], [], []
    for l in range(depth):
        gmix = g_mix_norm[l].reshape(1, D)
        gm = g_out_m[l].reshape(1, dv)
        gf = g_out_f[l].reshape(1, df)
        w_om, w_of = w_out16[l, :dv], w_out16[l, dv:]
        (qm, vm, kT, om, qp, kp, vp, kf, vf, G, Gc, GT, GcT) = _inproj(
            xp, gmix, w_m[l], w_fp[l], w_gate[l], b_gate[l], scat, dims, prompt=True, tm=512, seq_len=T)
        hm, caug, mout = _mlstm_prompt(qm, vm, kT, om, G, Gc, GT, GcT, gm, B=B, T=T, H=HM, DK=DK, DV=DV,
                                       L=min(256, T), c_ig=c_ig, c_lf=c_lf)
        hf = _fox_prompt(qp, kp, vp, gf, B=B, T=T, HF=HF, HD=HD, TB=512, G=4, RQ=256)
        xp = _outproj(xp, hm, hf, w_om, w_of, tm=512)
        pC.append(caug[..., :DV]); pn.append(caug[..., DV]); pm.append(mout[:, :HM, 0])
        pk.append(kf.reshape(B, T, HF, HD)); pv.append(vf.reshape(B, T, HF, HD))
        plf.append(G[:, :HF].reshape(B, T, HF))
        (qm, km, vm, om, qf, kf, vf, G) = _inproj(
            xs, gmix, w_m[l], w_fs[l], w_gate[l], b_gate[l], None, dims, prompt=False, tm=DB, seq_len=1)
        hm, Cn, nn, mn = _mlstm_step(qm, km, vm, om, G, state_C, sn, state_m, l, gm,
                                     NB=8, H=HM, DK=DK, DV=DV, c_ig=c_ig, c_lf=c_lf)
        heads = lambda a: a.reshape(DB, HF, HD)
        lf_rep = jnp.broadcast_to(G[:, :HF, None], (DB, HF, LANES))
        hf = _fox_step(heads(qf), heads(kf), heads(vf), lf_rep, cache_k, cache_v, clfT, page_table, l,
                       gf.reshape(HF, HD), HF=HF, HD=HD).reshape(DB, df)
        xs = _outproj(xs, hm, hf, w_om, w_of, tm=DB)
        sC.append(Cn); sn_o.append(nn.reshape(DB, HM, DK)); sm.append(mn[:, :HM])
        sk.append(kf.reshape(DB, 1, HF, HD)); sv.append(vf.reshape(DB, 1, HF, HD))
        slf.append(G[:, :HF].reshape(DB, 1, HF))
        gffn = g_ffn_norm[l].reshape(1, D)
        gfin = g_final.reshape(1, D) if l == depth - 1 else None
        j = l // 2
        if l % 2 == 0:
            args = (wfg[j], wfu[j], wfd[j], None)
        else:
            args = (wmg[j], wmu[j], wmd[j], (w_r[j], b_r[j]))
        xp = _ffn(xp, gffn, *args, gfin, tm=512, chunk=chunk)
        xs = _ffn(xs, gffn, *args, gfin, tm=DB, chunk=chunk)

    return (xp.reshape(B, T, D), xs.reshape(DB, 1, D),
            jnp.stack(pC), jnp.stack(pn), jnp.stack(pm),
            jnp.stack(pk), jnp.stack(pv), jnp.stack(plf),
            jnp.stack(sC), jnp.stack(sn_o), jnp.stack(sm),
            jnp.stack(sk), jnp.stack(sv), jnp.stack(slf))
```

```python
import functools

import numpy as np
import jax
import jax.numpy as jnp
from jax import lax
from jax.experimental import pallas as pl
from jax.experimental.pallas import tpu as pltpu

F32 = jnp.float32
BF16 = jnp.bfloat16
EPS = 1e-6
NEG_INIT = -1e30
TOP_K = 2
LANES = 128
GATE_ROWS = 16
VMEM_LIMIT = 56 * 1024 * 1024
LOG2E = 1.4426950408889634
N_SPLIT = 3


def _cparams(sem):
    return pltpu.CompilerParams(dimension_semantics=sem, vmem_limit_bytes=VMEM_LIMIT)


def _rms(x, g):
    var = jnp.mean(x * x, axis=-1, keepdims=True)
    return x * lax.rsqrt(var + EPS) * g


def _log_sigmoid(x):
    return jnp.minimum(x, 0.0) - jnp.log(1.0 + jnp.exp(-jnp.abs(x)))


def _sigmoid(x):
    return 1.0 / (1.0 + jnp.exp(-x))


def _cumsum_rows(a):
    n = a.shape[0]
    row = lax.broadcasted_iota(jnp.int32, a.shape, 0)
    k = 1
    while k < n:
        a = a + jnp.where(row >= k, pltpu.roll(a, k, 0), 0.0)
        k *= 2
    return a


def _merge_head_pairs(z, hd):
    n = z.shape[1] // (2 * LANES)
    return jnp.concatenate(
        [z[:, 2 * p * LANES:(2 * p + 1) * LANES] + pltpu.roll(z[:, (2 * p + 1) * LANES:(2 * p + 2) * LANES], hd, 1)
         for p in range(n)], axis=1)


def _inproj_kernel(x_ref, g_ref, wm_ref, wf_ref, wg_ref, bg_ref, *refs, dq, dv, df, hf, hm, hd, kscale, qscale,
                   prompt, tiles_per_seq):
    if prompt:
        (scat_ref, qm_ref, vm_ref, kT_ref, om_ref, qp_ref, kp_ref, vp_ref, kf_ref, vf_ref,
         G_ref, Gc_ref, GT_ref, GcT_ref, carry_ref) = refs
    else:
        (qm_ref, km_ref, vm_ref, om_ref, qf_ref, kf_ref, vf_ref, G_ref) = refs
    hn = _rms(x_ref[...], g_ref[...]).astype(BF16)

    def proj(w_ref, off, width):
        return jnp.dot(hn, w_ref[:, off:off + width], preferred_element_type=F32)

    zg = jnp.dot(hn, wg_ref[...], preferred_element_type=F32) + bg_ref[...]
    col = lax.broadcasted_iota(jnp.int32, zg.shape, 1)
    is_log = (col < hf) | ((col >= hf + hm) & (col < hf + 2 * hm))
    is_lin = (col >= hf) & (col < hf + hm)
    G = jnp.where(is_log, _log_sigmoid(zg), jnp.where(is_lin, zg, 0.0))
    G_ref[...] = G

    qm_ref[...] = proj(wm_ref, 0, dq).astype(qm_ref.dtype)
    km = proj(wm_ref, dq, dq) * kscale
    if prompt:
        kT_ref[...] = jnp.transpose(km).astype(kT_ref.dtype)
    else:
        km_ref[...] = km
    vm_ref[...] = proj(wm_ref, 2 * dq, dv).astype(vm_ref.dtype)
    om_ref[...] = proj(wm_ref, 2 * dq + dv, dv).astype(om_ref.dtype)
    if not prompt:
        qf_ref[...] = proj(wf_ref, 0, df) * qscale
        kf_ref[...] = proj(wf_ref, df, df)
        vf_ref[...] = proj(wf_ref, 2 * df, df)
        return

    i = pl.program_id(0)

    @pl.when(i % tiles_per_seq == 0)
    def _():
        carry_ref[...] = jnp.zeros_like(carry_ref)

    Gc = _cumsum_rows(G) + carry_ref[0:1, :]
    carry_ref[0:1, :] = Gc[-1:, :]
    Gc_ref[...] = Gc
    GT_ref[...] = jnp.transpose(G)[:GATE_ROWS, :]
    GcT_ref[...] = jnp.transpose(Gc)[:GATE_ROWS, :]

    dfp = hf * LANES
    lane = lax.broadcasted_iota(jnp.int32, (1, dfp), 1) % LANES
    zq = proj(wf_ref, 0, dfp) * (qscale * LOG2E)
    qp_ref[...] = (zq + jnp.where((lane >= hd) & (lane < hd + N_SPLIT), 1.0, 0.0)).astype(BF16)
    zk = proj(wf_ref, dfp, dfp)
    zv = proj(wf_ref, 2 * dfp, dfp)
    kf_ref[...] = jnp.transpose(_merge_head_pairs(zk, hd))
    vf_ref[...] = jnp.transpose(_merge_head_pairs(zv, hd))
    rem = Gc * (-LOG2E)
    bias = jnp.zeros_like(zk)
    for t in range(N_SPLIT):
        piece = rem.astype(BF16)
        rem = rem - piece.astype(F32)
        bias = bias + jnp.dot(piece, scat_ref[t], preferred_element_type=F32)
    kp_ref[...] = (zk + bias).astype(BF16)
    vp_ref[...] = (zv + jnp.where(lane == hd, 1.0, 0.0)).astype(BF16)


def _inproj(x, g, w_m, w_f, w_gate, b_gate, scat, dims, *, prompt, tm, seq_len):
    M, D = x.shape
    dq, dv, df, hf, hm, dk, hd = dims
    dfp = hf * LANES
    tm = min(tm, M)
    assert M % tm == 0
    if prompt:
        assert seq_len % tm == 0
    row = lambda i: (i, 0)
    colb = lambda i: (0, i)
    const = lambda i: (0, 0)
    tps = max(seq_len // tm, 1)
    seqb = lambda i: (i // tps, 0, i % tps)
    args = [x, g, w_m, w_f, w_gate, b_gate]
    in_specs = [pl.BlockSpec((tm, D), row), pl.BlockSpec((1, D), const),
                pl.BlockSpec(w_m.shape, const), pl.BlockSpec(w_f.shape, const),
                pl.BlockSpec((D, LANES), const), pl.BlockSpec((1, LANES), const)]
    if prompt:
        args.append(scat)
        in_specs.append(pl.BlockSpec(scat.shape, lambda i: (0, 0, 0)))
        out_shape = [jax.ShapeDtypeStruct((M, dq), BF16), jax.ShapeDtypeStruct((M, dv), BF16),
                     jax.ShapeDtypeStruct((dq, M), BF16), jax.ShapeDtypeStruct((M, dv), BF16),
                     jax.ShapeDtypeStruct((M, dfp), BF16), jax.ShapeDtypeStruct((M, dfp), BF16),
                     jax.ShapeDtypeStruct((M, dfp), BF16),
                     jax.ShapeDtypeStruct((M // seq_len, df, seq_len), F32),
                     jax.ShapeDtypeStruct((M // seq_len, df, seq_len), F32),
                     jax.ShapeDtypeStruct((M, LANES), F32), jax.ShapeDtypeStruct((M, LANES), F32),
                     jax.ShapeDtypeStruct((GATE_ROWS, M), F32), jax.ShapeDtypeStruct((GATE_ROWS, M), F32)]
        out_specs = [pl.BlockSpec((tm, dq), row), pl.BlockSpec((tm, dv), row),
                     pl.BlockSpec((dq, tm), colb), pl.BlockSpec((tm, dv), row),
                     pl.BlockSpec((tm, dfp), row), pl.BlockSpec((tm, dfp), row), pl.BlockSpec((tm, dfp), row),
                     pl.BlockSpec((None, df, tm), seqb), pl.BlockSpec((None, df, tm), seqb),
                     pl.BlockSpec((tm, LANES), row), pl.BlockSpec((tm, LANES), row),
                     pl.BlockSpec((GATE_ROWS, tm), colb), pl.BlockSpec((GATE_ROWS, tm), colb)]
        scratch = [pltpu.VMEM((8, LANES), F32)]
    else:
        out_shape = ([jax.ShapeDtypeStruct((M, dq), F32)] * 2 + [jax.ShapeDtypeStruct((M, dv), F32)] * 2
                     + [jax.ShapeDtypeStruct((M, df), F32)] * 3 + [jax.ShapeDtypeStruct((M, LANES), F32)])
        out_specs = ([pl.BlockSpec((tm, dq), row)] * 2 + [pl.BlockSpec((tm, dv), row)] * 2
                     + [pl.BlockSpec((tm, df), row)] * 3 + [pl.BlockSpec((tm, LANES), row)])
        scratch = []
    kern = functools.partial(_inproj_kernel, dq=dq, dv=dv, df=df, hf=hf, hm=hm, hd=hd,
                             kscale=float(dk) ** -0.5, qscale=float(hd) ** -0.5,
                             prompt=prompt, tiles_per_seq=max(seq_len // tm, 1))
    return pl.pallas_call(
        kern, grid=(M // tm,), in_specs=in_specs, out_specs=out_specs, out_shape=out_shape,
        scratch_shapes=scratch, compiler_params=_cparams(("arbitrary",)),
        name="inproj_prompt" if prompt else "inproj_sample")(*args)


def _mlstm_prompt_kernel(q_ref, v_ref, kT_ref, o_ref, G_ref, Gc_ref, GT_ref, GcT_ref, gm_ref,
                         hm_ref, caug_ref, mout_ref, bprev_ref, m_ref, *, H, DK, DV, L, c_ig, c_lf):
    c = pl.program_id(1)

    @pl.when(c == 0)
    def _():
        caug_ref[...] = jnp.zeros_like(caug_ref)
        m_ref[...] = jnp.full_like(m_ref, NEG_INIT)
        bprev_ref[...] = jnp.zeros_like(bprev_ref)

    row = lax.broadcasted_iota(jnp.int32, (L, L), 0)
    colL = lax.broadcasted_iota(jnp.int32, (L, L), 1)
    causal = row >= colL
    lane = lax.broadcasted_iota(jnp.int32, (L, LANES), 1)
    ones_col = jnp.where(lane == 0, 1.0, 0.0)
    G = G_ref[...]
    Gc = Gc_ref[...]
    GT = GT_ref[...]
    GcT = GcT_ref[...]
    for h in range(H):
        bp = bprev_ref[0:1, c_lf + h:c_lf + h + 1]
        b_col = Gc[:, c_lf + h:c_lf + h + 1] - bp
        b_row = GcT[c_lf + h:c_lf + h + 1, :] - bp
        i_row = GT[c_ig + h:c_ig + h + 1, :]
        i_col = G[:, c_ig + h:c_ig + h + 1]
        m_prev = m_ref[h:h + 1, 0:1]
        dm = jnp.where(causal, b_col - b_row + i_row, -jnp.inf)
        inter = b_col + m_prev
        m_row = jnp.maximum(inter, jnp.max(dm, axis=-1, keepdims=True))
        w_intra = jnp.exp(dm - m_row)
        w_inter = jnp.exp(inter - m_row)
        qh = q_ref[:, h * DK:(h + 1) * DK]
        kTh = kT_ref[h * DK:(h + 1) * DK, :]
        vh = v_ref[:, h * DV:(h + 1) * DV].astype(F32)
        vaug = jnp.concatenate([vh, ones_col], axis=1)
        s = jnp.dot(qh, kTh, preferred_element_type=F32) * w_intra
        intra = jnp.dot(s.astype(BF16), vaug.astype(BF16), preferred_element_type=F32)
        cst = caug_ref[h]
        inter_qc = jnp.dot(qh, cst.astype(BF16), preferred_element_type=F32)
        r = w_inter * inter_qc + intra
        num = r[:, :DV]
        den = r[:, DV:DV + 1]
        hval = num / jnp.maximum(jnp.abs(den), jnp.exp(-m_row))
        m_new = m_row[L - 1:L, :]
        decay = jnp.exp(inter[L - 1:L, :] - m_new)
        w_k = jnp.exp(b_col[L - 1:L, :] - b_col + i_col - m_new)
        upd = jnp.dot(kTh, (vaug * w_k).astype(BF16), preferred_element_type=F32)
        caug_ref[h] = decay * cst + upd
        m_ref[h:h + 1, :] = jnp.broadcast_to(m_new, (1, LANES))
        gate = _sigmoid(o_ref[:, h * DV:(h + 1) * DV].astype(F32))
        hm_ref[:, h * DV:(h + 1) * DV] = (_rms(hval, gm_ref[:, h * DV:(h + 1) * DV]) * gate).astype(hm_ref.dtype)
    bprev_ref[0:1, :] = Gc[L - 1:L, :]

    @pl.when(c == pl.num_programs(1) - 1)
    def _():
        mout_ref[...] = m_ref[...]


def _mlstm_prompt(qm, vm, kT, om, G, Gc, GT, GcT, g_m, *, B, T, H, DK, DV, L, c_ig, c_lf):
    M = B * T
    NC = T // L
    assert T % L == 0
    rowb = lambda b, c: (b * NC + c, 0)
    colb = lambda b, c: (0, b * NC + c)
    kern = functools.partial(_mlstm_prompt_kernel, H=H, DK=DK, DV=DV, L=L, c_ig=c_ig, c_lf=c_lf)
    return pl.pallas_call(
        kern, grid=(B, NC),
        in_specs=[pl.BlockSpec((L, H * DK), rowb), pl.BlockSpec((L, H * DV), rowb),
                  pl.BlockSpec((H * DK, L), colb), pl.BlockSpec((L, H * DV), rowb),
                  pl.BlockSpec((L, LANES), rowb), pl.BlockSpec((L, LANES), rowb),
                  pl.BlockSpec((GATE_ROWS, L), colb), pl.BlockSpec((GATE_ROWS, L), colb),
                  pl.BlockSpec((1, H * DV), lambda b, c: (0, 0))],
        out_specs=[pl.BlockSpec((L, H * DV), rowb),
                   pl.BlockSpec((None, H, DK, DV + LANES), lambda b, c: (b, 0, 0, 0)),
                   pl.BlockSpec((None, 8, LANES), lambda b, c: (b, 0, 0))],
        out_shape=[jax.ShapeDtypeStruct((M, H * DV), BF16),
                   jax.ShapeDtypeStruct((B, H, DK, DV + LANES), F32),
                   jax.ShapeDtypeStruct((B, 8, LANES), F32)],
        scratch_shapes=[pltpu.VMEM((8, LANES), F32), pltpu.VMEM((8, LANES), F32)],
        compiler_params=_cparams(("parallel", "arbitrary")),
        name="mlstm_prompt")(qm, vm, kT, om, G, Gc, GT, GcT, g_m)


def _fox_prompt_kernel(ii_ref, jj_ref, q_ref, k_ref, v_ref, gf_ref, o_ref, m_sc, acc_sc, *, TB, HD, G, RQ):
    p = pl.program_id(2)
    i = ii_ref[p]
    j = jj_ref[p]

    @pl.when(j == 0)
    def _():
        m_sc[...] = jnp.full_like(m_sc, -jnp.inf)
        acc_sc[...] = jnp.zeros_like(acc_sc)

    def update(masked):
        items = [(g, r) for g in range(G) for r in range(TB // RQ)]

        def qk(g, r):
            q = q_ref[r * RQ:(r + 1) * RQ, g * LANES:(g + 1) * LANES]
            k = k_ref[:, g * LANES:(g + 1) * LANES]
            return lax.dot_general(q, k, (((1,), (1,)), ((), ())), preferred_element_type=F32)

        s_next = qk(*items[0])
        for n, (g, r) in enumerate(items):
            s = s_next
            if n + 1 < len(items):
                s_next = qk(*items[n + 1])
            rows = slice(r * RQ, (r + 1) * RQ)
            v = v_ref[:, g * LANES:(g + 1) * LANES]
            if masked:
                rr = lax.broadcasted_iota(jnp.int32, (RQ, TB), 0) + r * RQ
                cc = lax.broadcasted_iota(jnp.int32, (RQ, TB), 1)
                s = jnp.where(rr >= cc, s, -jnp.inf)
            m_old = m_sc[g, rows, :]
            m_new = jnp.maximum(m_old, jnp.max(s, axis=1, keepdims=True))
            alpha = jnp.exp2(m_old - m_new)
            pr = jnp.concatenate([jnp.exp2(s[:, c * LANES:(c + 1) * LANES] - m_new).astype(BF16)
                                  for c in range(TB // LANES)], axis=1)
            acc_sc[g, rows, :] = alpha * acc_sc[g, rows, :] + jnp.dot(pr, v, preferred_element_type=F32)
            m_sc[g, rows, :] = m_new

    @pl.when(j < i)
    def _():
        update(False)

    @pl.when(j == i)
    def _():
        update(True)
        lane = lax.broadcasted_iota(jnp.int32, (1, LANES), 1)
        w = jnp.where(lane < HD, 1.0 / HD, jnp.where(lane == HD, EPS, 0.0))
        for gp in range(G // 2):
            outs = []
            for g in (2 * gp, 2 * gp + 1):
                a = acc_sc[g]
                outs.append(a * lax.rsqrt(jnp.sum(a * a * w, axis=1, keepdims=True)))
            out = jnp.where(lane < HD, outs[0], pltpu.roll(outs[1], HD, 1))
            o_ref[:, gp * LANES:(gp + 1) * LANES] = (out * gf_ref[:, gp * LANES:(gp + 1) * LANES]).astype(o_ref.dtype)


def _fox_prompt(qp, kp, vp, g_f, *, B, T, HF, HD, TB, G, RQ):
    M = B * T
    TB = min(TB, T)
    RQ = min(RQ, TB)
    assert T % TB == 0 and TB % RQ == 0 and HF % G == 0 and G % 2 == 0 and 2 * HD == LANES
    nq = T // TB
    ii = np.concatenate([np.full(i + 1, i, np.int32) for i in range(nq)])
    jj = np.concatenate([np.arange(i + 1, dtype=np.int32) for i in range(nq)])
    qmap = lambda b, hg, p, ii, jj: (b * nq + ii[p], hg)
    kmap = lambda b, hg, p, ii, jj: (b * nq + jj[p], hg)
    kern = functools.partial(_fox_prompt_kernel, TB=TB, HD=HD, G=G, RQ=RQ)
    grid_spec = pltpu.PrefetchScalarGridSpec(
        num_scalar_prefetch=2, grid=(B, HF // G, len(ii)),
        in_specs=[pl.BlockSpec((TB, G * LANES), qmap), pl.BlockSpec((TB, G * LANES), kmap),
                  pl.BlockSpec((TB, G * LANES), kmap),
                  pl.BlockSpec((1, G * HD), lambda b, hg, p, ii, jj: (0, hg))],
        out_specs=pl.BlockSpec((TB, G * HD), qmap),
        scratch_shapes=[pltpu.VMEM((G, TB, LANES), F32)] * 2)
    return pl.pallas_call(
        kern, grid_spec=grid_spec, out_shape=jax.ShapeDtypeStruct((M, HF * HD), BF16),
        compiler_params=_cparams(("parallel", "parallel", "arbitrary")),
        name="fox_prompt")(jnp.asarray(ii), jnp.asarray(jj), qp, kp, vp, g_f)


def _outproj_kernel(x_ref, hm_ref, hf_ref, wm_ref, wf_ref, o_ref):
    y = jnp.dot(hm_ref[...].astype(BF16), wm_ref[...], preferred_element_type=F32)
    y = y + jnp.dot(hf_ref[...].astype(BF16), wf_ref[...], preferred_element_type=F32)
    o_ref[...] = x_ref[...] + y


def _outproj(x, hm, hf, w_m, w_f, *, tm):
    M, D = x.shape
    tm = min(tm, M)
    assert M % tm == 0
    row = lambda i: (i, 0)
    const = lambda i: (0, 0)
    return pl.pallas_call(
        _outproj_kernel, grid=(M // tm,),
        in_specs=[pl.BlockSpec((tm, D), row), pl.BlockSpec((tm, hm.shape[1]), row),
                  pl.BlockSpec((tm, hf.shape[1]), row),
                  pl.BlockSpec(w_m.shape, const), pl.BlockSpec(w_f.shape, const)],
        out_specs=pl.BlockSpec((tm, D), row), out_shape=jax.ShapeDtypeStruct((M, D), F32),
        compiler_params=_cparams(("parallel",)), name="outproj")(x, hm, hf, w_m, w_f)


def _ffn_kernel(x_ref, g_ref, wg_ref, wu_ref, wd_ref, *refs, moe, n_experts, final):
    refs = list(refs)
    if moe:
        wr_ref, br_ref = refs[:2]
        refs = refs[2:]
    if final:
        gfin_ref = refs[0]
        refs = refs[1:]
    o_ref, hn_sc, acc_sc, comb_sc = refs
    e = pl.program_id(1)

    @pl.when(e == 0)
    def _():
        hn = _rms(x_ref[...], g_ref[...])
        hn_sc[...] = hn.astype(BF16)
        acc_sc[...] = jnp.zeros_like(acc_sc)
        if moe:
            lane = lax.broadcasted_iota(jnp.int32, (hn.shape[0], LANES), 1)
            lanef = lane.astype(F32)
            lg = jnp.dot(hn, wr_ref[...], preferred_element_type=F32,
                         precision=lax.Precision.HIGHEST) + br_ref[...]
            lg = jnp.where(lane < n_experts, lg, -jnp.inf)
            m1 = jnp.max(lg, axis=1, keepdims=True)
            i1 = jnp.min(jnp.where(lg == m1, lanef, float(LANES)), axis=1, keepdims=True)
            lg2 = jnp.where(lanef == i1, -jnp.inf, lg)
            m2 = jnp.max(lg2, axis=1, keepdims=True)
            i2 = jnp.min(jnp.where(lg2 == m2, lanef, float(LANES)), axis=1, keepdims=True)
            e2 = jnp.exp(m2 - m1)
            g1 = 1.0 / (1.0 + e2)
            g2 = e2 / (1.0 + e2)
            comb_sc[...] = jnp.where(lanef == i1, g1, 0.0) + jnp.where(lanef == i2, g2, 0.0)

    hn = hn_sc[...]
    a = jnp.dot(hn, wg_ref[...], preferred_element_type=F32)
    u = jnp.dot(hn, wu_ref[...], preferred_element_type=F32)
    act = (a * _sigmoid(a) * u).astype(BF16)
    y = jnp.dot(act, wd_ref[...], preferred_element_type=F32)
    if moe:
        lane = lax.broadcasted_iota(jnp.int32, comb_sc.shape, 1)
        ce = jnp.sum(jnp.where(lane == e, comb_sc[...], 0.0), axis=1, keepdims=True)
        y = ce * y
    acc_sc[...] += y

    @pl.when(e == pl.num_programs(1) - 1)
    def _():
        xo = x_ref[...] + acc_sc[...]
        if final:
            xo = _rms(xo, gfin_ref[...])
        o_ref[...] = xo


def _ffn(x, g, wg, wu, wd, router, g_final, *, tm, chunk):
    M, D = x.shape
    tm = min(tm, M)
    assert M % tm == 0
    moe = router is not None
    final = g_final is not None
    row = lambda i, e: (i, 0)
    const = lambda i, e: (0, 0)
    if moe:
        E = wg.shape[0]
        w_specs = [pl.BlockSpec((None, D, chunk), lambda i, e: (e, 0, 0)),
                   pl.BlockSpec((None, D, chunk), lambda i, e: (e, 0, 0)),
                   pl.BlockSpec((None, chunk, D), lambda i, e: (e, 0, 0))]
    else:
        assert wg.shape[1] % chunk == 0
        E = wg.shape[1] // chunk
        w_specs = [pl.BlockSpec((D, chunk), lambda i, e: (0, e)),
                   pl.BlockSpec((D, chunk), lambda i, e: (0, e)),
                   pl.BlockSpec((chunk, D), lambda i, e: (e, 0))]
    args = [x, g, wg, wu, wd]
    in_specs = [pl.BlockSpec((tm, D), row), pl.BlockSpec((1, D), const)] + w_specs
    if moe:
        args += list(router)
        in_specs += [pl.BlockSpec((D, LANES), const), pl.BlockSpec((1, LANES), const)]
    if final:
        args.append(g_final)
        in_specs.append(pl.BlockSpec((1, D), const))
    kern = functools.partial(_ffn_kernel, moe=moe, n_experts=E, final=final)
    return pl.pallas_call(
        kern, grid=(M // tm, E), in_specs=in_specs,
        out_specs=pl.BlockSpec((tm, D), row), out_shape=jax.ShapeDtypeStruct((M, D), F32),
        scratch_shapes=[pltpu.VMEM((tm, D), BF16), pltpu.VMEM((tm, D), F32), pltpu.VMEM((tm, LANES), F32)],
        compiler_params=_cparams(("parallel", "arbitrary")),
        name="ffn_moe" if moe else "ffn_dense")(*args)


def _mlstm_step_kernel(q_ref, k_ref, v_ref, o_ref, qc_ref, kc_ref, G_ref, C_ref, n_ref, m_ref, gm_ref,
                       hm_ref, Cn_ref, nn_ref, mn_ref, *, NB, H, DK, DV, c_ig, c_lf):
    lane = lax.broadcasted_iota(jnp.int32, (1, LANES), 1)
    for bb in range(NB):
        m_row = jnp.zeros((1, LANES), F32)
        for h in range(H):
            C = C_ref[bb, h]
            qc = qc_ref[h * DK:(h + 1) * DK, bb:bb + 1]
            kc = kc_ref[h * DK:(h + 1) * DK, bb:bb + 1]
            qr = q_ref[bb:bb + 1, h * DK:(h + 1) * DK]
            kr = k_ref[bb:bb + 1, h * DK:(h + 1) * DK]
            vr = v_ref[bb:bb + 1, h * DV:(h + 1) * DV]
            nr = n_ref[bb:bb + 1, h * DK:(h + 1) * DK]
            ig = G_ref[bb:bb + 1, c_ig + h:c_ig + h + 1]
            lf = G_ref[bb:bb + 1, c_lf + h:c_lf + h + 1]
            m0 = m_ref[bb:bb + 1, h:h + 1]
            inter = lf + m0
            m_new = jnp.maximum(inter, ig)
            w_inter = jnp.exp(inter - m_new)
            w_intra = jnp.exp(ig - m_new)
            s = jnp.sum(qr * kr, axis=1, keepdims=True) * w_intra
            num = w_inter * jnp.sum(qc * C, axis=0, keepdims=True) + s * vr
            den = w_inter * jnp.sum(qr * nr, axis=1, keepdims=True) + s
            hval = num / jnp.maximum(jnp.abs(den), jnp.exp(-m_new))
            Cn_ref[bb, h] = w_inter * C + (w_intra * kc) * vr
            nn_ref[bb:bb + 1, h * DK:(h + 1) * DK] = w_inter * nr + w_intra * kr
            m_row = jnp.where(lane == h, m_new, m_row)
            gate = _sigmoid(o_ref[bb:bb + 1, h * DV:(h + 1) * DV])
            hm_ref[bb:bb + 1, h * DV:(h + 1) * DV] = _rms(hval, gm_ref[:, h * DV:(h + 1) * DV]) * gate
        mn_ref[bb:bb + 1, :] = m_row


def _mlstm_step(q, k, v, o, G, state_C, state_n, state_m, layer, g_m, *, NB, H, DK, DV, c_ig, c_lf):
    DB = q.shape[0]
    NB = min(NB, DB)
    assert DB % NB == 0
    S = DB // NB
    qc = q.reshape(S, NB, H * DK).transpose(0, 2, 1)
    kc = k.reshape(S, NB, H * DK).transpose(0, 2, 1)
    row = lambda s: (s, 0)
    kern = functools.partial(_mlstm_step_kernel, NB=NB, H=H, DK=DK, DV=DV, c_ig=c_ig, c_lf=c_lf)
    return pl.pallas_call(
        kern, grid=(S,),
        in_specs=[pl.BlockSpec((NB, H * DK), row), pl.BlockSpec((NB, H * DK), row),
                  pl.BlockSpec((NB, H * DV), row), pl.BlockSpec((NB, H * DV), row),
                  pl.BlockSpec((None, H * DK, NB), lambda s: (s, 0, 0)),
                  pl.BlockSpec((None, H * DK, NB), lambda s: (s, 0, 0)),
                  pl.BlockSpec((NB, LANES), row),
                  pl.BlockSpec((None, NB, H, DK, DV), lambda s: (layer, s, 0, 0, 0)),
                  pl.BlockSpec((None, NB, H * DK), lambda s: (layer, s, 0)),
                  pl.BlockSpec((None, NB, H), lambda s: (layer, s, 0)),
                  pl.BlockSpec((1, H * DV), lambda s: (0, 0))],
        out_specs=[pl.BlockSpec((NB, H * DV), row),
                   pl.BlockSpec((NB, H, DK, DV), lambda s: (s, 0, 0, 0)),
                   pl.BlockSpec((NB, H * DK), row), pl.BlockSpec((NB, LANES), row)],
        out_shape=[jax.ShapeDtypeStruct((DB, H * DV), F32), jax.ShapeDtypeStruct((DB, H, DK, DV), F32),
                   jax.ShapeDtypeStruct((DB, H * DK), F32), jax.ShapeDtypeStruct((DB, LANES), F32)],
        compiler_params=_cparams(("parallel",)),
        name="mlstm_step")(q, k, v, o, qc, kc, G, state_C, state_n, state_m, g_m)


def _fox_step_kernel(pt_ref, q_ref, kc_ref, vc_ref, G_ref, gf_ref, *refs, NP, HF, HD, PAGE):
    k_refs = refs[:NP]
    v_refs = refs[NP:2 * NP]
    lf_refs = refs[2 * NP:3 * NP]
    o_ref = refs[3 * NP]
    DF = HF * HD
    sub = lax.broadcasted_iota(jnp.int32, (HF, DF), 0)
    lane = lax.broadcasted_iota(jnp.int32, (HF, DF), 1)
    diag = lane // HD == sub
    q = q_ref[...]
    qbd = jnp.where(diag, jnp.broadcast_to(q, (HF, DF)), 0.0)
    qbd16 = qbd.astype(BF16)
    sub_g = lax.broadcasted_iota(jnp.int32, (HF, LANES), 0)
    lane_g = lax.broadcasted_iota(jnp.int32, (HF, LANES), 1)
    carry = jnp.sum(jnp.where(lane_g == sub_g, jnp.broadcast_to(G_ref[...], (HF, LANES)), 0.0),
                    axis=1, keepdims=True)
    ti = lax.broadcasted_iota(jnp.int32, (PAGE, PAGE), 0)
    tj = lax.broadcasted_iota(jnp.int32, (PAGE, PAGE), 1)
    later = jnp.where(ti > tj, 1.0, 0.0)
    s_cur = jnp.sum(qbd * kc_ref[...], axis=1, keepdims=True)
    scores = [None] * NP
    mx = s_cur
    for p in range(NP - 1, -1, -1):
        lfT = lf_refs[p][...]
        suf = jnp.dot(lfT, later, preferred_element_type=F32, precision=lax.Precision.HIGHEST) + carry
        carry = carry + jnp.sum(lfT, axis=1, keepdims=True)
        kT = k_refs[p][...].astype(BF16)
        s = jnp.dot(qbd16, kT, preferred_element_type=F32) + suf
        scores[p] = s
        mx = jnp.maximum(mx, jnp.max(s, axis=1, keepdims=True))
    p_cur = jnp.exp(s_cur - mx)
    l = p_cur
    acc = p_cur * vc_ref[...]
    for p in range(NP):
        pr = jnp.exp(scores[p] - mx)
        l = l + jnp.sum(pr, axis=1, keepdims=True)
        vT = v_refs[p][...].astype(BF16)
        acc = acc + lax.dot_general(pr.astype(BF16), vT, (((1,), (1,)), ((), ())), preferred_element_type=F32)
    o = acc / l
    ms = jnp.sum(jnp.where(diag, o * o, 0.0), axis=1, keepdims=True) * (1.0 / HD)
    on = jnp.where(diag, o * lax.rsqrt(ms + EPS), 0.0)
    o_ref[...] = jnp.sum(on, axis=0, keepdims=True) * gf_ref[...]


def _fox_step(q, k_cur, v_cur, lf_cur, cache_kT, cache_vT, cache_lfT, page_table, layer, g_f, *, HF, HD):
    DB = q.shape[0]
    NP = page_table.shape[1]
    PAGE = cache_kT.shape[3]
    DF = HF * HD
    pt = page_table.reshape(-1)
    r3 = lambda a: a.reshape(DB, 1, a.shape[-1])
    cur = lambda b, pt: (b, 0, 0)

    def page_map(p):
        return lambda b, pt: (layer, pt[b * NP + p], 0, 0)

    kv_specs = [pl.BlockSpec((None, None, DF, PAGE), page_map(p)) for p in range(NP)]
    lf_specs = [pl.BlockSpec((None, None, HF, PAGE), page_map(p)) for p in range(NP)]
    grid_spec = pltpu.PrefetchScalarGridSpec(
        num_scalar_prefetch=1, grid=(DB,),
        in_specs=[pl.BlockSpec((None, 1, DF), cur)] * 3 + [pl.BlockSpec((None, 1, LANES), cur),
                                                           pl.BlockSpec((1, DF), lambda b, pt: (0, 0))]
                 + kv_specs + kv_specs + lf_specs,
        out_specs=pl.BlockSpec((None, 1, DF), cur))
    kern = functools.partial(_fox_step_kernel, NP=NP, HF=HF, HD=HD, PAGE=PAGE)
    out = pl.pallas_call(
        kern, grid_spec=grid_spec, out_shape=jax.ShapeDtypeStruct((DB, 1, DF), F32),
        compiler_params=_cparams(("arbitrary",)),
        name="fox_step")(pt, r3(q), r3(k_cur), r3(v_cur), r3(lf_cur), g_f,
                         *([cache_kT] * NP), *([cache_vT] * NP), *([cache_lfT] * NP))
    return out.reshape(DB, DF)


def kernel(x_prompt, x_sample, state_C, state_n, state_m, cache_k, cache_v, cache_logf, page_table, g_mix_norm, w_in, b_ig, b_fg, b_ff, g_out_m, g_out_f, w_out, g_ffn_norm, w_ffn_gate, w_ffn_up, w_ffn_down, w_router, b_router, w_moe_gate, w_moe_up, w_moe_down, g_final):
    B, T, D = x_prompt.shape
    DB = x_sample.shape[0]
    assert x_sample.shape[1] == 1
    depth = w_in.shape[0]
    HM, DK, DV = state_C.shape[2:]
    PAGE, HF, HD = cache_k.shape[2:]
    n_pool = cache_k.shape[1]
    dq, dv, df = HM * DK, HM * DV, HF * HD
    assert dq % LANES == 0 and dv % LANES == 0 and df % LANES == 0 and LANES % HD == 0
    assert HF + 2 * HM <= GATE_ROWS
    dims = (dq, dv, df, HF, HM, DK, HD)
    c_ig, c_lf = HF, HF + HM
    E = w_router.shape[-1]
    chunk = w_moe_gate.shape[-1]
    M = B * T

    o = np.cumsum([0, dq, dq, dv, dv, HM, HM, df, df, df, HF]).tolist()
    w_m = w_in[:, :, o[0]:o[4]].astype(BF16)
    w_fs = w_in[:, :, o[6]:o[9]].astype(BF16)
    w_fp = jnp.pad(w_fs.reshape(depth, D, 3 * HF, HD),
                   ((0, 0), (0, 0), (0, 0), (0, LANES - HD))).reshape(depth, D, 3 * HF * LANES)
    scat = np.zeros((N_SPLIT, LANES, HF * LANES), np.float32)
    for t in range(N_SPLIT):
        scat[t, np.arange(HF), np.arange(HF) * LANES + HD + t] = 1.0
    scat = jnp.asarray(scat, BF16)
    n_gate = HF + 2 * HM
    w_gate = jnp.concatenate([w_in[:, :, o[9]:o[10]], w_in[:, :, o[4]:o[6]],
                              jnp.zeros((depth, D, LANES - n_gate), w_in.dtype)], axis=-1).astype(BF16)
    b_gate = jnp.concatenate([b_ff, b_ig, b_fg, jnp.zeros((depth, LANES - n_gate), F32)],
                             axis=-1).astype(F32).reshape(depth, 1, LANES)
    w_out16 = w_out.astype(BF16)
    wfg, wfu, wfd = w_ffn_gate.astype(BF16), w_ffn_up.astype(BF16), w_ffn_down.astype(BF16)
    wmg, wmu, wmd = w_moe_gate.astype(BF16), w_moe_up.astype(BF16), w_moe_down.astype(BF16)
    w_r = jnp.concatenate([w_router, jnp.zeros(w_router.shape[:2] + (LANES - E,), F32)], axis=-1)
    b_r = jnp.concatenate([b_router, jnp.zeros((b_router.shape[0], LANES - E), F32)],
                          axis=-1).reshape(-1, 1, LANES)
    ckT = jnp.transpose(cache_k, (0, 1, 3, 4, 2)).reshape(depth, n_pool, df, PAGE)
    cvT = jnp.transpose(cache_v, (0, 1, 3, 4, 2)).reshape(depth, n_pool, df, PAGE)
    clfT = jnp.swapaxes(cache_logf, 2, 3)
    sn = state_n.reshape(depth, DB, dq)

    xp = x_prompt.reshape(M, D)
    xs = x_sample.reshape(DB, D)
    pC, pn, pm, pk, pv, plf = [], [], [], [], [], []
    sC, sn_o, sm, sk, sv, slf = [], [], [], [], [], []
    for l in range(depth):
        gmix = g_mix_norm[l].reshape(1, D)
        gm = g_out_m[l].reshape(1, dv)
        gf = g_out_f[l].reshape(1, df)
        w_om, w_of = w_out16[l, :dv], w_out16[l, dv:]
        (qm, vm, kT, om, qp, kp, vp, kf, vf, G, Gc, GT, GcT) = _inproj(
            xp, gmix, w_m[l], w_fp[l], w_gate[l], b_gate[l], scat, dims, prompt=True, tm=512, seq_len=T)
        hm, caug, mout = _mlstm_prompt(qm, vm, kT, om, G, Gc, GT, GcT, gm, B=B, T=T, H=HM, DK=DK, DV=DV,
                                       L=min(256, T), c_ig=c_ig, c_lf=c_lf)
        hf = _fox_prompt(qp, kp, vp, gf, B=B, T=T, HF=HF, HD=HD, TB=512, G=4, RQ=256)
        xp = _outproj(xp, hm, hf, w_om, w_of, tm=512)
        pC.append(caug[..., :DV]); pn.append(caug[..., DV]); pm.append(mout[:, :HM, 0])
        tok_major = lambda a: a.reshape(B, HF, HD, T).transpose(0, 3, 1, 2)
        pk.append(tok_major(kf)); pv.append(tok_major(vf))
        plf.append(G[:, :HF].reshape(B, T, HF))
        (qm, km, vm, om, qf, kf, vf, G) = _inproj(
            xs, gmix, w_m[l], w_fs[l], w_gate[l], b_gate[l], None, dims, prompt=False, tm=DB, seq_len=1)
        hm, Cn, nn, mn = _mlstm_step(qm, km, vm, om, G, state_C, sn, state_m, l, gm,
                                     NB=8, H=HM, DK=DK, DV=DV, c_ig=c_ig, c_lf=c_lf)
        hf = _fox_step(qf, kf, vf, G, ckT, cvT, clfT, page_table, l, gf, HF=HF, HD=HD)
        xs = _outproj(xs, hm, hf, w_om, w_of, tm=DB)
        sC.append(Cn); sn_o.append(nn.reshape(DB, HM, DK)); sm.append(mn[:, :HM])
        sk.append(kf.reshape(DB, 1, HF, HD)); sv.append(vf.reshape(DB, 1, HF, HD))
        slf.append(G[:, :HF].reshape(DB, 1, HF))
        gffn = g_ffn_norm[l].reshape(1, D)
        gfin = g_final.reshape(1, D) if l == depth - 1 else None
        j = l // 2
        if l % 2 == 0:
            args = (wfg[j], wfu[j], wfd[j], None)
        else:
            args = (wmg[j], wmu[j], wmd[j], (w_r[j], b_r[j]))
        xp = _ffn(xp, gffn, *args, gfin, tm=512, chunk=chunk)
        xs = _ffn(xs, gffn, *args, gfin, tm=DB, chunk=chunk)

    return (xp.reshape(B, T, D), xs.reshape(DB, 1, D),
            jnp.stack(pC), jnp.stack(pn), jnp.stack(pm),
            jnp.stack(pk), jnp.stack(pv), jnp.stack(plf),
            jnp.stack(sC), jnp.stack(sn_o), jnp.stack(sm),
            jnp.stack(sk), jnp.stack(sv), jnp.stack(slf))
```

```python
import functools

import numpy as np
import jax
import jax.numpy as jnp
from jax import lax
from jax.experimental import pallas as pl
from jax.experimental.pallas import tpu as pltpu

F32 = jnp.float32
BF16 = jnp.bfloat16
EPS = 1e-6
NEG_INIT = -1e30
TOP_K = 2
LANES = 128
GATE_ROWS = 16
VMEM_LIMIT = 56 * 1024 * 1024
LOG2E = 1.4426950408889634
N_SPLIT = 3


def _cparams(sem):
    return pltpu.CompilerParams(dimension_semantics=sem, vmem_limit_bytes=VMEM_LIMIT)


def _rms(x, g):
    var = jnp.mean(x * x, axis=-1, keepdims=True)
    return x * lax.rsqrt(var + EPS) * g


def _log_sigmoid(x):
    return jnp.minimum(x, 0.0) - jnp.log(1.0 + jnp.exp(-jnp.abs(x)))


def _sigmoid(x):
    return 1.0 / (1.0 + jnp.exp(-x))


def _cumsum_rows(a):
    n = a.shape[0]
    row = lax.broadcasted_iota(jnp.int32, a.shape, 0)
    k = 1
    while k < n:
        a = a + jnp.where(row >= k, pltpu.roll(a, k, 0), 0.0)
        k *= 2
    return a


def _spread_heads(z, hd, tail):
    lane = lax.broadcasted_iota(jnp.int32, (1, LANES), 1)
    blocks = []
    for p in range(z.shape[1] // LANES):
        a = z[:, p * LANES:(p + 1) * LANES]
        blocks.append(jnp.where(lane < hd, a, tail(2 * p)))
        blocks.append(jnp.where(lane < hd, pltpu.roll(a, hd, 1), tail(2 * p + 1)))
    return jnp.concatenate(blocks, axis=1)


def _inproj_kernel(x_ref, g_ref, wm_ref, wf_ref, wg_ref, bg_ref, *refs, dq, dv, df, hf, hm, hd, kscale, qscale,
                   prompt, tiles_per_seq):
    if prompt:
        (scat_ref, qm_ref, vm_ref, kT_ref, om_ref, qp_ref, kp_ref, vp_ref, kf_ref, vf_ref,
         G_ref, Gc_ref, GT_ref, GcT_ref, carry_ref) = refs
    else:
        (qm_ref, km_ref, vm_ref, om_ref, qf_ref, kf_ref, vf_ref, G_ref) = refs
    hn = _rms(x_ref[...], g_ref[...]).astype(BF16)

    def proj(w_ref, off, width):
        return jnp.dot(hn, w_ref[:, off:off + width], preferred_element_type=F32)

    zg = jnp.dot(hn, wg_ref[...], preferred_element_type=F32) + bg_ref[...]
    col = lax.broadcasted_iota(jnp.int32, zg.shape, 1)
    is_log = (col < hf) | ((col >= hf + hm) & (col < hf + 2 * hm))
    is_lin = (col >= hf) & (col < hf + hm)
    G = jnp.where(is_log, _log_sigmoid(zg), jnp.where(is_lin, zg, 0.0))
    G_ref[...] = G

    qm_ref[...] = proj(wm_ref, 0, dq).astype(qm_ref.dtype)
    km = proj(wm_ref, dq, dq) * kscale
    if prompt:
        kT_ref[...] = jnp.transpose(km).astype(kT_ref.dtype)
    else:
        km_ref[...] = km
    vm_ref[...] = proj(wm_ref, 2 * dq, dv).astype(vm_ref.dtype)
    om_ref[...] = proj(wm_ref, 2 * dq + dv, dv).astype(om_ref.dtype)
    if not prompt:
        qf_ref[...] = proj(wf_ref, 0, df) * qscale
        kf_ref[...] = proj(wf_ref, df, df)
        vf_ref[...] = proj(wf_ref, 2 * df, df)
        return

    i = pl.program_id(0)

    @pl.when(i % tiles_per_seq == 0)
    def _():
        carry_ref[...] = jnp.zeros_like(carry_ref)

    Gc = _cumsum_rows(G) + carry_ref[0:1, :]
    carry_ref[0:1, :] = Gc[-1:, :]
    Gc_ref[...] = Gc
    GT_ref[...] = jnp.transpose(G)[:GATE_ROWS, :]
    GcT_ref[...] = jnp.transpose(Gc)[:GATE_ROWS, :]

    lane = lax.broadcasted_iota(jnp.int32, (1, LANES), 1)
    zq = proj(wf_ref, 0, df) * (qscale * LOG2E)
    zk = proj(wf_ref, df, df)
    zv = proj(wf_ref, 2 * df, df)
    kf_ref[...] = jnp.transpose(zk)
    vf_ref[...] = jnp.transpose(zv)
    rem = Gc * (-LOG2E)
    bias = jnp.zeros((zk.shape[0], hf * LANES), F32)
    for t in range(N_SPLIT):
        piece = rem.astype(BF16)
        rem = rem - piece.astype(F32)
        bias = bias + jnp.dot(piece, scat_ref[t], preferred_element_type=F32)
    q_tail = jnp.where((lane >= hd) & (lane < hd + N_SPLIT), 1.0, 0.0)
    v_tail = jnp.where(lane == hd, 1.0, 0.0)
    qp_ref[...] = _spread_heads(zq, hd, lambda h: q_tail).astype(BF16)
    kp_ref[...] = _spread_heads(zk, hd, lambda h: bias[:, h * LANES:(h + 1) * LANES]).astype(BF16)
    vp_ref[...] = _spread_heads(zv, hd, lambda h: v_tail).astype(BF16)


def _inproj(x, g, w_m, w_f, w_gate, b_gate, scat, dims, *, prompt, tm, seq_len):
    M, D = x.shape
    dq, dv, df, hf, hm, dk, hd = dims
    dfp = hf * LANES
    tm = min(tm, M)
    assert M % tm == 0
    if prompt:
        assert seq_len % tm == 0
    row = lambda i: (i, 0)
    colb = lambda i: (0, i)
    const = lambda i: (0, 0)
    tps = max(seq_len // tm, 1)
    seqb = lambda i: (i // tps, 0, i % tps)
    args = [x, g, w_m, w_f, w_gate, b_gate]
    in_specs = [pl.BlockSpec((tm, D), row), pl.BlockSpec((1, D), const),
                pl.BlockSpec(w_m.shape, const), pl.BlockSpec(w_f.shape, const),
                pl.BlockSpec((D, LANES), const), pl.BlockSpec((1, LANES), const)]
    if prompt:
        args.append(scat)
        in_specs.append(pl.BlockSpec(scat.shape, lambda i: (0, 0, 0)))
        out_shape = [jax.ShapeDtypeStruct((M, dq), BF16), jax.ShapeDtypeStruct((M, dv), BF16),
                     jax.ShapeDtypeStruct((dq, M), BF16), jax.ShapeDtypeStruct((M, dv), BF16),
                     jax.ShapeDtypeStruct((M, dfp), BF16), jax.ShapeDtypeStruct((M, dfp), BF16),
                     jax.ShapeDtypeStruct((M, dfp), BF16),
                     jax.ShapeDtypeStruct((M // seq_len, df, seq_len), F32),
                     jax.ShapeDtypeStruct((M // seq_len, df, seq_len), F32),
                     jax.ShapeDtypeStruct((M, LANES), F32), jax.ShapeDtypeStruct((M, LANES), F32),
                     jax.ShapeDtypeStruct((GATE_ROWS, M), F32), jax.ShapeDtypeStruct((GATE_ROWS, M), F32)]
        out_specs = [pl.BlockSpec((tm, dq), row), pl.BlockSpec((tm, dv), row),
                     pl.BlockSpec((dq, tm), colb), pl.BlockSpec((tm, dv), row),
                     pl.BlockSpec((tm, dfp), row), pl.BlockSpec((tm, dfp), row), pl.BlockSpec((tm, dfp), row),
                     pl.BlockSpec((None, df, tm), seqb), pl.BlockSpec((None, df, tm), seqb),
                     pl.BlockSpec((tm, LANES), row), pl.BlockSpec((tm, LANES), row),
                     pl.BlockSpec((GATE_ROWS, tm), colb), pl.BlockSpec((GATE_ROWS, tm), colb)]
        scratch = [pltpu.VMEM((8, LANES), F32)]
    else:
        out_shape = ([jax.ShapeDtypeStruct((M, dq), F32)] * 2 + [jax.ShapeDtypeStruct((M, dv), F32)] * 2
                     + [jax.ShapeDtypeStruct((M, df), F32)] * 3 + [jax.ShapeDtypeStruct((M, LANES), F32)])
        out_specs = ([pl.BlockSpec((tm, dq), row)] * 2 + [pl.BlockSpec((tm, dv), row)] * 2
                     + [pl.BlockSpec((tm, df), row)] * 3 + [pl.BlockSpec((tm, LANES), row)])
        scratch = []
    kern = functools.partial(_inproj_kernel, dq=dq, dv=dv, df=df, hf=hf, hm=hm, hd=hd,
                             kscale=float(dk) ** -0.5, qscale=float(hd) ** -0.5,
                             prompt=prompt, tiles_per_seq=max(seq_len // tm, 1))
    return pl.pallas_call(
        kern, grid=(M // tm,), in_specs=in_specs, out_specs=out_specs, out_shape=out_shape,
        scratch_shapes=scratch, compiler_params=_cparams(("arbitrary",)),
        name="inproj_prompt" if prompt else "inproj_sample")(*args)


def _mlstm_prompt_kernel(q_ref, v_ref, kT_ref, o_ref, G_ref, Gc_ref, GT_ref, GcT_ref, gm_ref,
                         hm_ref, caug_ref, mout_ref, bprev_ref, m_ref, *, H, DK, DV, L, c_ig, c_lf):
    c = pl.program_id(1)

    @pl.when(c == 0)
    def _():
        caug_ref[...] = jnp.zeros_like(caug_ref)
        m_ref[...] = jnp.full_like(m_ref, NEG_INIT)
        bprev_ref[...] = jnp.zeros_like(bprev_ref)

    row = lax.broadcasted_iota(jnp.int32, (L, L), 0)
    colL = lax.broadcasted_iota(jnp.int32, (L, L), 1)
    causal = row >= colL
    lane = lax.broadcasted_iota(jnp.int32, (L, LANES), 1)
    ones_col = jnp.where(lane == 0, 1.0, 0.0)
    G = G_ref[...]
    Gc = Gc_ref[...]
    GT = GT_ref[...]
    GcT = GcT_ref[...]
    for h in range(H):
        bp = bprev_ref[0:1, c_lf + h:c_lf + h + 1]
        b_col = Gc[:, c_lf + h:c_lf + h + 1] - bp
        b_row = GcT[c_lf + h:c_lf + h + 1, :] - bp
        i_row = GT[c_ig + h:c_ig + h + 1, :]
        i_col = G[:, c_ig + h:c_ig + h + 1]
        m_prev = m_ref[h:h + 1, 0:1]
        dm = jnp.where(causal, b_col - b_row + i_row, -jnp.inf)
        inter = b_col + m_prev
        m_row = jnp.maximum(inter, jnp.max(dm, axis=-1, keepdims=True))
        w_intra = jnp.exp(dm - m_row)
        w_inter = jnp.exp(inter - m_row)
        qh = q_ref[:, h * DK:(h + 1) * DK]
        kTh = kT_ref[h * DK:(h + 1) * DK, :]
        vh = v_ref[:, h * DV:(h + 1) * DV].astype(F32)
        vaug = jnp.concatenate([vh, ones_col], axis=1)
        s = jnp.dot(qh, kTh, preferred_element_type=F32) * w_intra
        intra = jnp.dot(s.astype(BF16), vaug.astype(BF16), preferred_element_type=F32)
        cst = caug_ref[h]
        inter_qc = jnp.dot(qh, cst.astype(BF16), preferred_element_type=F32)
        r = w_inter * inter_qc + intra
        num = r[:, :DV]
        den = r[:, DV:DV + 1]
        hval = num / jnp.maximum(jnp.abs(den), jnp.exp(-m_row))
        m_new = m_row[L - 1:L, :]
        decay = jnp.exp(inter[L - 1:L, :] - m_new)
        w_k = jnp.exp(b_col[L - 1:L, :] - b_col + i_col - m_new)
        upd = jnp.dot(kTh, (vaug * w_k).astype(BF16), preferred_element_type=F32)
        caug_ref[h] = decay * cst + upd
        m_ref[h:h + 1, :] = jnp.broadcast_to(m_new, (1, LANES))
        gate = _sigmoid(o_ref[:, h * DV:(h + 1) * DV].astype(F32))
        hm_ref[:, h * DV:(h + 1) * DV] = (_rms(hval, gm_ref[:, h * DV:(h + 1) * DV]) * gate).astype(hm_ref.dtype)
    bprev_ref[0:1, :] = Gc[L - 1:L, :]

    @pl.when(c == pl.num_programs(1) - 1)
    def _():
        mout_ref[...] = m_ref[...]


def _mlstm_prompt(qm, vm, kT, om, G, Gc, GT, GcT, g_m, *, B, T, H, DK, DV, L, c_ig, c_lf):
    M = B * T
    NC = T // L
    assert T % L == 0
    rowb = lambda b, c: (b * NC + c, 0)
    colb = lambda b, c: (0, b * NC + c)
    kern = functools.partial(_mlstm_prompt_kernel, H=H, DK=DK, DV=DV, L=L, c_ig=c_ig, c_lf=c_lf)
    return pl.pallas_call(
        kern, grid=(B, NC),
        in_specs=[pl.BlockSpec((L, H * DK), rowb), pl.BlockSpec((L, H * DV), rowb),
                  pl.BlockSpec((H * DK, L), colb), pl.BlockSpec((L, H * DV), rowb),
                  pl.BlockSpec((L, LANES), rowb), pl.BlockSpec((L, LANES), rowb),
                  pl.BlockSpec((GATE_ROWS, L), colb), pl.BlockSpec((GATE_ROWS, L), colb),
                  pl.BlockSpec((1, H * DV), lambda b, c: (0, 0))],
        out_specs=[pl.BlockSpec((L, H * DV), rowb),
                   pl.BlockSpec((None, H, DK, DV + LANES), lambda b, c: (b, 0, 0, 0)),
                   pl.BlockSpec((None, 8, LANES), lambda b, c: (b, 0, 0))],
        out_shape=[jax.ShapeDtypeStruct((M, H * DV), BF16),
                   jax.ShapeDtypeStruct((B, H, DK, DV + LANES), F32),
                   jax.ShapeDtypeStruct((B, 8, LANES), F32)],
        scratch_shapes=[pltpu.VMEM((8, LANES), F32), pltpu.VMEM((8, LANES), F32)],
        compiler_params=_cparams(("parallel", "arbitrary")),
        name="mlstm_prompt")(qm, vm, kT, om, G, Gc, GT, GcT, g_m)


def _fox_prompt_kernel(ii_ref, jj_ref, q_ref, k_ref, v_ref, gf_ref, o_ref, m_sc, acc_sc, *, TB, HD, G, RQ):
    p = pl.program_id(2)
    i = ii_ref[p]
    j = jj_ref[p]

    @pl.when(j == 0)
    def _():
        m_sc[...] = jnp.full_like(m_sc, -jnp.inf)
        acc_sc[...] = jnp.zeros_like(acc_sc)

    def update(masked):
        items = [(g, r) for g in range(G) for r in range(TB // RQ)]

        def qk(g, r):
            q = q_ref[r * RQ:(r + 1) * RQ, g * LANES:(g + 1) * LANES]
            k = k_ref[:, g * LANES:(g + 1) * LANES]
            return lax.dot_general(q, k, (((1,), (1,)), ((), ())), preferred_element_type=F32)

        s_next = qk(*items[0])
        for n, (g, r) in enumerate(items):
            s = s_next
            if n + 1 < len(items):
                s_next = qk(*items[n + 1])
            rows = slice(r * RQ, (r + 1) * RQ)
            v = v_ref[:, g * LANES:(g + 1) * LANES]
            if masked:
                rr = lax.broadcasted_iota(jnp.int32, (RQ, TB), 0) + r * RQ
                cc = lax.broadcasted_iota(jnp.int32, (RQ, TB), 1)
                s = jnp.where(rr >= cc, s, -jnp.inf)
            m_old = m_sc[g, rows, :]
            m_new = jnp.maximum(m_old, jnp.max(s, axis=1, keepdims=True))
            alpha = jnp.exp2(m_old - m_new)
            pr = jnp.concatenate([jnp.exp2(s[:, c * LANES:(c + 1) * LANES] - m_new).astype(BF16)
                                  for c in range(TB // LANES)], axis=1)
            acc_sc[g, rows, :] = alpha * acc_sc[g, rows, :] + jnp.dot(pr, v, preferred_element_type=F32)
            m_sc[g, rows, :] = m_new

    @pl.when(j < i)
    def _():
        update(False)

    @pl.when(j == i)
    def _():
        update(True)
        lane = lax.broadcasted_iota(jnp.int32, (1, LANES), 1)
        w = jnp.where(lane < HD, 1.0 / HD, jnp.where(lane == HD, EPS, 0.0))
        for gp in range(G // 2):
            outs = []
            for g in (2 * gp, 2 * gp + 1):
                a = acc_sc[g]
                outs.append(a * lax.rsqrt(jnp.sum(a * a * w, axis=1, keepdims=True)))
            out = jnp.where(lane < HD, outs[0], pltpu.roll(outs[1], HD, 1))
            o_ref[:, gp * LANES:(gp + 1) * LANES] = (out * gf_ref[:, gp * LANES:(gp + 1) * LANES]).astype(o_ref.dtype)


def _fox_prompt(qp, kp, vp, g_f, *, B, T, HF, HD, TB, G, RQ):
    M = B * T
    TB = min(TB, T)
    RQ = min(RQ, TB)
    assert T % TB == 0 and TB % RQ == 0 and HF % G == 0 and G % 2 == 0 and 2 * HD == LANES
    nq = T // TB
    ii = np.concatenate([np.full(i + 1, i, np.int32) for i in range(nq)])
    jj = np.concatenate([np.arange(i + 1, dtype=np.int32) for i in range(nq)])
    qmap = lambda b, hg, p, ii, jj: (b * nq + ii[p], hg)
    kmap = lambda b, hg, p, ii, jj: (b * nq + jj[p], hg)
    kern = functools.partial(_fox_prompt_kernel, TB=TB, HD=HD, G=G, RQ=RQ)
    grid_spec = pltpu.PrefetchScalarGridSpec(
        num_scalar_prefetch=2, grid=(B, HF // G, len(ii)),
        in_specs=[pl.BlockSpec((TB, G * LANES), qmap), pl.BlockSpec((TB, G * LANES), kmap),
                  pl.BlockSpec((TB, G * LANES), kmap),
                  pl.BlockSpec((1, G * HD), lambda b, hg, p, ii, jj: (0, hg))],
        out_specs=pl.BlockSpec((TB, G * HD), qmap),
        scratch_shapes=[pltpu.VMEM((G, TB, LANES), F32)] * 2)
    return pl.pallas_call(
        kern, grid_spec=grid_spec, out_shape=jax.ShapeDtypeStruct((M, HF * HD), BF16),
        compiler_params=_cparams(("parallel", "parallel", "arbitrary")),
        name="fox_prompt")(jnp.asarray(ii), jnp.asarray(jj), qp, kp, vp, g_f)


def _outproj_kernel(x_ref, hm_ref, hf_ref, wm_ref, wf_ref, o_ref):
    y = jnp.dot(hm_ref[...].astype(BF16), wm_ref[...], preferred_element_type=F32)
    y = y + jnp.dot(hf_ref[...].astype(BF16), wf_ref[...], preferred_element_type=F32)
    o_ref[...] = x_ref[...] + y


def _outproj(x, hm, hf, w_m, w_f, *, tm):
    M, D = x.shape
    tm = min(tm, M)
    assert M % tm == 0
    row = lambda i: (i, 0)
    const = lambda i: (0, 0)
    return pl.pallas_call(
        _outproj_kernel, grid=(M // tm,),
        in_specs=[pl.BlockSpec((tm, D), row), pl.BlockSpec((tm, hm.shape[1]), row),
                  pl.BlockSpec((tm, hf.shape[1]), row),
                  pl.BlockSpec(w_m.shape, const), pl.BlockSpec(w_f.shape, const)],
        out_specs=pl.BlockSpec((tm, D), row), out_shape=jax.ShapeDtypeStruct((M, D), F32),
        compiler_params=_cparams(("parallel",)), name="outproj")(x, hm, hf, w_m, w_f)


def _ffn_kernel(x_ref, g_ref, wg_ref, wu_ref, wd_ref, *refs, moe, n_experts, final):
    refs = list(refs)
    if moe:
        wr_ref, br_ref = refs[:2]
        refs = refs[2:]
    if final:
        gfin_ref = refs[0]
        refs = refs[1:]
    o_ref, hn_sc, acc_sc, comb_sc = refs
    e = pl.program_id(1)

    @pl.when(e == 0)
    def _():
        hn = _rms(x_ref[...], g_ref[...])
        hn_sc[...] = hn.astype(BF16)
        acc_sc[...] = jnp.zeros_like(acc_sc)
        if moe:
            lane = lax.broadcasted_iota(jnp.int32, (hn.shape[0], LANES), 1)
            lanef = lane.astype(F32)
            lg = jnp.dot(hn, wr_ref[...], preferred_element_type=F32,
                         precision=lax.Precision.HIGHEST) + br_ref[...]
            lg = jnp.where(lane < n_experts, lg, -jnp.inf)
            m1 = jnp.max(lg, axis=1, keepdims=True)
            i1 = jnp.min(jnp.where(lg == m1, lanef, float(LANES)), axis=1, keepdims=True)
            lg2 = jnp.where(lanef == i1, -jnp.inf, lg)
            m2 = jnp.max(lg2, axis=1, keepdims=True)
            i2 = jnp.min(jnp.where(lg2 == m2, lanef, float(LANES)), axis=1, keepdims=True)
            e2 = jnp.exp(m2 - m1)
            g1 = 1.0 / (1.0 + e2)
            g2 = e2 / (1.0 + e2)
            comb_sc[...] = jnp.where(lanef == i1, g1, 0.0) + jnp.where(lanef == i2, g2, 0.0)

    hn = hn_sc[...]
    a = jnp.dot(hn, wg_ref[...], preferred_element_type=F32)
    u = jnp.dot(hn, wu_ref[...], preferred_element_type=F32)
    act = (a * _sigmoid(a) * u).astype(BF16)
    y = jnp.dot(act, wd_ref[...], preferred_element_type=F32)
    if moe:
        lane = lax.broadcasted_iota(jnp.int32, comb_sc.shape, 1)
        ce = jnp.sum(jnp.where(lane == e, comb_sc[...], 0.0), axis=1, keepdims=True)
        y = ce * y
    acc_sc[...] += y

    @pl.when(e == pl.num_programs(1) - 1)
    def _():
        xo = x_ref[...] + acc_sc[...]
        if final:
            xo = _rms(xo, gfin_ref[...])
        o_ref[...] = xo


def _ffn(x, g, wg, wu, wd, router, g_final, *, tm, chunk):
    M, D = x.shape
    tm = min(tm, M)
    assert M % tm == 0
    moe = router is not None
    final = g_final is not None
    row = lambda i, e: (i, 0)
    const = lambda i, e: (0, 0)
    if moe:
        E = wg.shape[0]
        w_specs = [pl.BlockSpec((None, D, chunk), lambda i, e: (e, 0, 0)),
                   pl.BlockSpec((None, D, chunk), lambda i, e: (e, 0, 0)),
                   pl.BlockSpec((None, chunk, D), lambda i, e: (e, 0, 0))]
    else:
        assert wg.shape[1] % chunk == 0
        E = wg.shape[1] // chunk
        w_specs = [pl.BlockSpec((D, chunk), lambda i, e: (0, e)),
                   pl.BlockSpec((D, chunk), lambda i, e: (0, e)),
                   pl.BlockSpec((chunk, D), lambda i, e: (e, 0))]
    args = [x, g, wg, wu, wd]
    in_specs = [pl.BlockSpec((tm, D), row), pl.BlockSpec((1, D), const)] + w_specs
    if moe:
        args += list(router)
        in_specs += [pl.BlockSpec((D, LANES), const), pl.BlockSpec((1, LANES), const)]
    if final:
        args.append(g_final)
        in_specs.append(pl.BlockSpec((1, D), const))
    kern = functools.partial(_ffn_kernel, moe=moe, n_experts=E, final=final)
    return pl.pallas_call(
        kern, grid=(M // tm, E), in_specs=in_specs,
        out_specs=pl.BlockSpec((tm, D), row), out_shape=jax.ShapeDtypeStruct((M, D), F32),
        scratch_shapes=[pltpu.VMEM((tm, D), BF16), pltpu.VMEM((tm, D), F32), pltpu.VMEM((tm, LANES), F32)],
        compiler_params=_cparams(("parallel", "arbitrary")),
        name="ffn_moe" if moe else "ffn_dense")(*args)


def _mlstm_step_kernel(q_ref, k_ref, v_ref, o_ref, kc_ref, G_ref, C_ref, n_ref, m_ref, gm_ref,
                       hm_ref, Cn_ref, nn_ref, mn_ref, *, NB, H, DK, DV, c_ig, c_lf):
    sub = lax.broadcasted_iota(jnp.int32, (NB, DV), 0)
    lane = lax.broadcasted_iota(jnp.int32, (NB, LANES), 1)
    m_out = jnp.zeros((NB, LANES), F32)
    for h in range(H):
        q = q_ref[:, h * DK:(h + 1) * DK]
        k = k_ref[:, h * DK:(h + 1) * DK]
        v = v_ref[:, h * DV:(h + 1) * DV]
        n = n_ref[:, h * DK:(h + 1) * DK]
        ig = G_ref[:, c_ig + h:c_ig + h + 1]
        lf = G_ref[:, c_lf + h:c_lf + h + 1]
        m0 = m_ref[:, h:h + 1]
        inter = lf + m0
        m_new = jnp.maximum(inter, ig)
        w_inter = jnp.exp(inter - m_new)
        w_intra = jnp.exp(ig - m_new)
        s = jnp.sum(q * k, axis=1, keepdims=True) * w_intra
        qn = jnp.sum(q * n, axis=1, keepdims=True)
        vw = v * w_intra
        kc = kc_ref[h * DK:(h + 1) * DK, :]
        qC = jnp.zeros((NB, DV), F32)
        for bb in range(NB):
            C = C_ref[bb, h]
            qC = jnp.where(sub == bb, jnp.dot(q, C, preferred_element_type=F32), qC)
            vsel = jnp.concatenate([jnp.where(sub == bb, vw, 0.0), jnp.zeros((LANES - NB, DV), F32)], axis=0)
            Cn_ref[bb, h] = w_inter[bb:bb + 1, :] * C + jnp.dot(kc, vsel, preferred_element_type=F32)
        num = w_inter * qC + s * v
        den = w_inter * qn + s
        hval = num / jnp.maximum(jnp.abs(den), jnp.exp(-m_new))
        nn_ref[:, h * DK:(h + 1) * DK] = w_inter * n + w_intra * k
        m_out = jnp.where(lane == h, m_new, m_out)
        gate = _sigmoid(o_ref[:, h * DV:(h + 1) * DV])
        hm_ref[:, h * DV:(h + 1) * DV] = _rms(hval, gm_ref[:, h * DV:(h + 1) * DV]) * gate
    mn_ref[...] = m_out


def _mlstm_step(q, k, v, o, G, state_C, state_n, state_m, layer, g_m, *, NB, H, DK, DV, c_ig, c_lf):
    DB = q.shape[0]
    NB = min(NB, DB)
    assert DB % NB == 0 and NB <= LANES
    S = DB // NB
    kc = jnp.pad(k.reshape(S, NB, H * DK).transpose(0, 2, 1), ((0, 0), (0, 0), (0, LANES - NB)))
    row = lambda s: (s, 0)
    kern = functools.partial(_mlstm_step_kernel, NB=NB, H=H, DK=DK, DV=DV, c_ig=c_ig, c_lf=c_lf)
    return pl.pallas_call(
        kern, grid=(S,),
        in_specs=[pl.BlockSpec((NB, H * DK), row), pl.BlockSpec((NB, H * DK), row),
                  pl.BlockSpec((NB, H * DV), row), pl.BlockSpec((NB, H * DV), row),
                  pl.BlockSpec((None, H * DK, LANES), lambda s: (s, 0, 0)),
                  pl.BlockSpec((NB, LANES), row),
                  pl.BlockSpec((None, NB, H, DK, DV), lambda s: (layer, s, 0, 0, 0)),
                  pl.BlockSpec((None, NB, H * DK), lambda s: (layer, s, 0)),
                  pl.BlockSpec((None, NB, H), lambda s: (layer, s, 0)),
                  pl.BlockSpec((1, H * DV), lambda s: (0, 0))],
        out_specs=[pl.BlockSpec((NB, H * DV), row),
                   pl.BlockSpec((NB, H, DK, DV), lambda s: (s, 0, 0, 0)),
                   pl.BlockSpec((NB, H * DK), row), pl.BlockSpec((NB, LANES), row)],
        out_shape=[jax.ShapeDtypeStruct((DB, H * DV), F32), jax.ShapeDtypeStruct((DB, H, DK, DV), F32),
                   jax.ShapeDtypeStruct((DB, H * DK), F32), jax.ShapeDtypeStruct((DB, LANES), F32)],
        compiler_params=_cparams(("parallel",)),
        name="mlstm_step")(q, k, v, o, kc, G, state_C, state_n, state_m, g_m)


def _fox_step_kernel(pt_ref, q_ref, kc_ref, vc_ref, G_ref, gf_ref, *refs, NP, HF, HD, PAGE):
    k_refs = refs[:NP]
    v_refs = refs[NP:2 * NP]
    lf_refs = refs[2 * NP:3 * NP]
    o_ref = refs[3 * NP]
    DF = HF * HD
    sub = lax.broadcasted_iota(jnp.int32, (HF, DF), 0)
    lane = lax.broadcasted_iota(jnp.int32, (HF, DF), 1)
    diag = lane // HD == sub
    q = q_ref[...]
    qbd = jnp.where(diag, jnp.broadcast_to(q, (HF, DF)), 0.0)
    qbd16 = qbd.astype(BF16)
    sub_g = lax.broadcasted_iota(jnp.int32, (HF, LANES), 0)
    lane_g = lax.broadcasted_iota(jnp.int32, (HF, LANES), 1)
    carry = jnp.sum(jnp.where(lane_g == sub_g, jnp.broadcast_to(G_ref[...], (HF, LANES)), 0.0),
                    axis=1, keepdims=True)
    ti = lax.broadcasted_iota(jnp.int32, (PAGE, PAGE), 0)
    tj = lax.broadcasted_iota(jnp.int32, (PAGE, PAGE), 1)
    later = jnp.where(ti > tj, 1.0, 0.0)
    s_cur = jnp.sum(qbd * kc_ref[...], axis=1, keepdims=True)
    scores = [None] * NP
    mx = s_cur
    for p in range(NP - 1, -1, -1):
        lfT = lf_refs[p][...]
        suf = jnp.dot(lfT, later, preferred_element_type=F32, precision=lax.Precision.HIGHEST) + carry
        carry = carry + jnp.sum(lfT, axis=1, keepdims=True)
        kT = k_refs[p][...].astype(BF16)
        s = jnp.dot(qbd16, kT, preferred_element_type=F32) + suf
        scores[p] = s
        mx = jnp.maximum(mx, jnp.max(s, axis=1, keepdims=True))
    p_cur = jnp.exp(s_cur - mx)
    l = p_cur
    acc = p_cur * vc_ref[...]
    for p in range(NP):
        pr = jnp.exp(scores[p] - mx)
        l = l + jnp.sum(pr, axis=1, keepdims=True)
        vT = v_refs[p][...].astype(BF16)
        acc = acc + lax.dot_general(pr.astype(BF16), vT, (((1,), (1,)), ((), ())), preferred_element_type=F32)
    o = acc / l
    ms = jnp.sum(jnp.where(diag, o * o, 0.0), axis=1, keepdims=True) * (1.0 / HD)
    on = jnp.where(diag, o * lax.rsqrt(ms + EPS), 0.0)
    o_ref[...] = jnp.sum(on, axis=0, keepdims=True) * gf_ref[...]


def _fox_step(q, k_cur, v_cur, lf_cur, cache_kT, cache_vT, cache_lfT, page_table, layer, g_f, *, HF, HD):
    DB = q.shape[0]
    NP = page_table.shape[1]
    PAGE = cache_kT.shape[3]
    DF = HF * HD
    pt = page_table.reshape(-1)
    r3 = lambda a: a.reshape(DB, 1, a.shape[-1])
    cur = lambda b, pt: (b, 0, 0)

    def page_map(p):
        return lambda b, pt: (layer, pt[b * NP + p], 0, 0)

    kv_specs = [pl.BlockSpec((None, None, DF, PAGE), page_map(p)) for p in range(NP)]
    lf_specs = [pl.BlockSpec((None, None, HF, PAGE), page_map(p)) for p in range(NP)]
    grid_spec = pltpu.PrefetchScalarGridSpec(
        num_scalar_prefetch=1, grid=(DB,),
        in_specs=[pl.BlockSpec((None, 1, DF), cur)] * 3 + [pl.BlockSpec((None, 1, LANES), cur),
                                                           pl.BlockSpec((1, DF), lambda b, pt: (0, 0))]
                 + kv_specs + kv_specs + lf_specs,
        out_specs=pl.BlockSpec((None, 1, DF), cur))
    kern = functools.partial(_fox_step_kernel, NP=NP, HF=HF, HD=HD, PAGE=PAGE)
    out = pl.pallas_call(
        kern, grid_spec=grid_spec, out_shape=jax.ShapeDtypeStruct((DB, 1, DF), F32),
        compiler_params=_cparams(("arbitrary",)),
        name="fox_step")(pt, r3(q), r3(k_cur), r3(v_cur), r3(lf_cur), g_f,
                         *([cache_kT] * NP), *([cache_vT] * NP), *([cache_lfT] * NP))
    return out.reshape(DB, DF)


def kernel(x_prompt, x_sample, state_C, state_n, state_m, cache_k, cache_v, cache_logf, page_table, g_mix_norm, w_in, b_ig, b_fg, b_ff, g_out_m, g_out_f, w_out, g_ffn_norm, w_ffn_gate, w_ffn_up, w_ffn_down, w_router, b_router, w_moe_gate, w_moe_up, w_moe_down, g_final):
    B, T, D = x_prompt.shape
    DB = x_sample.shape[0]
    assert x_sample.shape[1] == 1
    depth = w_in.shape[0]
    HM, DK, DV = state_C.shape[2:]
    PAGE, HF, HD = cache_k.shape[2:]
    n_pool = cache_k.shape[1]
    dq, dv, df = HM * DK, HM * DV, HF * HD
    assert dq % LANES == 0 and dv % LANES == 0 and df % LANES == 0 and LANES % HD == 0
    assert HF + 2 * HM <= GATE_ROWS
    dims = (dq, dv, df, HF, HM, DK, HD)
    c_ig, c_lf = HF, HF + HM
    E = w_router.shape[-1]
    chunk = w_moe_gate.shape[-1]
    M = B * T

    o = np.cumsum([0, dq, dq, dv, dv, HM, HM, df, df, df, HF]).tolist()
    w_m = w_in[:, :, o[0]:o[4]].astype(BF16)
    w_f = w_in[:, :, o[6]:o[9]].astype(BF16)
    scat = np.zeros((N_SPLIT, LANES, HF * LANES), np.float32)
    for t in range(N_SPLIT):
        scat[t, np.arange(HF), np.arange(HF) * LANES + HD + t] = 1.0
    scat = jnp.asarray(scat, BF16)
    n_gate = HF + 2 * HM
    w_gate = jnp.concatenate([w_in[:, :, o[9]:o[10]], w_in[:, :, o[4]:o[6]],
                              jnp.zeros((depth, D, LANES - n_gate), w_in.dtype)], axis=-1).astype(BF16)
    b_gate = jnp.concatenate([b_ff, b_ig, b_fg, jnp.zeros((depth, LANES - n_gate), F32)],
                             axis=-1).astype(F32).reshape(depth, 1, LANES)
    w_out16 = w_out.astype(BF16)
    wfg, wfu, wfd = w_ffn_gate.astype(BF16), w_ffn_up.astype(BF16), w_ffn_down.astype(BF16)
    wmg, wmu, wmd = w_moe_gate.astype(BF16), w_moe_up.astype(BF16), w_moe_down.astype(BF16)
    w_r = jnp.concatenate([w_router, jnp.zeros(w_router.shape[:2] + (LANES - E,), F32)], axis=-1)
    b_r = jnp.concatenate([b_router, jnp.zeros((b_router.shape[0], LANES - E), F32)],
                          axis=-1).reshape(-1, 1, LANES)
    ckT = jnp.transpose(cache_k, (0, 1, 3, 4, 2)).reshape(depth, n_pool, df, PAGE)
    cvT = jnp.transpose(cache_v, (0, 1, 3, 4, 2)).reshape(depth, n_pool, df, PAGE)
    clfT = jnp.swapaxes(cache_logf, 2, 3)
    sn = state_n.reshape(depth, DB, dq)

    xp = x_prompt.reshape(M, D)
    xs = x_sample.reshape(DB, D)
    pC, pn, pm, pk, pv, plf = [], [], [], [], [], []
    sC, sn_o, sm, sk, sv, slf = [], [], [], [], [], []
    for l in range(depth):
        gmix = g_mix_norm[l].reshape(1, D)
        gm = g_out_m[l].reshape(1, dv)
        gf = g_out_f[l].reshape(1, df)
        w_om, w_of = w_out16[l, :dv], w_out16[l, dv:]
        (qm, vm, kT, om, qp, kp, vp, kf, vf, G, Gc, GT, GcT) = _inproj(
            xp, gmix, w_m[l], w_f[l], w_gate[l], b_gate[l], scat, dims, prompt=True, tm=512, seq_len=T)
        hm, caug, mout = _mlstm_prompt(qm, vm, kT, om, G, Gc, GT, GcT, gm, B=B, T=T, H=HM, DK=DK, DV=DV,
                                       L=min(512, T), c_ig=c_ig, c_lf=c_lf)
        hf = _fox_prompt(qp, kp, vp, gf, B=B, T=T, HF=HF, HD=HD, TB=512, G=HF, RQ=256)
        xp = _outproj(xp, hm, hf, w_om, w_of, tm=512)
        pC.append(caug[..., :DV]); pn.append(caug[..., DV]); pm.append(mout[:, :HM, 0])
        tok_major = lambda a: a.reshape(B, HF, HD, T).transpose(0, 3, 1, 2)
        pk.append(tok_major(kf)); pv.append(tok_major(vf))
        plf.append(G[:, :HF].reshape(B, T, HF))
        (qm, km, vm, om, qf, kf, vf, G) = _inproj(
            xs, gmix, w_m[l], w_f[l], w_gate[l], b_gate[l], None, dims, prompt=False, tm=DB, seq_len=1)
        hm, Cn, nn, mn = _mlstm_step(qm, km, vm, om, G, state_C, sn, state_m, l, gm,
                                     NB=8, H=HM, DK=DK, DV=DV, c_ig=c_ig, c_lf=c_lf)
        hf = _fox_step(qf, kf, vf, G, ckT, cvT, clfT, page_table, l, gf, HF=HF, HD=HD)
        xs = _outproj(xs, hm, hf, w_om, w_of, tm=DB)
        sC.append(Cn); sn_o.append(nn.reshape(DB, HM, DK)); sm.append(mn[:, :HM])
        sk.append(kf.reshape(DB, 1, HF, HD)); sv.append(vf.reshape(DB, 1, HF, HD))
        slf.append(G[:, :HF].reshape(DB, 1, HF))
        gffn = g_ffn_norm[l].reshape(1, D)
        gfin = g_final.reshape(1, D) if l == depth - 1 else None
        j = l // 2
        if l % 2 == 0:
            args = (wfg[j], wfu[j], wfd[j], None)
        else:
            args = (wmg[j], wmu[j], wmd[j], (w_r[j], b_r[j]))
        xp = _ffn(xp, gffn, *args, gfin, tm=512, chunk=chunk)
        xs = _ffn(xs, gffn, *args, gfin, tm=DB, chunk=chunk)

    return (xp.reshape(B, T, D), xs.reshape(DB, 1, D),
            jnp.stack(pC), jnp.stack(pn), jnp.stack(pm),
            jnp.stack(pk), jnp.stack(pv), jnp.stack(plf),
            jnp.stack(sC), jnp.stack(sn_o), jnp.stack(sm),
            jnp.stack(sk), jnp.stack(sv), jnp.stack(slf))
```

```python
import functools

import numpy as np
import jax
import jax.numpy as jnp
from jax import lax
from jax.experimental import pallas as pl
from jax.experimental.pallas import tpu as pltpu

F32 = jnp.float32
BF16 = jnp.bfloat16
EPS = 1e-6
NEG_INIT = -1e30
TOP_K = 2
LANES = 128
GATE_ROWS = 16
VMEM_LIMIT = 56 * 1024 * 1024
LOG2E = 1.4426950408889634
N_SPLIT = 3


def _cparams(sem):
    return pltpu.CompilerParams(dimension_semantics=sem, vmem_limit_bytes=VMEM_LIMIT)


def _rms(x, g):
    var = jnp.mean(x * x, axis=-1, keepdims=True)
    return x * lax.rsqrt(var + EPS) * g


def _log_sigmoid(x):
    return jnp.minimum(x, 0.0) - jnp.log(1.0 + jnp.exp(-jnp.abs(x)))


def _sigmoid(x):
    return 1.0 / (1.0 + jnp.exp(-x))


def _cumsum_rows(a):
    n = a.shape[0]
    row = lax.broadcasted_iota(jnp.int32, a.shape, 0)
    k = 1
    while k < n:
        a = a + jnp.where(row >= k, pltpu.roll(a, k, 0), 0.0)
        k *= 2
    return a


def _spread_heads(z, hd, tail):
    lane = lax.broadcasted_iota(jnp.int32, (1, LANES), 1)
    blocks = []
    for p in range(z.shape[1] // LANES):
        a = z[:, p * LANES:(p + 1) * LANES]
        blocks.append(jnp.where(lane < hd, a, tail(2 * p)))
        blocks.append(jnp.where(lane < hd, pltpu.roll(a, hd, 1), tail(2 * p + 1)))
    return jnp.concatenate(blocks, axis=1)


def _inproj_kernel(x_ref, g_ref, wm_ref, wf_ref, wg_ref, bg_ref, *refs, dq, dv, df, hf, hm, hd, kscale, qscale,
                   prompt, tiles_per_seq):
    if prompt:
        (scat_ref, qm_ref, vm_ref, kT_ref, om_ref, qp_ref, kp_ref, vp_ref, kf_ref, vf_ref,
         G_ref, Gc_ref, GT_ref, GcT_ref, carry_ref) = refs
    else:
        (qm_ref, km_ref, vm_ref, om_ref, qf_ref, kf_ref, vf_ref, G_ref) = refs
    hn = _rms(x_ref[...], g_ref[...]).astype(BF16)

    def proj(w_ref, off, width):
        return jnp.dot(hn, w_ref[:, off:off + width], preferred_element_type=F32)

    zg = jnp.dot(hn, wg_ref[...], preferred_element_type=F32) + bg_ref[...]
    col = lax.broadcasted_iota(jnp.int32, zg.shape, 1)
    is_log = (col < hf) | ((col >= hf + hm) & (col < hf + 2 * hm))
    is_lin = (col >= hf) & (col < hf + hm)
    G = jnp.where(is_log, _log_sigmoid(zg), jnp.where(is_lin, zg, 0.0))
    G_ref[...] = G

    qm_ref[...] = proj(wm_ref, 0, dq).astype(qm_ref.dtype)
    km = proj(wm_ref, dq, dq) * kscale
    if prompt:
        kT_ref[...] = jnp.transpose(km).astype(kT_ref.dtype)
    else:
        km_ref[...] = km
    vm_ref[...] = proj(wm_ref, 2 * dq, dv).astype(vm_ref.dtype)
    om_ref[...] = proj(wm_ref, 2 * dq + dv, dv).astype(om_ref.dtype)
    if not prompt:
        qf_ref[...] = proj(wf_ref, 0, df) * qscale
        kf_ref[...] = proj(wf_ref, df, df)
        vf_ref[...] = proj(wf_ref, 2 * df, df)
        return

    i = pl.program_id(0)

    @pl.when(i % tiles_per_seq == 0)
    def _():
        carry_ref[...] = jnp.zeros_like(carry_ref)

    Gc = _cumsum_rows(G) + carry_ref[0:1, :]
    carry_ref[0:1, :] = Gc[-1:, :]
    Gc_ref[...] = Gc
    GT_ref[...] = jnp.transpose(G)[:GATE_ROWS, :]
    GcT_ref[...] = jnp.transpose(Gc)[:GATE_ROWS, :]

    lane = lax.broadcasted_iota(jnp.int32, (1, LANES), 1)
    zq = proj(wf_ref, 0, df) * (qscale * LOG2E)
    zk = proj(wf_ref, df, df)
    zv = proj(wf_ref, 2 * df, df)
    kf_ref[...] = jnp.transpose(zk)
    vf_ref[...] = jnp.transpose(zv)
    rem = Gc * (-LOG2E)
    bias = jnp.zeros((zk.shape[0], hf * LANES), F32)
    for t in range(N_SPLIT):
        piece = rem.astype(BF16)
        rem = rem - piece.astype(F32)
        bias = bias + jnp.dot(piece, scat_ref[t], preferred_element_type=F32)
    q_tail = jnp.where((lane >= hd) & (lane < hd + N_SPLIT), 1.0, 0.0)
    v_tail = jnp.where(lane == hd, 1.0, 0.0)
    qp_ref[...] = _spread_heads(zq, hd, lambda h: q_tail).astype(BF16)
    kp_ref[...] = _spread_heads(zk, hd, lambda h: bias[:, h * LANES:(h + 1) * LANES]).astype(BF16)
    vp_ref[...] = _spread_heads(zv, hd, lambda h: v_tail).astype(BF16)


def _inproj(x, g, w_m, w_f, w_gate, b_gate, scat, dims, *, prompt, tm, seq_len):
    M, D = x.shape
    dq, dv, df, hf, hm, dk, hd = dims
    dfp = hf * LANES
    tm = min(tm, M)
    assert M % tm == 0
    if prompt:
        assert seq_len % tm == 0
    row = lambda i: (i, 0)
    colb = lambda i: (0, i)
    const = lambda i: (0, 0)
    tps = max(seq_len // tm, 1)
    seqb = lambda i: (i // tps, 0, i % tps)
    args = [x, g, w_m, w_f, w_gate, b_gate]
    in_specs = [pl.BlockSpec((tm, D), row), pl.BlockSpec((1, D), const),
                pl.BlockSpec(w_m.shape, const), pl.BlockSpec(w_f.shape, const),
                pl.BlockSpec((D, LANES), const), pl.BlockSpec((1, LANES), const)]
    if prompt:
        args.append(scat)
        in_specs.append(pl.BlockSpec(scat.shape, lambda i: (0, 0, 0)))
        out_shape = [jax.ShapeDtypeStruct((M, dq), BF16), jax.ShapeDtypeStruct((M, dv), BF16),
                     jax.ShapeDtypeStruct((dq, M), BF16), jax.ShapeDtypeStruct((M, dv), BF16),
                     jax.ShapeDtypeStruct((M, dfp), BF16), jax.ShapeDtypeStruct((M, dfp), BF16),
                     jax.ShapeDtypeStruct((M, dfp), BF16),
                     jax.ShapeDtypeStruct((M // seq_len, df, seq_len), F32),
                     jax.ShapeDtypeStruct((M // seq_len, df, seq_len), F32),
                     jax.ShapeDtypeStruct((M, LANES), F32), jax.ShapeDtypeStruct((M, LANES), F32),
                     jax.ShapeDtypeStruct((GATE_ROWS, M), F32), jax.ShapeDtypeStruct((GATE_ROWS, M), F32)]
        out_specs = [pl.BlockSpec((tm, dq), row), pl.BlockSpec((tm, dv), row),
                     pl.BlockSpec((dq, tm), colb), pl.BlockSpec((tm, dv), row),
                     pl.BlockSpec((tm, dfp), row), pl.BlockSpec((tm, dfp), row), pl.BlockSpec((tm, dfp), row),
                     pl.BlockSpec((None, df, tm), seqb), pl.BlockSpec((None, df, tm), seqb),
                     pl.BlockSpec((tm, LANES), row), pl.BlockSpec((tm, LANES), row),
                     pl.BlockSpec((GATE_ROWS, tm), colb), pl.BlockSpec((GATE_ROWS, tm), colb)]
        scratch = [pltpu.VMEM((8, LANES), F32)]
    else:
        out_shape = ([jax.ShapeDtypeStruct((M, dq), F32)] * 2 + [jax.ShapeDtypeStruct((M, dv), F32)] * 2
                     + [jax.ShapeDtypeStruct((M, df), F32)] * 3 + [jax.ShapeDtypeStruct((M, LANES), F32)])
        out_specs = ([pl.BlockSpec((tm, dq), row)] * 2 + [pl.BlockSpec((tm, dv), row)] * 2
                     + [pl.BlockSpec((tm, df), row)] * 3 + [pl.BlockSpec((tm, LANES), row)])
        scratch = []
    kern = functools.partial(_inproj_kernel, dq=dq, dv=dv, df=df, hf=hf, hm=hm, hd=hd,
                             kscale=float(dk) ** -0.5, qscale=float(hd) ** -0.5,
                             prompt=prompt, tiles_per_seq=max(seq_len // tm, 1))
    return pl.pallas_call(
        kern, grid=(M // tm,), in_specs=in_specs, out_specs=out_specs, out_shape=out_shape,
        scratch_shapes=scratch, compiler_params=_cparams(("arbitrary",)),
        name="inproj_prompt" if prompt else "inproj_sample")(*args)


def _mlstm_prompt_kernel(q_ref, v_ref, kT_ref, o_ref, G_ref, Gc_ref, GT_ref, GcT_ref, gm_ref,
                         hm_ref, caug_ref, mout_ref, bprev_ref, m_ref, *, H, DK, DV, L, c_ig, c_lf):
    c = pl.program_id(1)

    @pl.when(c == 0)
    def _():
        caug_ref[...] = jnp.zeros_like(caug_ref)
        m_ref[...] = jnp.full_like(m_ref, NEG_INIT)
        bprev_ref[...] = jnp.zeros_like(bprev_ref)

    row = lax.broadcasted_iota(jnp.int32, (L, L), 0)
    colL = lax.broadcasted_iota(jnp.int32, (L, L), 1)
    causal = row >= colL
    lane = lax.broadcasted_iota(jnp.int32, (L, LANES), 1)
    ones_col = jnp.where(lane == 0, 1.0, 0.0)
    G = G_ref[...]
    Gc = Gc_ref[...]
    GT = GT_ref[...]
    GcT = GcT_ref[...]
    for h in range(H):
        bp = bprev_ref[0:1, c_lf + h:c_lf + h + 1]
        b_col = Gc[:, c_lf + h:c_lf + h + 1] - bp
        b_row = GcT[c_lf + h:c_lf + h + 1, :] - bp
        i_row = GT[c_ig + h:c_ig + h + 1, :]
        i_col = G[:, c_ig + h:c_ig + h + 1]
        m_prev = m_ref[h:h + 1, 0:1]
        dm = jnp.where(causal, b_col - b_row + i_row, -jnp.inf)
        inter = b_col + m_prev
        m_row = jnp.maximum(inter, jnp.max(dm, axis=-1, keepdims=True))
        w_intra = jnp.exp(dm - m_row)
        w_inter = jnp.exp(inter - m_row)
        qh = q_ref[:, h * DK:(h + 1) * DK]
        kTh = kT_ref[h * DK:(h + 1) * DK, :]
        vh = v_ref[:, h * DV:(h + 1) * DV].astype(F32)
        vaug = jnp.concatenate([vh, ones_col], axis=1)
        s = jnp.dot(qh, kTh, preferred_element_type=F32) * w_intra
        intra = jnp.dot(s.astype(BF16), vaug.astype(BF16), preferred_element_type=F32)
        cst = caug_ref[h]
        inter_qc = jnp.dot(qh, cst.astype(BF16), preferred_element_type=F32)
        r = w_inter * inter_qc + intra
        num = r[:, :DV]
        den = r[:, DV:DV + 1]
        hval = num / jnp.maximum(jnp.abs(den), jnp.exp(-m_row))
        m_new = m_row[L - 1:L, :]
        decay = jnp.exp(inter[L - 1:L, :] - m_new)
        w_k = jnp.exp(b_col[L - 1:L, :] - b_col + i_col - m_new)
        upd = jnp.dot(kTh, (vaug * w_k).astype(BF16), preferred_element_type=F32)
        caug_ref[h] = decay * cst + upd
        m_ref[h:h + 1, :] = jnp.broadcast_to(m_new, (1, LANES))
        gate = _sigmoid(o_ref[:, h * DV:(h + 1) * DV].astype(F32))
        hm_ref[:, h * DV:(h + 1) * DV] = (_rms(hval, gm_ref[:, h * DV:(h + 1) * DV]) * gate).astype(hm_ref.dtype)
    bprev_ref[0:1, :] = Gc[L - 1:L, :]

    @pl.when(c == pl.num_programs(1) - 1)
    def _():
        mout_ref[...] = m_ref[...]


def _mlstm_prompt(qm, vm, kT, om, G, Gc, GT, GcT, g_m, *, B, T, H, DK, DV, L, c_ig, c_lf):
    M = B * T
    NC = T // L
    assert T % L == 0
    rowb = lambda b, c: (b * NC + c, 0)
    colb = lambda b, c: (0, b * NC + c)
    kern = functools.partial(_mlstm_prompt_kernel, H=H, DK=DK, DV=DV, L=L, c_ig=c_ig, c_lf=c_lf)
    return pl.pallas_call(
        kern, grid=(B, NC),
        in_specs=[pl.BlockSpec((L, H * DK), rowb), pl.BlockSpec((L, H * DV), rowb),
                  pl.BlockSpec((H * DK, L), colb), pl.BlockSpec((L, H * DV), rowb),
                  pl.BlockSpec((L, LANES), rowb), pl.BlockSpec((L, LANES), rowb),
                  pl.BlockSpec((GATE_ROWS, L), colb), pl.BlockSpec((GATE_ROWS, L), colb),
                  pl.BlockSpec((1, H * DV), lambda b, c: (0, 0))],
        out_specs=[pl.BlockSpec((L, H * DV), rowb),
                   pl.BlockSpec((None, H, DK, DV + LANES), lambda b, c: (b, 0, 0, 0)),
                   pl.BlockSpec((None, 8, LANES), lambda b, c: (b, 0, 0))],
        out_shape=[jax.ShapeDtypeStruct((M, H * DV), BF16),
                   jax.ShapeDtypeStruct((B, H, DK, DV + LANES), F32),
                   jax.ShapeDtypeStruct((B, 8, LANES), F32)],
        scratch_shapes=[pltpu.VMEM((8, LANES), F32), pltpu.VMEM((8, LANES), F32)],
        compiler_params=_cparams(("parallel", "arbitrary")),
        name="mlstm_prompt")(qm, vm, kT, om, G, Gc, GT, GcT, g_m)


def _fox_prompt_kernel(ii_ref, jj_ref, q_ref, k_ref, v_ref, gf_ref, o_ref, m_sc, acc_sc, *, TB, HD, G, RQ):
    p = pl.program_id(2)
    i = ii_ref[p]
    j = jj_ref[p]

    @pl.when(j == 0)
    def _():
        m_sc[...] = jnp.full_like(m_sc, -jnp.inf)
        acc_sc[...] = jnp.zeros_like(acc_sc)

    def update(masked):
        items = [(g, r) for g in range(G) for r in range(TB // RQ)]

        def qk(g, r):
            q = q_ref[r * RQ:(r + 1) * RQ, g * LANES:(g + 1) * LANES]
            k = k_ref[:, g * LANES:(g + 1) * LANES]
            return lax.dot_general(q, k, (((1,), (1,)), ((), ())), preferred_element_type=F32)

        s_next = qk(*items[0])
        for n, (g, r) in enumerate(items):
            s = s_next
            if n + 1 < len(items):
                s_next = qk(*items[n + 1])
            rows = slice(r * RQ, (r + 1) * RQ)
            v = v_ref[:, g * LANES:(g + 1) * LANES]
            if masked:
                rr = lax.broadcasted_iota(jnp.int32, (RQ, TB), 0) + r * RQ
                cc = lax.broadcasted_iota(jnp.int32, (RQ, TB), 1)
                s = jnp.where(rr >= cc, s, -jnp.inf)
            m_old = m_sc[g, rows, :]
            m_new = jnp.maximum(m_old, jnp.max(s, axis=1, keepdims=True))
            alpha = jnp.exp2(m_old - m_new)
            pr = jnp.concatenate([jnp.exp2(s[:, c * LANES:(c + 1) * LANES] - m_new).astype(BF16)
                                  for c in range(TB // LANES)], axis=1)
            acc_sc[g, rows, :] = alpha * acc_sc[g, rows, :] + jnp.dot(pr, v, preferred_element_type=F32)
            m_sc[g, rows, :] = m_new

    @pl.when(j < i)
    def _():
        update(False)

    @pl.when(j == i)
    def _():
        update(True)
        lane = lax.broadcasted_iota(jnp.int32, (1, LANES), 1)
        w = jnp.where(lane < HD, 1.0 / HD, jnp.where(lane == HD, EPS, 0.0))
        for gp in range(G // 2):
            outs = []
            for g in (2 * gp, 2 * gp + 1):
                a = acc_sc[g]
                outs.append(a * lax.rsqrt(jnp.sum(a * a * w, axis=1, keepdims=True)))
            out = jnp.where(lane < HD, outs[0], pltpu.roll(outs[1], HD, 1))
            o_ref[:, gp * LANES:(gp + 1) * LANES] = (out * gf_ref[:, gp * LANES:(gp + 1) * LANES]).astype(o_ref.dtype)


def _fox_prompt(qp, kp, vp, g_f, *, B, T, HF, HD, TB, G, RQ):
    M = B * T
    TB = min(TB, T)
    RQ = min(RQ, TB)
    assert T % TB == 0 and TB % RQ == 0 and HF % G == 0 and G % 2 == 0 and 2 * HD == LANES
    nq = T // TB
    ii = np.concatenate([np.full(i + 1, i, np.int32) for i in range(nq)])
    jj = np.concatenate([np.arange(i + 1, dtype=np.int32) for i in range(nq)])
    qmap = lambda b, hg, p, ii, jj: (b * nq + ii[p], hg)
    kmap = lambda b, hg, p, ii, jj: (b * nq + jj[p], hg)
    kern = functools.partial(_fox_prompt_kernel, TB=TB, HD=HD, G=G, RQ=RQ)
    grid_spec = pltpu.PrefetchScalarGridSpec(
        num_scalar_prefetch=2, grid=(B, HF // G, len(ii)),
        in_specs=[pl.BlockSpec((TB, G * LANES), qmap), pl.BlockSpec((TB, G * LANES), kmap),
                  pl.BlockSpec((TB, G * LANES), kmap),
                  pl.BlockSpec((1, G * HD), lambda b, hg, p, ii, jj: (0, hg))],
        out_specs=pl.BlockSpec((TB, G * HD), qmap),
        scratch_shapes=[pltpu.VMEM((G, TB, LANES), F32)] * 2)
    return pl.pallas_call(
        kern, grid_spec=grid_spec, out_shape=jax.ShapeDtypeStruct((M, HF * HD), BF16),
        compiler_params=_cparams(("parallel", "parallel", "arbitrary")),
        name="fox_prompt")(jnp.asarray(ii), jnp.asarray(jj), qp, kp, vp, g_f)


def _outproj_kernel(x_ref, hm_ref, hf_ref, wm_ref, wf_ref, o_ref):
    y = jnp.dot(hm_ref[...].astype(BF16), wm_ref[...], preferred_element_type=F32)
    y = y + jnp.dot(hf_ref[...].astype(BF16), wf_ref[...], preferred_element_type=F32)
    o_ref[...] = x_ref[...] + y


def _outproj(x, hm, hf, w_m, w_f, *, tm):
    M, D = x.shape
    tm = min(tm, M)
    assert M % tm == 0
    row = lambda i: (i, 0)
    const = lambda i: (0, 0)
    return pl.pallas_call(
        _outproj_kernel, grid=(M // tm,),
        in_specs=[pl.BlockSpec((tm, D), row), pl.BlockSpec((tm, hm.shape[1]), row),
                  pl.BlockSpec((tm, hf.shape[1]), row),
                  pl.BlockSpec(w_m.shape, const), pl.BlockSpec(w_f.shape, const)],
        out_specs=pl.BlockSpec((tm, D), row), out_shape=jax.ShapeDtypeStruct((M, D), F32),
        compiler_params=_cparams(("parallel",)), name="outproj")(x, hm, hf, w_m, w_f)


def _ffn_kernel(x_ref, g_ref, wgu_ref, wd_ref, *refs, moe, n_experts, final, cap):
    refs = list(refs)
    if moe:
        wr_ref, br_ref = refs[:2]
        refs = refs[2:]
    if final:
        gfin_ref = refs[0]
        refs = refs[1:]
    if moe:
        o_ref, hn_sc, acc_sc, comb_sc, rk_sc, rkT_sc, cnt_sc = refs
    else:
        o_ref, hn_sc, acc_sc = refs
    e = pl.program_id(1)
    tm = hn_sc.shape[0]
    chunk = wd_ref.shape[0]

    def swiglu(h):
        z = jnp.dot(h, wgu_ref[...], preferred_element_type=F32)
        a = z[:, :chunk]
        act = (a * _sigmoid(a) * z[:, chunk:]).astype(BF16)
        return jnp.dot(act, wd_ref[...], preferred_element_type=F32)

    @pl.when(e == 0)
    def _():
        hn = _rms(x_ref[...], g_ref[...])
        hn_sc[...] = hn.astype(BF16)
        acc_sc[...] = jnp.zeros_like(acc_sc)
        if moe:
            lane = lax.broadcasted_iota(jnp.int32, (tm, LANES), 1)
            lanef = lane.astype(F32)
            hn_hi = hn.astype(BF16)
            hn_lo = (hn - hn_hi.astype(F32)).astype(BF16)
            wr = wr_ref[...]
            wr_hi = wr.astype(BF16)
            wr_lo = (wr - wr_hi.astype(F32)).astype(BF16)
            lg = (jnp.dot(hn_hi, wr_hi, preferred_element_type=F32) + jnp.dot(hn_hi, wr_lo, preferred_element_type=F32)
                  + jnp.dot(hn_lo, wr_hi, preferred_element_type=F32)) + br_ref[...]
            lg = jnp.where(lane < n_experts, lg, -jnp.inf)
            m1 = jnp.max(lg, axis=1, keepdims=True)
            i1 = jnp.min(jnp.where(lg == m1, lanef, float(LANES)), axis=1, keepdims=True)
            lg2 = jnp.where(lanef == i1, -jnp.inf, lg)
            m2 = jnp.max(lg2, axis=1, keepdims=True)
            i2 = jnp.min(jnp.where(lg2 == m2, lanef, float(LANES)), axis=1, keepdims=True)
            e2 = jnp.exp(m2 - m1)
            g1 = 1.0 / (1.0 + e2)
            g2 = e2 / (1.0 + e2)
            comb_sc[...] = jnp.where(lanef == i1, g1, 0.0) + jnp.where(lanef == i2, g2, 0.0)
            routed = jnp.where((lanef == i1) | (lanef == i2), 1.0, 0.0)
            incl = _cumsum_rows(routed)
            rk = jnp.where(routed > 0.0, incl - 1.0, -1.0)
            rk_sc[...] = rk
            rkT_sc[...] = jnp.transpose(rk)
            cnt_sc[...] = jnp.broadcast_to(incl[tm - 1:tm, :], cnt_sc.shape)

    if not moe:
        acc_sc[...] += swiglu(hn_sc[...])
    else:
        lane = lax.broadcasted_iota(jnp.int32, (tm, LANES), 1)
        ce = jnp.sum(jnp.where(lane == e, comb_sc[...], 0.0), axis=1, keepdims=True)
        lane_row = lax.broadcasted_iota(jnp.int32, (1, LANES), 1)
        count = jnp.sum(jnp.where(lane_row == e, cnt_sc[0:1, :], 0.0))

        @pl.when(count > cap)
        def _():
            acc_sc[...] += ce * swiglu(hn_sc[...])

        @pl.when(count <= cap)
        def _():
            rk_row = rkT_sc[pl.ds(e, 1), :]
            slot = lax.broadcasted_iota(jnp.int32, (cap, tm), 0).astype(F32)
            pick = jnp.where(rk_row == slot, 1.0, 0.0).astype(BF16)
            xs = jnp.dot(pick, hn_sc[...], preferred_element_type=F32).astype(BF16)
            y = swiglu(xs).astype(BF16)
            rk_col = jnp.sum(jnp.where(lane == e, rk_sc[...], 0.0), axis=1, keepdims=True)
            slot_l = lax.broadcasted_iota(jnp.int32, (tm, cap), 1).astype(F32)
            place = jnp.where(rk_col == slot_l, 1.0, 0.0).astype(BF16)
            acc_sc[...] += ce * jnp.dot(place, y, preferred_element_type=F32)

    @pl.when(e == pl.num_programs(1) - 1)
    def _():
        xo = x_ref[...] + acc_sc[...]
        if final:
            xo = _rms(xo, gfin_ref[...])
        o_ref[...] = xo


def _expert_capacity(tm, n_experts):
    mean = tm * TOP_K / n_experts
    return min(tm, max(64, int(-(-1.5 * mean // 64)) * 64))


def _ffn(x, g, wgu, wd, router, g_final, *, tm):
    M, D = x.shape
    tm = min(tm, M)
    assert M % tm == 0
    moe = router is not None
    final = g_final is not None
    E, chunk = wd.shape[0], wd.shape[1]
    row = lambda i, e: (i, 0)
    const = lambda i, e: (0, 0)
    args = [x, g, wgu, wd]
    in_specs = [pl.BlockSpec((tm, D), row), pl.BlockSpec((1, D), const),
                pl.BlockSpec((None, D, 2 * chunk), lambda i, e: (e, 0, 0)),
                pl.BlockSpec((None, chunk, D), lambda i, e: (e, 0, 0))]
    scratch = [pltpu.VMEM((tm, D), BF16), pltpu.VMEM((tm, D), F32)]
    if moe:
        args += list(router)
        in_specs += [pl.BlockSpec((D, LANES), const), pl.BlockSpec((1, LANES), const)]
        scratch += [pltpu.VMEM((tm, LANES), F32), pltpu.VMEM((tm, LANES), F32), pltpu.VMEM((LANES, tm), F32),
                    pltpu.VMEM((8, LANES), F32)]
    if final:
        args.append(g_final)
        in_specs.append(pl.BlockSpec((1, D), const))
    kern = functools.partial(_ffn_kernel, moe=moe, n_experts=E, final=final, cap=_expert_capacity(tm, E))
    return pl.pallas_call(
        kern, grid=(M // tm, E), in_specs=in_specs,
        out_specs=pl.BlockSpec((tm, D), row), out_shape=jax.ShapeDtypeStruct((M, D), F32),
        scratch_shapes=scratch, compiler_params=_cparams(("parallel", "arbitrary")),
        name="ffn_moe" if moe else "ffn_dense")(*args)


def _mlstm_step_kernel(q_ref, k_ref, v_ref, o_ref, kc_ref, G_ref, C_ref, n_ref, m_ref, gm_ref,
                       hm_ref, Cn_ref, nn_ref, mn_ref, *, NB, H, DK, DV, c_ig, c_lf):
    sub = lax.broadcasted_iota(jnp.int32, (NB, DV), 0)
    lane = lax.broadcasted_iota(jnp.int32, (NB, LANES), 1)
    m_out = jnp.zeros((NB, LANES), F32)
    for h in range(H):
        q = q_ref[:, h * DK:(h + 1) * DK]
        k = k_ref[:, h * DK:(h + 1) * DK]
        v = v_ref[:, h * DV:(h + 1) * DV]
        n = n_ref[:, h * DK:(h + 1) * DK]
        ig = G_ref[:, c_ig + h:c_ig + h + 1]
        lf = G_ref[:, c_lf + h:c_lf + h + 1]
        m0 = m_ref[:, h:h + 1]
        inter = lf + m0
        m_new = jnp.maximum(inter, ig)
        w_inter = jnp.exp(inter - m_new)
        w_intra = jnp.exp(ig - m_new)
        s = jnp.sum(q * k, axis=1, keepdims=True) * w_intra
        qn = jnp.sum(q * n, axis=1, keepdims=True)
        vw = v * w_intra
        kc = kc_ref[h * DK:(h + 1) * DK, :]
        qC = jnp.zeros((NB, DV), F32)
        for bb in range(NB):
            C = C_ref[bb, h]
            qC = jnp.where(sub == bb, jnp.dot(q, C, preferred_element_type=F32), qC)
            vsel = jnp.concatenate([jnp.where(sub == bb, vw, 0.0), jnp.zeros((LANES - NB, DV), F32)], axis=0)
            Cn_ref[bb, h] = w_inter[bb:bb + 1, :] * C + jnp.dot(kc, vsel, preferred_element_type=F32)
        num = w_inter * qC + s * v
        den = w_inter * qn + s
        hval = num / jnp.maximum(jnp.abs(den), jnp.exp(-m_new))
        nn_ref[:, h * DK:(h + 1) * DK] = w_inter * n + w_intra * k
        m_out = jnp.where(lane == h, m_new, m_out)
        gate = _sigmoid(o_ref[:, h * DV:(h + 1) * DV])
        hm_ref[:, h * DV:(h + 1) * DV] = _rms(hval, gm_ref[:, h * DV:(h + 1) * DV]) * gate
    mn_ref[...] = m_out


def _mlstm_step(q, k, v, o, G, state_C, state_n, state_m, layer, g_m, *, NB, H, DK, DV, c_ig, c_lf):
    DB = q.shape[0]
    NB = min(NB, DB)
    assert DB % NB == 0 and NB <= LANES
    S = DB // NB
    kc = jnp.pad(k.reshape(S, NB, H * DK).transpose(0, 2, 1), ((0, 0), (0, 0), (0, LANES - NB)))
    row = lambda s: (s, 0)
    kern = functools.partial(_mlstm_step_kernel, NB=NB, H=H, DK=DK, DV=DV, c_ig=c_ig, c_lf=c_lf)
    return pl.pallas_call(
        kern, grid=(S,),
        in_specs=[pl.BlockSpec((NB, H * DK), row), pl.BlockSpec((NB, H * DK), row),
                  pl.BlockSpec((NB, H * DV), row), pl.BlockSpec((NB, H * DV), row),
                  pl.BlockSpec((None, H * DK, LANES), lambda s: (s, 0, 0)),
                  pl.BlockSpec((NB, LANES), row),
                  pl.BlockSpec((None, NB, H, DK, DV), lambda s: (layer, s, 0, 0, 0)),
                  pl.BlockSpec((None, NB, H * DK), lambda s: (layer, s, 0)),
                  pl.BlockSpec((None, NB, H), lambda s: (layer, s, 0)),
                  pl.BlockSpec((1, H * DV), lambda s: (0, 0))],
        out_specs=[pl.BlockSpec((NB, H * DV), row),
                   pl.BlockSpec((NB, H, DK, DV), lambda s: (s, 0, 0, 0)),
                   pl.BlockSpec((NB, H * DK), row), pl.BlockSpec((NB, LANES), row)],
        out_shape=[jax.ShapeDtypeStruct((DB, H * DV), F32), jax.ShapeDtypeStruct((DB, H, DK, DV), F32),
                   jax.ShapeDtypeStruct((DB, H * DK), F32), jax.ShapeDtypeStruct((DB, LANES), F32)],
        compiler_params=_cparams(("parallel",)),
        name="mlstm_step")(q, k, v, o, kc, G, state_C, state_n, state_m, g_m)


def _fox_step_kernel(pt_ref, q_ref, kc_ref, vc_ref, G_ref, gf_ref, *refs, NP, HF, HD, PAGE):
    k_refs = refs[:NP]
    v_refs = refs[NP:2 * NP]
    lf_refs = refs[2 * NP:3 * NP]
    o_ref = refs[3 * NP]
    DF = HF * HD
    sub = lax.broadcasted_iota(jnp.int32, (HF, DF), 0)
    lane = lax.broadcasted_iota(jnp.int32, (HF, DF), 1)
    diag = lane // HD == sub
    q = q_ref[...]
    qbd = jnp.where(diag, jnp.broadcast_to(q, (HF, DF)), 0.0)
    qbd16 = qbd.astype(BF16)
    sub_g = lax.broadcasted_iota(jnp.int32, (HF, LANES), 0)
    lane_g = lax.broadcasted_iota(jnp.int32, (HF, LANES), 1)
    carry = jnp.sum(jnp.where(lane_g == sub_g, jnp.broadcast_to(G_ref[...], (HF, LANES)), 0.0),
                    axis=1, keepdims=True)
    ti = lax.broadcasted_iota(jnp.int32, (PAGE, PAGE), 0)
    tj = lax.broadcasted_iota(jnp.int32, (PAGE, PAGE), 1)
    later = jnp.where(ti > tj, 1.0, 0.0)
    s_cur = jnp.sum(qbd * kc_ref[...], axis=1, keepdims=True)
    scores = [None] * NP
    mx = s_cur
    for p in range(NP - 1, -1, -1):
        lfT = lf_refs[p][...]
        suf = jnp.dot(lfT, later, preferred_element_type=F32, precision=lax.Precision.HIGHEST) + carry
        carry = carry + jnp.sum(lfT, axis=1, keepdims=True)
        kT = k_refs[p][...].astype(BF16)
        s = jnp.dot(qbd16, kT, preferred_element_type=F32) + suf
        scores[p] = s
        mx = jnp.maximum(mx, jnp.max(s, axis=1, keepdims=True))
    p_cur = jnp.exp(s_cur - mx)
    l = p_cur
    acc = p_cur * vc_ref[...]
    for p in range(NP):
        pr = jnp.exp(scores[p] - mx)
        l = l + jnp.sum(pr, axis=1, keepdims=True)
        vT = v_refs[p][...].astype(BF16)
        acc = acc + lax.dot_general(pr.astype(BF16), vT, (((1,), (1,)), ((), ())), preferred_element_type=F32)
    o = acc / l
    ms = jnp.sum(jnp.where(diag, o * o, 0.0), axis=1, keepdims=True) * (1.0 / HD)
    on = jnp.where(diag, o * lax.rsqrt(ms + EPS), 0.0)
    o_ref[...] = jnp.sum(on, axis=0, keepdims=True) * gf_ref[...]


def _fox_step(q, k_cur, v_cur, lf_cur, cache_kT, cache_vT, cache_lfT, page_table, layer, g_f, *, HF, HD):
    DB = q.shape[0]
    NP = page_table.shape[1]
    PAGE = cache_kT.shape[3]
    DF = HF * HD
    pt = page_table.reshape(-1)
    r3 = lambda a: a.reshape(DB, 1, a.shape[-1])
    cur = lambda b, pt: (b, 0, 0)

    def page_map(p):
        return lambda b, pt: (layer, pt[b * NP + p], 0, 0)

    kv_specs = [pl.BlockSpec((None, None, DF, PAGE), page_map(p)) for p in range(NP)]
    lf_specs = [pl.BlockSpec((None, None, HF, PAGE), page_map(p)) for p in range(NP)]
    grid_spec = pltpu.PrefetchScalarGridSpec(
        num_scalar_prefetch=1, grid=(DB,),
        in_specs=[pl.BlockSpec((None, 1, DF), cur)] * 3 + [pl.BlockSpec((None, 1, LANES), cur),
                                                           pl.BlockSpec((1, DF), lambda b, pt: (0, 0))]
                 + kv_specs + kv_specs + lf_specs,
        out_specs=pl.BlockSpec((None, 1, DF), cur))
    kern = functools.partial(_fox_step_kernel, NP=NP, HF=HF, HD=HD, PAGE=PAGE)
    out = pl.pallas_call(
        kern, grid_spec=grid_spec, out_shape=jax.ShapeDtypeStruct((DB, 1, DF), F32),
        compiler_params=_cparams(("arbitrary",)),
        name="fox_step")(pt, r3(q), r3(k_cur), r3(v_cur), r3(lf_cur), g_f,
                         *([cache_kT] * NP), *([cache_vT] * NP), *([cache_lfT] * NP))
    return out.reshape(DB, DF)


def kernel(x_prompt, x_sample, state_C, state_n, state_m, cache_k, cache_v, cache_logf, page_table, g_mix_norm, w_in, b_ig, b_fg, b_ff, g_out_m, g_out_f, w_out, g_ffn_norm, w_ffn_gate, w_ffn_up, w_ffn_down, w_router, b_router, w_moe_gate, w_moe_up, w_moe_down, g_final):
    B, T, D = x_prompt.shape
    DB = x_sample.shape[0]
    assert x_sample.shape[1] == 1
    depth = w_in.shape[0]
    HM, DK, DV = state_C.shape[2:]
    PAGE, HF, HD = cache_k.shape[2:]
    n_pool = cache_k.shape[1]
    dq, dv, df = HM * DK, HM * DV, HF * HD
    assert dq % LANES == 0 and dv % LANES == 0 and df % LANES == 0 and LANES % HD == 0
    assert HF + 2 * HM <= GATE_ROWS
    dims = (dq, dv, df, HF, HM, DK, HD)
    c_ig, c_lf = HF, HF + HM
    E = w_router.shape[-1]
    chunk = w_moe_gate.shape[-1]
    M = B * T

    o = np.cumsum([0, dq, dq, dv, dv, HM, HM, df, df, df, HF]).tolist()
    w_m = w_in[:, :, o[0]:o[4]].astype(BF16)
    w_f = w_in[:, :, o[6]:o[9]].astype(BF16)
    scat = np.zeros((N_SPLIT, LANES, HF * LANES), np.float32)
    for t in range(N_SPLIT):
        scat[t, np.arange(HF), np.arange(HF) * LANES + HD + t] = 1.0
    scat = jnp.asarray(scat, BF16)
    n_gate = HF + 2 * HM
    w_gate = jnp.concatenate([w_in[:, :, o[9]:o[10]], w_in[:, :, o[4]:o[6]],
                              jnp.zeros((depth, D, LANES - n_gate), w_in.dtype)], axis=-1).astype(BF16)
    b_gate = jnp.concatenate([b_ff, b_ig, b_fg, jnp.zeros((depth, LANES - n_gate), F32)],
                             axis=-1).astype(F32).reshape(depth, 1, LANES)
    w_out16 = w_out.astype(BF16)
    n_dense, _, d_ff = w_ffn_gate.shape
    assert d_ff % chunk == 0
    nc = d_ff // chunk
    split = lambda w: w.astype(BF16).reshape(n_dense, D, nc, chunk)
    wfgu = jnp.concatenate([split(w_ffn_gate), split(w_ffn_up)], axis=-1).transpose(0, 2, 1, 3)
    wfd = w_ffn_down.astype(BF16).reshape(n_dense, nc, chunk, D)
    wmgu = jnp.concatenate([w_moe_gate.astype(BF16), w_moe_up.astype(BF16)], axis=-1)
    wmd = w_moe_down.astype(BF16)
    w_r = jnp.concatenate([w_router, jnp.zeros(w_router.shape[:2] + (LANES - E,), F32)], axis=-1)
    b_r = jnp.concatenate([b_router, jnp.zeros((b_router.shape[0], LANES - E), F32)],
                          axis=-1).reshape(-1, 1, LANES)
    ckT = jnp.transpose(cache_k, (0, 1, 3, 4, 2)).reshape(depth, n_pool, df, PAGE)
    cvT = jnp.transpose(cache_v, (0, 1, 3, 4, 2)).reshape(depth, n_pool, df, PAGE)
    clfT = jnp.swapaxes(cache_logf, 2, 3)
    sn = state_n.reshape(depth, DB, dq)

    xp = x_prompt.reshape(M, D)
    xs = x_sample.reshape(DB, D)
    pC, pn, pm, pk, pv, plf = [], [], [], [], [], []
    sC, sn_o, sm, sk, sv, slf = [], [], [], [], [], []
    for l in range(depth):
        gmix = g_mix_norm[l].reshape(1, D)
        gm = g_out_m[l].reshape(1, dv)
        gf = g_out_f[l].reshape(1, df)
        w_om, w_of = w_out16[l, :dv], w_out16[l, dv:]
        (qm, vm, kT, om, qp, kp, vp, kf, vf, G, Gc, GT, GcT) = _inproj(
            xp, gmix, w_m[l], w_f[l], w_gate[l], b_gate[l], scat, dims, prompt=True, tm=512, seq_len=T)
        hm, caug, mout = _mlstm_prompt(qm, vm, kT, om, G, Gc, GT, GcT, gm, B=B, T=T, H=HM, DK=DK, DV=DV,
                                       L=min(512, T), c_ig=c_ig, c_lf=c_lf)
        hf = _fox_prompt(qp, kp, vp, gf, B=B, T=T, HF=HF, HD=HD, TB=512, G=HF, RQ=256)
        xp = _outproj(xp, hm, hf, w_om, w_of, tm=512)
        pC.append(caug[..., :DV]); pn.append(caug[..., DV]); pm.append(mout[:, :HM, 0])
        tok_major = lambda a: a.reshape(B, HF, HD, T).transpose(0, 3, 1, 2)
        pk.append(tok_major(kf)); pv.append(tok_major(vf))
        plf.append(G[:, :HF].reshape(B, T, HF))
        (qm, km, vm, om, qf, kf, vf, G) = _inproj(
            xs, gmix, w_m[l], w_f[l], w_gate[l], b_gate[l], None, dims, prompt=False, tm=DB, seq_len=1)
        hm, Cn, nn, mn = _mlstm_step(qm, km, vm, om, G, state_C, sn, state_m, l, gm,
                                     NB=8, H=HM, DK=DK, DV=DV, c_ig=c_ig, c_lf=c_lf)
        hf = _fox_step(qf, kf, vf, G, ckT, cvT, clfT, page_table, l, gf, HF=HF, HD=HD)
        xs = _outproj(xs, hm, hf, w_om, w_of, tm=DB)
        sC.append(Cn); sn_o.append(nn.reshape(DB, HM, DK)); sm.append(mn[:, :HM])
        sk.append(kf.reshape(DB, 1, HF, HD)); sv.append(vf.reshape(DB, 1, HF, HD))
        slf.append(G[:, :HF].reshape(DB, 1, HF))
        gffn = g_ffn_norm[l].reshape(1, D)
        gfin = g_final.reshape(1, D) if l == depth - 1 else None
        j = l // 2
        if l % 2 == 0:
            args = (wfgu[j], wfd[j], None)
        else:
            args = (wmgu[j], wmd[j], (w_r[j], b_r[j]))
        xp = _ffn(xp, gffn, *args, gfin, tm=512)
        xs = _ffn(xs, gffn, *args, gfin, tm=DB)

    return (xp.reshape(B, T, D), xs.reshape(DB, 1, D),
            jnp.stack(pC), jnp.stack(pn), jnp.stack(pm),
            jnp.stack(pk), jnp.stack(pv), jnp.stack(plf),
            jnp.stack(sC), jnp.stack(sn_o), jnp.stack(sm),
            jnp.stack(sk), jnp.stack(sv), jnp.stack(slf))
```

```python
import functools

import numpy as np
import jax
import jax.numpy as jnp
from jax import lax
from jax.experimental import pallas as pl
from jax.experimental.pallas import tpu as pltpu

F32 = jnp.float32
BF16 = jnp.bfloat16
EPS = 1e-6
NEG_INIT = -1e30
TOP_K = 2
LANES = 128
GATE_ROWS = 16
VMEM_LIMIT = 56 * 1024 * 1024
LOG2E = 1.4426950408889634
N_SPLIT = 3
FULL_TILE_ROWS = 512


def _cparams(sem):
    return pltpu.CompilerParams(dimension_semantics=sem, vmem_limit_bytes=VMEM_LIMIT)


def _rms(x, g):
    var = jnp.mean(x * x, axis=-1, keepdims=True)
    return x * lax.rsqrt(var + EPS) * g


def _log_sigmoid(x):
    return jnp.minimum(x, 0.0) - jnp.log(1.0 + jnp.exp(-jnp.abs(x)))


def _sigmoid(x):
    return 1.0 / (1.0 + jnp.exp(-x))


def _cumsum_rows(a):
    n = a.shape[0]
    row = lax.broadcasted_iota(jnp.int32, a.shape, 0)
    k = 1
    while k < n:
        a = a + jnp.where(row >= k, pltpu.roll(a, k, 0), 0.0)
        k *= 2
    return a


def _spread_heads(z, hd, tail):
    lane = lax.broadcasted_iota(jnp.int32, (1, LANES), 1)
    blocks = []
    for p in range(z.shape[1] // LANES):
        a = z[:, p * LANES:(p + 1) * LANES]
        blocks.append(jnp.where(lane < hd, a, tail(2 * p)))
        blocks.append(jnp.where(lane < hd, pltpu.roll(a, hd, 1), tail(2 * p + 1)))
    return jnp.concatenate(blocks, axis=1)


def _inproj_kernel(x_ref, g_ref, wm_ref, wf_ref, wg_ref, bg_ref, *refs, dq, dv, df, hf, hm, hd, kscale, qscale,
                   prompt, tiles_per_seq, n_carried):
    if prompt:
        (scat_ref, qm_ref, vm_ref, kT_ref, om_ref, qp_ref, kp_ref, vp_ref, kf_ref, vf_ref,
         G_ref, Gc_ref, GT_ref, GcT_ref, carry_ref) = refs[:1] + refs[1 + n_carried:]
    else:
        (qm_ref, km_ref, vm_ref, om_ref, qf_ref, kf_ref, vf_ref, G_ref) = refs
    hn = _rms(x_ref[...], g_ref[...]).astype(BF16)

    def proj(w_ref, off, width):
        return jnp.dot(hn, w_ref[:, off:off + width], preferred_element_type=F32)

    zg = jnp.dot(hn, wg_ref[...], preferred_element_type=F32) + bg_ref[...]
    col = lax.broadcasted_iota(jnp.int32, zg.shape, 1)
    is_log = (col < hf) | ((col >= hf + hm) & (col < hf + 2 * hm))
    is_lin = (col >= hf) & (col < hf + hm)
    G = jnp.where(is_log, _log_sigmoid(zg), jnp.where(is_lin, zg, 0.0))
    G_ref[...] = G

    qm_ref[...] = proj(wm_ref, 0, dq).astype(qm_ref.dtype)
    km = proj(wm_ref, dq, dq) * kscale
    if prompt:
        kT_ref[...] = jnp.transpose(km).astype(kT_ref.dtype)
    else:
        km_ref[...] = km
    vm_ref[...] = proj(wm_ref, 2 * dq, dv).astype(vm_ref.dtype)
    om_ref[...] = proj(wm_ref, 2 * dq + dv, dv).astype(om_ref.dtype)
    if not prompt:
        qf_ref[...] = proj(wf_ref, 0, df) * qscale
        kf_ref[...] = proj(wf_ref, df, df)
        vf_ref[...] = proj(wf_ref, 2 * df, df)
        return

    i = pl.program_id(0)

    @pl.when(i % tiles_per_seq == 0)
    def _():
        carry_ref[...] = jnp.zeros_like(carry_ref)

    Gc = _cumsum_rows(G) + carry_ref[0:1, :]
    carry_ref[0:1, :] = Gc[-1:, :]
    Gc_ref[...] = Gc
    GT_ref[...] = jnp.transpose(G)[:GATE_ROWS, :]
    GcT_ref[...] = jnp.transpose(Gc)[:GATE_ROWS, :]

    lane = lax.broadcasted_iota(jnp.int32, (1, LANES), 1)
    zq = proj(wf_ref, 0, df) * (qscale * LOG2E)
    zk = proj(wf_ref, df, df)
    zv = proj(wf_ref, 2 * df, df)
    kf_ref[...] = jnp.transpose(zk)
    vf_ref[...] = jnp.transpose(zv)
    rem = Gc * (-LOG2E)
    bias = jnp.zeros((zk.shape[0], hf * LANES), F32)
    for t in range(N_SPLIT):
        piece = rem.astype(BF16)
        rem = rem - piece.astype(F32)
        bias = bias + jnp.dot(piece, scat_ref[t], preferred_element_type=F32)
    q_tail = jnp.where((lane >= hd) & (lane < hd + N_SPLIT), 1.0, 0.0)
    v_tail = jnp.where(lane == hd, 1.0, 0.0)
    qp_ref[...] = _spread_heads(zq, hd, lambda h: q_tail).astype(BF16)
    kp_ref[...] = _spread_heads(zk, hd, lambda h: bias[:, h * LANES:(h + 1) * LANES]).astype(BF16)
    vp_ref[...] = _spread_heads(zv, hd, lambda h: v_tail).astype(BF16)


def _inproj(x, g, w_m, w_f, w_gate, b_gate, scat, dims, *, prompt, tm, seq_len, layer=0, depth=1, carried=()):
    M, D = x.shape
    dq, dv, df, hf, hm, dk, hd = dims
    dfp = hf * LANES
    tm = min(tm, M)
    assert M % tm == 0
    if prompt:
        assert seq_len % tm == 0
    row = lambda i: (i, 0)
    colb = lambda i: (0, i)
    const = lambda i: (0, 0)
    tps = max(seq_len // tm, 1)
    seqb = lambda i: (layer, i // tps, 0, i % tps)
    aliases = {}
    args = [x, g, w_m, w_f, w_gate, b_gate]
    in_specs = [pl.BlockSpec((tm, D), row), pl.BlockSpec((1, D), const),
                pl.BlockSpec(w_m.shape, const), pl.BlockSpec(w_f.shape, const),
                pl.BlockSpec((D, LANES), const), pl.BlockSpec((1, LANES), const)]
    if prompt:
        args.append(scat)
        in_specs.append(pl.BlockSpec(scat.shape, lambda i: (0, 0, 0)))
        kv_out = 7
        for n, buf in enumerate(carried):
            aliases[len(args)] = kv_out + n
            args.append(buf)
            in_specs.append(pl.BlockSpec(memory_space=pl.ANY))
        out_shape = [jax.ShapeDtypeStruct((M, dq), BF16), jax.ShapeDtypeStruct((M, dv), BF16),
                     jax.ShapeDtypeStruct((dq, M), BF16), jax.ShapeDtypeStruct((M, dv), BF16),
                     jax.ShapeDtypeStruct((M, dfp), BF16), jax.ShapeDtypeStruct((M, dfp), BF16),
                     jax.ShapeDtypeStruct((M, dfp), BF16),
                     jax.ShapeDtypeStruct((depth, M // seq_len, df, seq_len), F32),
                     jax.ShapeDtypeStruct((depth, M // seq_len, df, seq_len), F32),
                     jax.ShapeDtypeStruct((M, LANES), F32), jax.ShapeDtypeStruct((M, LANES), F32),
                     jax.ShapeDtypeStruct((GATE_ROWS, M), F32), jax.ShapeDtypeStruct((GATE_ROWS, M), F32)]
        out_specs = [pl.BlockSpec((tm, dq), row), pl.BlockSpec((tm, dv), row),
                     pl.BlockSpec((dq, tm), colb), pl.BlockSpec((tm, dv), row),
                     pl.BlockSpec((tm, dfp), row), pl.BlockSpec((tm, dfp), row), pl.BlockSpec((tm, dfp), row),
                     pl.BlockSpec((None, None, df, tm), seqb), pl.BlockSpec((None, None, df, tm), seqb),
                     pl.BlockSpec((tm, LANES), row), pl.BlockSpec((tm, LANES), row),
                     pl.BlockSpec((GATE_ROWS, tm), colb), pl.BlockSpec((GATE_ROWS, tm), colb)]
        scratch = [pltpu.VMEM((8, LANES), F32)]
    else:
        out_shape = ([jax.ShapeDtypeStruct((M, dq), F32)] * 2 + [jax.ShapeDtypeStruct((M, dv), F32)] * 2
                     + [jax.ShapeDtypeStruct((M, df), F32)] * 3 + [jax.ShapeDtypeStruct((M, LANES), F32)])
        out_specs = ([pl.BlockSpec((tm, dq), row)] * 2 + [pl.BlockSpec((tm, dv), row)] * 2
                     + [pl.BlockSpec((tm, df), row)] * 3 + [pl.BlockSpec((tm, LANES), row)])
        scratch = []
    kern = functools.partial(_inproj_kernel, dq=dq, dv=dv, df=df, hf=hf, hm=hm, hd=hd,
                             kscale=float(dk) ** -0.5, qscale=float(hd) ** -0.5,
                             prompt=prompt, tiles_per_seq=max(seq_len // tm, 1), n_carried=len(carried))
    return pl.pallas_call(
        kern, grid=(M // tm,), in_specs=in_specs, out_specs=out_specs, out_shape=out_shape,
        input_output_aliases=aliases, scratch_shapes=scratch, compiler_params=_cparams(("arbitrary",)),
        name="inproj_prompt" if prompt else "inproj_sample")(*args)


def _mlstm_prompt_kernel(q_ref, v_ref, kT_ref, o_ref, G_ref, Gc_ref, GT_ref, GcT_ref, gm_ref,
                         hm_ref, caug_ref, mout_ref, bprev_ref, m_ref, *, H, DK, DV, L, c_ig, c_lf):
    c = pl.program_id(1)

    @pl.when(c == 0)
    def _():
        caug_ref[...] = jnp.zeros_like(caug_ref)
        m_ref[...] = jnp.full_like(m_ref, NEG_INIT)
        bprev_ref[...] = jnp.zeros_like(bprev_ref)

    row = lax.broadcasted_iota(jnp.int32, (L, L), 0)
    colL = lax.broadcasted_iota(jnp.int32, (L, L), 1)
    causal = row >= colL
    lane = lax.broadcasted_iota(jnp.int32, (L, LANES), 1)
    ones_col = jnp.where(lane == 0, 1.0, 0.0)
    G = G_ref[...]
    Gc = Gc_ref[...]
    GT = GT_ref[...]
    GcT = GcT_ref[...]
    for h in range(H):
        bp = bprev_ref[0:1, c_lf + h:c_lf + h + 1]
        b_col = Gc[:, c_lf + h:c_lf + h + 1] - bp
        b_row = GcT[c_lf + h:c_lf + h + 1, :] - bp
        i_row = GT[c_ig + h:c_ig + h + 1, :]
        i_col = G[:, c_ig + h:c_ig + h + 1]
        m_prev = m_ref[h:h + 1, 0:1]
        dm = jnp.where(causal, b_col - b_row + i_row, -jnp.inf)
        inter = b_col + m_prev
        m_row = jnp.maximum(inter, jnp.max(dm, axis=-1, keepdims=True))
        w_intra = jnp.exp(dm - m_row)
        w_inter = jnp.exp(inter - m_row)
        qh = q_ref[:, h * DK:(h + 1) * DK]
        kTh = kT_ref[h * DK:(h + 1) * DK, :]
        vh = v_ref[:, h * DV:(h + 1) * DV].astype(F32)
        vaug = jnp.concatenate([vh, ones_col], axis=1)
        s = jnp.dot(qh, kTh, preferred_element_type=F32) * w_intra
        intra = jnp.dot(s.astype(BF16), vaug.astype(BF16), preferred_element_type=F32)
        cst = caug_ref[h]
        inter_qc = jnp.dot(qh, cst.astype(BF16), preferred_element_type=F32)
        r = w_inter * inter_qc + intra
        num = r[:, :DV]
        den = r[:, DV:DV + 1]
        hval = num / jnp.maximum(jnp.abs(den), jnp.exp(-m_row))
        m_new = m_row[L - 1:L, :]
        decay = jnp.exp(inter[L - 1:L, :] - m_new)
        w_k = jnp.exp(b_col[L - 1:L, :] - b_col + i_col - m_new)
        upd = jnp.dot(kTh, (vaug * w_k).astype(BF16), preferred_element_type=F32)
        caug_ref[h] = decay * cst + upd
        m_ref[h:h + 1, :] = jnp.broadcast_to(m_new, (1, LANES))
        gate = _sigmoid(o_ref[:, h * DV:(h + 1) * DV].astype(F32))
        hm_ref[:, h * DV:(h + 1) * DV] = (_rms(hval, gm_ref[:, h * DV:(h + 1) * DV]) * gate).astype(hm_ref.dtype)
    bprev_ref[0:1, :] = Gc[L - 1:L, :]

    @pl.when(c == pl.num_programs(1) - 1)
    def _():
        mout_ref[...] = m_ref[...]


def _mlstm_prompt(qm, vm, kT, om, G, Gc, GT, GcT, g_m, *, B, T, H, DK, DV, L, c_ig, c_lf):
    M = B * T
    NC = T // L
    assert T % L == 0
    rowb = lambda b, c: (b * NC + c, 0)
    colb = lambda b, c: (0, b * NC + c)
    kern = functools.partial(_mlstm_prompt_kernel, H=H, DK=DK, DV=DV, L=L, c_ig=c_ig, c_lf=c_lf)
    return pl.pallas_call(
        kern, grid=(B, NC),
        in_specs=[pl.BlockSpec((L, H * DK), rowb), pl.BlockSpec((L, H * DV), rowb),
                  pl.BlockSpec((H * DK, L), colb), pl.BlockSpec((L, H * DV), rowb),
                  pl.BlockSpec((L, LANES), rowb), pl.BlockSpec((L, LANES), rowb),
                  pl.BlockSpec((GATE_ROWS, L), colb), pl.BlockSpec((GATE_ROWS, L), colb),
                  pl.BlockSpec((1, H * DV), lambda b, c: (0, 0))],
        out_specs=[pl.BlockSpec((L, H * DV), rowb),
                   pl.BlockSpec((None, H, DK, DV + LANES), lambda b, c: (b, 0, 0, 0)),
                   pl.BlockSpec((None, 8, LANES), lambda b, c: (b, 0, 0))],
        out_shape=[jax.ShapeDtypeStruct((M, H * DV), BF16),
                   jax.ShapeDtypeStruct((B, H, DK, DV + LANES), F32),
                   jax.ShapeDtypeStruct((B, 8, LANES), F32)],
        scratch_shapes=[pltpu.VMEM((8, LANES), F32), pltpu.VMEM((8, LANES), F32)],
        compiler_params=_cparams(("parallel", "arbitrary")),
        name="mlstm_prompt")(qm, vm, kT, om, G, Gc, GT, GcT, g_m)


def _fox_prompt_kernel(ii_ref, jj_ref, q_ref, k_ref, v_ref, gf_ref, o_ref, m_sc, acc_sc, *, TB, HD, G, RQ):
    p = pl.program_id(2)
    i = ii_ref[p]
    j = jj_ref[p]

    @pl.when(j == 0)
    def _():
        m_sc[...] = jnp.full_like(m_sc, -jnp.inf)
        acc_sc[...] = jnp.zeros_like(acc_sc)

    def update(masked):
        items = [(g, r) for g in range(G) for r in range(TB // RQ)]

        def qk(g, r):
            q = q_ref[r * RQ:(r + 1) * RQ, g * LANES:(g + 1) * LANES]
            k = k_ref[:, g * LANES:(g + 1) * LANES]
            return lax.dot_general(q, k, (((1,), (1,)), ((), ())), preferred_element_type=F32)

        s_next = qk(*items[0])
        for n, (g, r) in enumerate(items):
            s = s_next
            if n + 1 < len(items):
                s_next = qk(*items[n + 1])
            rows = slice(r * RQ, (r + 1) * RQ)
            v = v_ref[:, g * LANES:(g + 1) * LANES]
            if masked:
                rr = lax.broadcasted_iota(jnp.int32, (RQ, TB), 0) + r * RQ
                cc = lax.broadcasted_iota(jnp.int32, (RQ, TB), 1)
                s = jnp.where(rr >= cc, s, -jnp.inf)
            m_old = m_sc[g, rows, :]
            m_new = jnp.maximum(m_old, jnp.max(s, axis=1, keepdims=True))
            alpha = jnp.exp2(m_old - m_new)
            pr = jnp.concatenate([jnp.exp2(s[:, c * LANES:(c + 1) * LANES] - m_new).astype(BF16)
                                  for c in range(TB // LANES)], axis=1)
            acc_sc[g, rows, :] = alpha * acc_sc[g, rows, :] + jnp.dot(pr, v, preferred_element_type=F32)
            m_sc[g, rows, :] = m_new

    @pl.when(j < i)
    def _():
        update(False)

    @pl.when(j == i)
    def _():
        update(True)
        lane = lax.broadcasted_iota(jnp.int32, (1, LANES), 1)
        lrow = lax.broadcasted_iota(jnp.int32, (LANES, LANES), 0)
        wmat = jnp.where(lrow < HD, 1.0 / HD, jnp.where(lrow == HD, EPS, 0.0))
        for gp in range(G // 2):
            outs = []
            for g in (2 * gp, 2 * gp + 1):
                a = acc_sc[g]
                outs.append(a * lax.rsqrt(jnp.dot(a * a, wmat, preferred_element_type=F32)))
            out = jnp.where(lane < HD, outs[0], pltpu.roll(outs[1], HD, 1))
            o_ref[:, gp * LANES:(gp + 1) * LANES] = (out * gf_ref[:, gp * LANES:(gp + 1) * LANES]).astype(o_ref.dtype)


def _fox_prompt(qp, kp, vp, g_f, *, B, T, HF, HD, TB, G, RQ):
    M = B * T
    TB = min(TB, T)
    RQ = min(RQ, TB)
    assert T % TB == 0 and TB % RQ == 0 and HF % G == 0 and G % 2 == 0 and 2 * HD == LANES
    nq = T // TB
    ii = np.concatenate([np.full(i + 1, i, np.int32) for i in range(nq)])
    jj = np.concatenate([np.arange(i + 1, dtype=np.int32) for i in range(nq)])
    qmap = lambda b, hg, p, ii, jj: (b * nq + ii[p], hg)
    kmap = lambda b, hg, p, ii, jj: (b * nq + jj[p], hg)
    kern = functools.partial(_fox_prompt_kernel, TB=TB, HD=HD, G=G, RQ=RQ)
    grid_spec = pltpu.PrefetchScalarGridSpec(
        num_scalar_prefetch=2, grid=(B, HF // G, len(ii)),
        in_specs=[pl.BlockSpec((TB, G * LANES), qmap), pl.BlockSpec((TB, G * LANES), kmap),
                  pl.BlockSpec((TB, G * LANES), kmap),
                  pl.BlockSpec((1, G * HD), lambda b, hg, p, ii, jj: (0, hg))],
        out_specs=pl.BlockSpec((TB, G * HD), qmap),
        scratch_shapes=[pltpu.VMEM((G, TB, LANES), F32)] * 2)
    return pl.pallas_call(
        kern, grid_spec=grid_spec, out_shape=jax.ShapeDtypeStruct((M, HF * HD), BF16),
        compiler_params=_cparams(("parallel", "parallel", "arbitrary")),
        name="fox_prompt")(jnp.asarray(ii), jnp.asarray(jj), qp, kp, vp, g_f)


def _outproj_kernel(x_ref, hm_ref, hf_ref, wm_ref, wf_ref, o_ref):
    y = jnp.dot(hm_ref[...].astype(BF16), wm_ref[...], preferred_element_type=F32)
    y = y + jnp.dot(hf_ref[...].astype(BF16), wf_ref[...], preferred_element_type=F32)
    o_ref[...] = x_ref[...] + y


def _outproj(x, hm, hf, w_m, w_f, *, tm):
    M, D = x.shape
    tm = min(tm, M)
    assert M % tm == 0
    row = lambda i: (i, 0)
    const = lambda i: (0, 0)
    return pl.pallas_call(
        _outproj_kernel, grid=(M // tm,),
        in_specs=[pl.BlockSpec((tm, D), row), pl.BlockSpec((tm, hm.shape[1]), row),
                  pl.BlockSpec((tm, hf.shape[1]), row),
                  pl.BlockSpec(w_m.shape, const), pl.BlockSpec(w_f.shape, const)],
        out_specs=pl.BlockSpec((tm, D), row), out_shape=jax.ShapeDtypeStruct((M, D), F32),
        compiler_params=_cparams(("parallel",)), name="outproj")(x, hm, hf, w_m, w_f)


def _ffn_kernel(x_ref, g_ref, wgu_ref, wd_ref, *refs, moe, n_experts, final, cap):
    refs = list(refs)
    if moe:
        wr_ref, br_ref = refs[:2]
        refs = refs[2:]
    if final:
        gfin_ref = refs[0]
        refs = refs[1:]
    if moe:
        o_ref, hn_sc, acc_sc, comb_sc, rk_sc, rkT_sc, cnt_sc = refs
    else:
        o_ref, hn_sc, acc_sc = refs
    e = pl.program_id(1)
    tm = hn_sc.shape[0]
    chunk = wd_ref.shape[0]

    def swiglu(h):
        z = jnp.dot(h, wgu_ref[...], preferred_element_type=F32)
        a = z[:, :chunk]
        act = (a * _sigmoid(a) * z[:, chunk:]).astype(BF16)
        return jnp.dot(act, wd_ref[...], preferred_element_type=F32)

    @pl.when(e == 0)
    def _():
        hn = _rms(x_ref[...], g_ref[...])
        hn_sc[...] = hn.astype(BF16)
        acc_sc[...] = jnp.zeros_like(acc_sc)
        if moe:
            lane = lax.broadcasted_iota(jnp.int32, (tm, LANES), 1)
            lanef = lane.astype(F32)
            hn_hi = hn.astype(BF16)
            hn_lo = (hn - hn_hi.astype(F32)).astype(BF16)
            wr = wr_ref[...]
            wr_hi = wr.astype(BF16)
            wr_lo = (wr - wr_hi.astype(F32)).astype(BF16)
            lg = (jnp.dot(hn_hi, wr_hi, preferred_element_type=F32) + jnp.dot(hn_hi, wr_lo, preferred_element_type=F32)
                  + jnp.dot(hn_lo, wr_hi, preferred_element_type=F32)) + br_ref[...]
            lg = jnp.where(lane < n_experts, lg, -jnp.inf)
            m1 = jnp.max(lg, axis=1, keepdims=True)
            i1 = jnp.min(jnp.where(lg == m1, lanef, float(LANES)), axis=1, keepdims=True)
            lg2 = jnp.where(lanef == i1, -jnp.inf, lg)
            m2 = jnp.max(lg2, axis=1, keepdims=True)
            i2 = jnp.min(jnp.where(lg2 == m2, lanef, float(LANES)), axis=1, keepdims=True)
            e2 = jnp.exp(m2 - m1)
            g1 = 1.0 / (1.0 + e2)
            g2 = e2 / (1.0 + e2)
            comb_sc[...] = jnp.where(lanef == i1, g1, 0.0) + jnp.where(lanef == i2, g2, 0.0)
            routed = jnp.where((lanef == i1) | (lanef == i2), 1.0, 0.0)
            incl = _cumsum_rows(routed)
            rk = jnp.where(routed > 0.0, incl - 1.0, -1.0)
            rk_sc[...] = rk
            rkT_sc[...] = jnp.transpose(rk)
            cnt_sc[...] = jnp.broadcast_to(incl[tm - 1:tm, :], cnt_sc.shape)

    if not moe:
        acc_sc[...] += swiglu(hn_sc[...])
    else:
        lane = lax.broadcasted_iota(jnp.int32, (tm, LANES), 1)
        ce = jnp.sum(jnp.where(lane == e, comb_sc[...], 0.0), axis=1, keepdims=True)
        lane_row = lax.broadcasted_iota(jnp.int32, (1, LANES), 1)
        count = jnp.sum(jnp.where(lane_row == e, cnt_sc[0:1, :], 0.0))

        @pl.when(count > cap)
        def _():
            rb = min(tm, FULL_TILE_ROWS)
            for r in range(tm // rb):
                rows = slice(r * rb, (r + 1) * rb)
                acc_sc[rows, :] += ce[rows, :] * swiglu(hn_sc[rows, :])

        @pl.when(count <= cap)
        def _():
            rk_row = rkT_sc[pl.ds(e, 1), :]
            slot = lax.broadcasted_iota(jnp.int32, (cap, tm), 0).astype(F32)
            pick = jnp.where(rk_row == slot, 1.0, 0.0).astype(BF16)
            xs = jnp.dot(pick, hn_sc[...], preferred_element_type=F32).astype(BF16)
            y = swiglu(xs).astype(BF16)
            rk_col = jnp.sum(jnp.where(lane == e, rk_sc[...], 0.0), axis=1, keepdims=True)
            slot_l = lax.broadcasted_iota(jnp.int32, (tm, cap), 1).astype(F32)
            place = jnp.where(rk_col == slot_l, 1.0, 0.0).astype(BF16)
            acc_sc[...] += ce * jnp.dot(place, y, preferred_element_type=F32)

    @pl.when(e == pl.num_programs(1) - 1)
    def _():
        xo = x_ref[...] + acc_sc[...]
        if final:
            xo = _rms(xo, gfin_ref[...])
        o_ref[...] = xo


def _expert_capacity(tm, n_experts):
    mean = tm * TOP_K / n_experts
    return min(tm, max(64, int(-(-1.5 * mean // 64)) * 64))


def _ffn(x, g, wgu, wd, router, g_final, *, tm):
    M, D = x.shape
    tm = min(tm, M)
    assert M % tm == 0
    moe = router is not None
    final = g_final is not None
    E, chunk = wd.shape[0], wd.shape[1]
    row = lambda i, e: (i, 0)
    const = lambda i, e: (0, 0)
    args = [x, g, wgu, wd]
    in_specs = [pl.BlockSpec((tm, D), row), pl.BlockSpec((1, D), const),
                pl.BlockSpec((None, D, 2 * chunk), lambda i, e: (e, 0, 0)),
                pl.BlockSpec((None, chunk, D), lambda i, e: (e, 0, 0))]
    scratch = [pltpu.VMEM((tm, D), BF16), pltpu.VMEM((tm, D), F32)]
    if moe:
        args += list(router)
        in_specs += [pl.BlockSpec((D, LANES), const), pl.BlockSpec((1, LANES), const)]
        scratch += [pltpu.VMEM((tm, LANES), F32), pltpu.VMEM((tm, LANES), F32), pltpu.VMEM((LANES, tm), F32),
                    pltpu.VMEM((8, LANES), F32)]
    if final:
        args.append(g_final)
        in_specs.append(pl.BlockSpec((1, D), const))
    kern = functools.partial(_ffn_kernel, moe=moe, n_experts=E, final=final, cap=_expert_capacity(tm, E))
    return pl.pallas_call(
        kern, grid=(M // tm, E), in_specs=in_specs,
        out_specs=pl.BlockSpec((tm, D), row), out_shape=jax.ShapeDtypeStruct((M, D), F32),
        scratch_shapes=scratch, compiler_params=_cparams(("parallel", "arbitrary")),
        name="ffn_moe" if moe else "ffn_dense")(*args)


def _mlstm_step_kernel(q_ref, k_ref, v_ref, o_ref, kc_ref, G_ref, C_ref, n_ref, m_ref, gm_ref, *refs,
                       NB, H, DK, DV, c_ig, c_lf):
    hm_ref, Cn_ref, nn_ref, mn_ref = refs[-4:]
    sub = lax.broadcasted_iota(jnp.int32, (NB, DV), 0)
    lane = lax.broadcasted_iota(jnp.int32, (NB, LANES), 1)
    m_out = jnp.zeros((NB, LANES), F32)
    for h in range(H):
        q = q_ref[:, h * DK:(h + 1) * DK]
        k = k_ref[:, h * DK:(h + 1) * DK]
        v = v_ref[:, h * DV:(h + 1) * DV]
        n = n_ref[:, h * DK:(h + 1) * DK]
        ig = G_ref[:, c_ig + h:c_ig + h + 1]
        lf = G_ref[:, c_lf + h:c_lf + h + 1]
        m0 = m_ref[:, h:h + 1]
        inter = lf + m0
        m_new = jnp.maximum(inter, ig)
        w_inter = jnp.exp(inter - m_new)
        w_intra = jnp.exp(ig - m_new)
        s = jnp.sum(q * k, axis=1, keepdims=True) * w_intra
        qn = jnp.sum(q * n, axis=1, keepdims=True)
        vw = v * w_intra
        kc = kc_ref[h * DK:(h + 1) * DK, :]
        qC = jnp.zeros((NB, DV), F32)
        for bb in range(NB):
            C = C_ref[bb, h]
            qC = jnp.where(sub == bb, jnp.dot(q, C, preferred_element_type=F32), qC)
            vsel = jnp.concatenate([jnp.where(sub == bb, vw, 0.0), jnp.zeros((LANES - NB, DV), F32)], axis=0)
            Cn_ref[bb, h] = w_inter[bb:bb + 1, :] * C + jnp.dot(kc, vsel, preferred_element_type=F32)
        num = w_inter * qC + s * v
        den = w_inter * qn + s
        hval = num / jnp.maximum(jnp.abs(den), jnp.exp(-m_new))
        nn_ref[:, h * DK:(h + 1) * DK] = w_inter * n + w_intra * k
        m_out = jnp.where(lane == h, m_new, m_out)
        gate = _sigmoid(o_ref[:, h * DV:(h + 1) * DV])
        hm_ref[:, h * DV:(h + 1) * DV] = _rms(hval, gm_ref[:, h * DV:(h + 1) * DV]) * gate
    mn_ref[...] = m_out


def _mlstm_step(q, k, v, o, G, state_C, state_n, state_m, layer, g_m, *, NB, H, DK, DV, c_ig, c_lf, carried=()):
    DB = q.shape[0]
    depth = state_C.shape[0]
    NB = min(NB, DB)
    assert DB % NB == 0 and NB <= LANES
    S = DB // NB
    kc = jnp.pad(k.reshape(S, NB, H * DK).transpose(0, 2, 1), ((0, 0), (0, 0), (0, LANES - NB)))
    row = lambda s: (s, 0)
    args = [q, k, v, o, kc, G, state_C, state_n, state_m, g_m]
    in_specs = [pl.BlockSpec((NB, H * DK), row), pl.BlockSpec((NB, H * DK), row),
                pl.BlockSpec((NB, H * DV), row), pl.BlockSpec((NB, H * DV), row),
                pl.BlockSpec((None, H * DK, LANES), lambda s: (s, 0, 0)),
                pl.BlockSpec((NB, LANES), row),
                pl.BlockSpec((None, NB, H, DK, DV), lambda s: (layer, s, 0, 0, 0)),
                pl.BlockSpec((None, NB, H * DK), lambda s: (layer, s, 0)),
                pl.BlockSpec((None, NB, H), lambda s: (layer, s, 0)),
                pl.BlockSpec((1, H * DV), lambda s: (0, 0))]
    aliases = {}
    for buf in carried:
        aliases[len(args)] = 1
        args.append(buf)
        in_specs.append(pl.BlockSpec(memory_space=pl.ANY))
    kern = functools.partial(_mlstm_step_kernel, NB=NB, H=H, DK=DK, DV=DV, c_ig=c_ig, c_lf=c_lf)
    return pl.pallas_call(
        kern, grid=(S,), in_specs=in_specs,
        out_specs=[pl.BlockSpec((NB, H * DV), row),
                   pl.BlockSpec((None, NB, H, DK, DV), lambda s: (layer, s, 0, 0, 0)),
                   pl.BlockSpec((NB, H * DK), row), pl.BlockSpec((NB, LANES), row)],
        out_shape=[jax.ShapeDtypeStruct((DB, H * DV), F32), jax.ShapeDtypeStruct((depth, DB, H, DK, DV), F32),
                   jax.ShapeDtypeStruct((DB, H * DK), F32), jax.ShapeDtypeStruct((DB, LANES), F32)],
        input_output_aliases=aliases, compiler_params=_cparams(("parallel",)),
        name="mlstm_step")(*args)


def _fox_step_kernel(pt_ref, q_ref, kc_ref, vc_ref, G_ref, gf_ref, *refs, NP, HF, HD, PAGE):
    k_refs = refs[:NP]
    v_refs = refs[NP:2 * NP]
    lf_refs = refs[2 * NP:3 * NP]
    o_ref = refs[3 * NP]
    DF = HF * HD
    sub = lax.broadcasted_iota(jnp.int32, (HF, DF), 0)
    lane = lax.broadcasted_iota(jnp.int32, (HF, DF), 1)
    diag = lane // HD == sub
    q = q_ref[...]
    qbd = jnp.where(diag, jnp.broadcast_to(q, (HF, DF)), 0.0)
    qbd16 = qbd.astype(BF16)
    sub_g = lax.broadcasted_iota(jnp.int32, (HF, LANES), 0)
    lane_g = lax.broadcasted_iota(jnp.int32, (HF, LANES), 1)
    carry = jnp.sum(jnp.where(lane_g == sub_g, jnp.broadcast_to(G_ref[...], (HF, LANES)), 0.0),
                    axis=1, keepdims=True)
    ti = lax.broadcasted_iota(jnp.int32, (PAGE, PAGE), 0)
    tj = lax.broadcasted_iota(jnp.int32, (PAGE, PAGE), 1)
    later = jnp.where(ti > tj, 1.0, 0.0)
    s_cur = jnp.sum(qbd * kc_ref[...], axis=1, keepdims=True)
    scores = [None] * NP
    mx = s_cur
    for p in range(NP - 1, -1, -1):
        lfT = lf_refs[p][...]
        suf = jnp.dot(lfT, later, preferred_element_type=F32, precision=lax.Precision.HIGHEST) + carry
        carry = carry + jnp.sum(lfT, axis=1, keepdims=True)
        kT = k_refs[p][...].astype(BF16)
        s = jnp.dot(qbd16, kT, preferred_element_type=F32) + suf
        scores[p] = s
        mx = jnp.maximum(mx, jnp.max(s, axis=1, keepdims=True))
    p_cur = jnp.exp(s_cur - mx)
    l = p_cur
    acc = p_cur * vc_ref[...]
    for p in range(NP):
        pr = jnp.exp(scores[p] - mx)
        l = l + jnp.sum(pr, axis=1, keepdims=True)
        vT = v_refs[p][...].astype(BF16)
        acc = acc + lax.dot_general(pr.astype(BF16), vT, (((1,), (1,)), ((), ())), preferred_element_type=F32)
    o = acc / l
    ms = jnp.sum(jnp.where(diag, o * o, 0.0), axis=1, keepdims=True) * (1.0 / HD)
    on = jnp.where(diag, o * lax.rsqrt(ms + EPS), 0.0)
    o_ref[...] = jnp.sum(on, axis=0, keepdims=True) * gf_ref[...]


def _fox_step(q, k_cur, v_cur, lf_cur, cache_kT, cache_vT, cache_lfT, page_table, layer, g_f, *, HF, HD):
    DB = q.shape[0]
    NP = page_table.shape[1]
    PAGE = cache_kT.shape[3]
    DF = HF * HD
    pt = page_table.reshape(-1)
    r3 = lambda a: a.reshape(DB, 1, a.shape[-1])
    cur = lambda b, pt: (b, 0, 0)

    def page_map(p):
        return lambda b, pt: (layer, pt[b * NP + p], 0, 0)

    kv_specs = [pl.BlockSpec((None, None, DF, PAGE), page_map(p)) for p in range(NP)]
    lf_specs = [pl.BlockSpec((None, None, HF, PAGE), page_map(p)) for p in range(NP)]
    grid_spec = pltpu.PrefetchScalarGridSpec(
        num_scalar_prefetch=1, grid=(DB,),
        in_specs=[pl.BlockSpec((None, 1, DF), cur)] * 3 + [pl.BlockSpec((None, 1, LANES), cur),
                                                           pl.BlockSpec((1, DF), lambda b, pt: (0, 0))]
                 + kv_specs + kv_specs + lf_specs,
        out_specs=pl.BlockSpec((None, 1, DF), cur))
    kern = functools.partial(_fox_step_kernel, NP=NP, HF=HF, HD=HD, PAGE=PAGE)
    out = pl.pallas_call(
        kern, grid_spec=grid_spec, out_shape=jax.ShapeDtypeStruct((DB, 1, DF), F32),
        compiler_params=_cparams(("arbitrary",)),
        name="fox_step")(pt, r3(q), r3(k_cur), r3(v_cur), r3(lf_cur), g_f,
                         *([cache_kT] * NP), *([cache_vT] * NP), *([cache_lfT] * NP))
    return out.reshape(DB, DF)


def kernel(x_prompt, x_sample, state_C, state_n, state_m, cache_k, cache_v, cache_logf, page_table, g_mix_norm, w_in, b_ig, b_fg, b_ff, g_out_m, g_out_f, w_out, g_ffn_norm, w_ffn_gate, w_ffn_up, w_ffn_down, w_router, b_router, w_moe_gate, w_moe_up, w_moe_down, g_final):
    B, T, D = x_prompt.shape
    DB = x_sample.shape[0]
    assert x_sample.shape[1] == 1
    depth = w_in.shape[0]
    HM, DK, DV = state_C.shape[2:]
    PAGE, HF, HD = cache_k.shape[2:]
    n_pool = cache_k.shape[1]
    dq, dv, df = HM * DK, HM * DV, HF * HD
    assert dq % LANES == 0 and dv % LANES == 0 and df % LANES == 0 and LANES % HD == 0
    assert HF + 2 * HM <= GATE_ROWS
    dims = (dq, dv, df, HF, HM, DK, HD)
    c_ig, c_lf = HF, HF + HM
    E = w_router.shape[-1]
    chunk = w_moe_gate.shape[-1]
    M = B * T

    o = np.cumsum([0, dq, dq, dv, dv, HM, HM, df, df, df, HF]).tolist()
    w_m = w_in[:, :, o[0]:o[4]].astype(BF16)
    w_f = w_in[:, :, o[6]:o[9]].astype(BF16)
    scat = np.zeros((N_SPLIT, LANES, HF * LANES), np.float32)
    for t in range(N_SPLIT):
        scat[t, np.arange(HF), np.arange(HF) * LANES + HD + t] = 1.0
    scat = jnp.asarray(scat, BF16)
    n_gate = HF + 2 * HM
    w_gate = jnp.concatenate([w_in[:, :, o[9]:o[10]], w_in[:, :, o[4]:o[6]],
                              jnp.zeros((depth, D, LANES - n_gate), w_in.dtype)], axis=-1).astype(BF16)
    b_gate = jnp.concatenate([b_ff, b_ig, b_fg, jnp.zeros((depth, LANES - n_gate), F32)],
                             axis=-1).astype(F32).reshape(depth, 1, LANES)
    w_out16 = w_out.astype(BF16)
    n_dense, _, d_ff = w_ffn_gate.shape
    assert d_ff % chunk == 0
    nc = d_ff // chunk
    split = lambda w: w.astype(BF16).reshape(n_dense, D, nc, chunk)
    wfgu = jnp.concatenate([split(w_ffn_gate), split(w_ffn_up)], axis=-1).transpose(0, 2, 1, 3)
    wfd = w_ffn_down.astype(BF16).reshape(n_dense, nc, chunk, D)
    wmgu = jnp.concatenate([w_moe_gate.astype(BF16), w_moe_up.astype(BF16)], axis=-1)
    wmd = w_moe_down.astype(BF16)
    w_r = jnp.concatenate([w_router, jnp.zeros(w_router.shape[:2] + (LANES - E,), F32)], axis=-1)
    b_r = jnp.concatenate([b_router, jnp.zeros((b_router.shape[0], LANES - E), F32)],
                          axis=-1).reshape(-1, 1, LANES)
    ckT = jnp.transpose(cache_k, (0, 1, 3, 4, 2)).reshape(depth, n_pool, df, PAGE)
    cvT = jnp.transpose(cache_v, (0, 1, 3, 4, 2)).reshape(depth, n_pool, df, PAGE)
    clfT = jnp.swapaxes(cache_logf, 2, 3)
    sn = state_n.reshape(depth, DB, dq)

    xp = x_prompt.reshape(M, D)
    xs = x_sample.reshape(DB, D)
    pC, pn, pm, plf = [], [], [], []
    kv_acc = ()
    sn_o, sm, sk, sv, slf = [], [], [], [], []
    sC_acc = ()
    for l in range(depth):
        gmix = g_mix_norm[l].reshape(1, D)
        gm = g_out_m[l].reshape(1, dv)
        gf = g_out_f[l].reshape(1, df)
        w_om, w_of = w_out16[l, :dv], w_out16[l, dv:]
        (qm, vm, kT, om, qp, kp, vp, kf, vf, G, Gc, GT, GcT) = _inproj(
            xp, gmix, w_m[l], w_f[l], w_gate[l], b_gate[l], scat, dims, prompt=True, tm=512, seq_len=T,
            layer=l, depth=depth, carried=kv_acc)
        kv_acc = (kf, vf)
        hm, caug, mout = _mlstm_prompt(qm, vm, kT, om, G, Gc, GT, GcT, gm, B=B, T=T, H=HM, DK=DK, DV=DV,
                                       L=min(512, T), c_ig=c_ig, c_lf=c_lf)
        hf = _fox_prompt(qp, kp, vp, gf, B=B, T=T, HF=HF, HD=HD, TB=512, G=HF, RQ=256)
        xp = _outproj(xp, hm, hf, w_om, w_of, tm=512)
        pC.append(caug[..., :DV]); pn.append(caug[..., DV]); pm.append(mout[:, :HM, 0])
        plf.append(G[:, :HF].reshape(B, T, HF))
        (qm, km, vm, om, qf, kf, vf, G) = _inproj(
            xs, gmix, w_m[l], w_f[l], w_gate[l], b_gate[l], None, dims, prompt=False, tm=DB, seq_len=1)
        hm, Cn, nn, mn = _mlstm_step(qm, km, vm, om, G, state_C, sn, state_m, l, gm,
                                     NB=8, H=HM, DK=DK, DV=DV, c_ig=c_ig, c_lf=c_lf, carried=sC_acc)
        sC_acc = (Cn,)
        hf = _fox_step(qf, kf, vf, G, ckT, cvT, clfT, page_table, l, gf, HF=HF, HD=HD)
        xs = _outproj(xs, hm, hf, w_om, w_of, tm=DB)
        sn_o.append(nn.reshape(DB, HM, DK)); sm.append(mn[:, :HM])
        sk.append(kf.reshape(DB, 1, HF, HD)); sv.append(vf.reshape(DB, 1, HF, HD))
        slf.append(G[:, :HF].reshape(DB, 1, HF))
        gffn = g_ffn_norm[l].reshape(1, D)
        gfin = g_final.reshape(1, D) if l == depth - 1 else None
        j = l // 2
        if l % 2 == 0:
            args = (wfgu[j], wfd[j], None)
        else:
            args = (wmgu[j], wmd[j], (w_r[j], b_r[j]))
        xp = _ffn(xp, gffn, *args, gfin, tm=512 if l % 2 == 0 else 1024)
        xs = _ffn(xs, gffn, *args, gfin, tm=DB)

    tok_major = lambda a: a.reshape(depth, B, HF, HD, T).transpose(0, 1, 4, 2, 3)
    return (xp.reshape(B, T, D), xs.reshape(DB, 1, D),
            jnp.stack(pC), jnp.stack(pn), jnp.stack(pm),
            tok_major(kv_acc[0]), tok_major(kv_acc[1]), jnp.stack(plf),
            sC_acc[0], jnp.stack(sn_o), jnp.stack(sm),
            jnp.stack(sk), jnp.stack(sv), jnp.stack(slf))
```

```python
import functools

import numpy as np
import jax
import jax.numpy as jnp
from jax import lax
from jax.experimental import pallas as pl
from jax.experimental.pallas import tpu as pltpu

F32 = jnp.float32
BF16 = jnp.bfloat16
EPS = 1e-6
NEG_INIT = -1e30
TOP_K = 2
LANES = 128
GATE_ROWS = 16
VMEM_LIMIT = 56 * 1024 * 1024
LOG2E = 1.4426950408889634
N_SPLIT = 3
FULL_TILE_ROWS = 512


def _cparams(sem):
    return pltpu.CompilerParams(dimension_semantics=sem, vmem_limit_bytes=VMEM_LIMIT)


def _rms(x, g):
    var = jnp.mean(x * x, axis=-1, keepdims=True)
    return x * lax.rsqrt(var + EPS) * g


def _log_sigmoid(x):
    return jnp.minimum(x, 0.0) - jnp.log(1.0 + jnp.exp(-jnp.abs(x)))


def _sigmoid(x):
    return 1.0 / (1.0 + jnp.exp(-x))


def _cumsum_rows(a):
    n = a.shape[0]
    row = lax.broadcasted_iota(jnp.int32, a.shape, 0)
    k = 1
    while k < n:
        a = a + jnp.where(row >= k, pltpu.roll(a, k, 0), 0.0)
        k *= 2
    return a


def _spread_heads(z, hd, tail):
    lane = lax.broadcasted_iota(jnp.int32, (1, LANES), 1)
    blocks = []
    for p in range(z.shape[1] // LANES):
        a = z[:, p * LANES:(p + 1) * LANES]
        blocks.append(jnp.where(lane < hd, a, tail(2 * p)))
        blocks.append(jnp.where(lane < hd, pltpu.roll(a, hd, 1), tail(2 * p + 1)))
    return jnp.concatenate(blocks, axis=1)


def _inproj_kernel(x_ref, g_ref, wm_ref, wf_ref, wg_ref, bg_ref, *refs, dq, dv, df, hf, hm, hd, kscale, qscale,
                   prompt, tiles_per_seq, n_carried):
    if prompt:
        (scat_ref, qm_ref, vm_ref, kT_ref, om_ref, qp_ref, kp_ref, vp_ref, kf_ref, vf_ref,
         G_ref, Gc_ref, GT_ref, GcT_ref, carry_ref) = refs[:1] + refs[1 + n_carried:]
    else:
        (qm_ref, km_ref, vm_ref, om_ref, qf_ref, kf_ref, vf_ref, G_ref) = refs
    hn = _rms(x_ref[...], g_ref[...]).astype(BF16)

    def proj(w_ref, off, width):
        return jnp.dot(hn, w_ref[:, off:off + width], preferred_element_type=F32)

    zg = jnp.dot(hn, wg_ref[...], preferred_element_type=F32) + bg_ref[...]
    col = lax.broadcasted_iota(jnp.int32, zg.shape, 1)
    is_log = (col < hf) | ((col >= hf + hm) & (col < hf + 2 * hm))
    is_lin = (col >= hf) & (col < hf + hm)
    G = jnp.where(is_log, _log_sigmoid(zg), jnp.where(is_lin, zg, 0.0))
    G_ref[...] = G

    qm_ref[...] = proj(wm_ref, 0, dq).astype(qm_ref.dtype)
    km = proj(wm_ref, dq, dq) * kscale
    if prompt:
        kT_ref[...] = jnp.transpose(km).astype(kT_ref.dtype)
    else:
        km_ref[...] = km
    vm_ref[...] = proj(wm_ref, 2 * dq, dv).astype(vm_ref.dtype)
    om_ref[...] = proj(wm_ref, 2 * dq + dv, dv).astype(om_ref.dtype)
    if not prompt:
        qf_ref[...] = proj(wf_ref, 0, df) * qscale
        kf_ref[...] = proj(wf_ref, df, df)
        vf_ref[...] = proj(wf_ref, 2 * df, df)
        return

    i = pl.program_id(0)

    @pl.when(i % tiles_per_seq == 0)
    def _():
        carry_ref[...] = jnp.zeros_like(carry_ref)

    Gc = _cumsum_rows(G) + carry_ref[0:1, :]
    carry_ref[0:1, :] = Gc[-1:, :]
    Gc_ref[...] = Gc
    GT_ref[...] = jnp.transpose(G)[:GATE_ROWS, :]
    GcT_ref[...] = jnp.transpose(Gc)[:GATE_ROWS, :]

    lane = lax.broadcasted_iota(jnp.int32, (1, LANES), 1)
    zq = proj(wf_ref, 0, df) * (qscale * LOG2E)
    zk = proj(wf_ref, df, df)
    zv = proj(wf_ref, 2 * df, df)
    kf_ref[...] = jnp.transpose(zk)
    vf_ref[...] = jnp.transpose(zv)
    rem = Gc * (-LOG2E)
    bias = jnp.zeros((zk.shape[0], hf * LANES), F32)
    for t in range(N_SPLIT):
        piece = rem.astype(BF16)
        rem = rem - piece.astype(F32)
        bias = bias + jnp.dot(piece, scat_ref[t], preferred_element_type=F32)
    q_tail = jnp.where((lane >= hd) & (lane < hd + N_SPLIT), 1.0, 0.0)
    v_tail = jnp.where(lane == hd, 1.0, 0.0)
    qp_ref[...] = _spread_heads(zq, hd, lambda h: q_tail).astype(BF16)
    kp_ref[...] = _spread_heads(zk, hd, lambda h: bias[:, h * LANES:(h + 1) * LANES]).astype(BF16)
    vp_ref[...] = _spread_heads(zv, hd, lambda h: v_tail).astype(BF16)


def _inproj(x, g, w_m, w_f, w_gate, b_gate, scat, dims, *, prompt, tm, seq_len, layer=0, depth=1, carried=()):
    M, D = x.shape
    dq, dv, df, hf, hm, dk, hd = dims
    dfp = hf * LANES
    tm = min(tm, M)
    assert M % tm == 0
    if prompt:
        assert seq_len % tm == 0
    row = lambda i: (i, 0)
    colb = lambda i: (0, i)
    const = lambda i: (0, 0)
    tps = max(seq_len // tm, 1)
    seqb = lambda i: (layer, i // tps, 0, i % tps)
    aliases = {}
    args = [x, g, w_m, w_f, w_gate, b_gate]
    in_specs = [pl.BlockSpec((tm, D), row), pl.BlockSpec((1, D), const),
                pl.BlockSpec(w_m.shape, const), pl.BlockSpec(w_f.shape, const),
                pl.BlockSpec((D, LANES), const), pl.BlockSpec((1, LANES), const)]
    if prompt:
        args.append(scat)
        in_specs.append(pl.BlockSpec(scat.shape, lambda i: (0, 0, 0)))
        kv_out = 7
        for n, buf in enumerate(carried):
            aliases[len(args)] = kv_out + n
            args.append(buf)
            in_specs.append(pl.BlockSpec(memory_space=pl.ANY))
        out_shape = [jax.ShapeDtypeStruct((M, dq), BF16), jax.ShapeDtypeStruct((M, dv), BF16),
                     jax.ShapeDtypeStruct((dq, M), BF16), jax.ShapeDtypeStruct((M, dv), BF16),
                     jax.ShapeDtypeStruct((M, dfp), BF16), jax.ShapeDtypeStruct((M, dfp), BF16),
                     jax.ShapeDtypeStruct((M, dfp), BF16),
                     jax.ShapeDtypeStruct((depth, M // seq_len, df, seq_len), F32),
                     jax.ShapeDtypeStruct((depth, M // seq_len, df, seq_len), F32),
                     jax.ShapeDtypeStruct((M, LANES), F32), jax.ShapeDtypeStruct((M, LANES), F32),
                     jax.ShapeDtypeStruct((GATE_ROWS, M), F32), jax.ShapeDtypeStruct((GATE_ROWS, M), F32)]
        out_specs = [pl.BlockSpec((tm, dq), row), pl.BlockSpec((tm, dv), row),
                     pl.BlockSpec((dq, tm), colb), pl.BlockSpec((tm, dv), row),
                     pl.BlockSpec((tm, dfp), row), pl.BlockSpec((tm, dfp), row), pl.BlockSpec((tm, dfp), row),
                     pl.BlockSpec((None, None, df, tm), seqb), pl.BlockSpec((None, None, df, tm), seqb),
                     pl.BlockSpec((tm, LANES), row), pl.BlockSpec((tm, LANES), row),
                     pl.BlockSpec((GATE_ROWS, tm), colb), pl.BlockSpec((GATE_ROWS, tm), colb)]
        scratch = [pltpu.VMEM((8, LANES), F32)]
    else:
        out_shape = ([jax.ShapeDtypeStruct((M, dq), F32)] * 2 + [jax.ShapeDtypeStruct((M, dv), F32)] * 2
                     + [jax.ShapeDtypeStruct((M, df), F32)] * 3 + [jax.ShapeDtypeStruct((M, LANES), F32)])
        out_specs = ([pl.BlockSpec((tm, dq), row)] * 2 + [pl.BlockSpec((tm, dv), row)] * 2
                     + [pl.BlockSpec((tm, df), row)] * 3 + [pl.BlockSpec((tm, LANES), row)])
        scratch = []
    kern = functools.partial(_inproj_kernel, dq=dq, dv=dv, df=df, hf=hf, hm=hm, hd=hd,
                             kscale=float(dk) ** -0.5, qscale=float(hd) ** -0.5,
                             prompt=prompt, tiles_per_seq=max(seq_len // tm, 1), n_carried=len(carried))
    return pl.pallas_call(
        kern, grid=(M // tm,), in_specs=in_specs, out_specs=out_specs, out_shape=out_shape,
        input_output_aliases=aliases, scratch_shapes=scratch, compiler_params=_cparams(("arbitrary",)),
        name="inproj_prompt" if prompt else "inproj_sample")(*args)


def _mlstm_prompt_kernel(q_ref, v_ref, kT_ref, o_ref, G_ref, Gc_ref, GT_ref, GcT_ref, gm_ref,
                         hm_ref, caug_ref, mout_ref, bprev_ref, m_ref, *, H, DK, DV, L, c_ig, c_lf):
    c = pl.program_id(1)

    @pl.when(c == 0)
    def _():
        caug_ref[...] = jnp.zeros_like(caug_ref)
        m_ref[...] = jnp.full_like(m_ref, NEG_INIT)
        bprev_ref[...] = jnp.zeros_like(bprev_ref)

    row = lax.broadcasted_iota(jnp.int32, (L, L), 0)
    colL = lax.broadcasted_iota(jnp.int32, (L, L), 1)
    causal = row >= colL
    lane = lax.broadcasted_iota(jnp.int32, (L, LANES), 1)
    ones_col = jnp.where(lane == 0, 1.0, 0.0)
    G = G_ref[...]
    Gc = Gc_ref[...]
    GT = GT_ref[...]
    GcT = GcT_ref[...]
    for h in range(H):
        bp = bprev_ref[0:1, c_lf + h:c_lf + h + 1]
        b_col = Gc[:, c_lf + h:c_lf + h + 1] - bp
        b_row = GcT[c_lf + h:c_lf + h + 1, :] - bp
        i_row = GT[c_ig + h:c_ig + h + 1, :]
        i_col = G[:, c_ig + h:c_ig + h + 1]
        m_prev = m_ref[h:h + 1, 0:1]
        dm = jnp.where(causal, b_col - b_row + i_row, -jnp.inf)
        inter = b_col + m_prev
        m_row = jnp.maximum(inter, jnp.max(dm, axis=-1, keepdims=True))
        w_intra = jnp.exp(dm - m_row)
        w_inter = jnp.exp(inter - m_row)
        qh = q_ref[:, h * DK:(h + 1) * DK]
        kTh = kT_ref[h * DK:(h + 1) * DK, :]
        vh = v_ref[:, h * DV:(h + 1) * DV].astype(F32)
        vaug = jnp.concatenate([vh, ones_col], axis=1)
        s = jnp.dot(qh, kTh, preferred_element_type=F32) * w_intra
        intra = jnp.dot(s.astype(BF16), vaug.astype(BF16), preferred_element_type=F32)
        cst = caug_ref[h]
        inter_qc = jnp.dot(qh, cst.astype(BF16), preferred_element_type=F32)
        r = w_inter * inter_qc + intra
        num = r[:, :DV]
        den = r[:, DV:DV + 1]
        hval = num / jnp.maximum(jnp.abs(den), jnp.exp(-m_row))
        m_new = m_row[L - 1:L, :]
        decay = jnp.exp(inter[L - 1:L, :] - m_new)
        w_k = jnp.exp(b_col[L - 1:L, :] - b_col + i_col - m_new)
        upd = jnp.dot(kTh, (vaug * w_k).astype(BF16), preferred_element_type=F32)
        caug_ref[h] = decay * cst + upd
        m_ref[h:h + 1, :] = jnp.broadcast_to(m_new, (1, LANES))
        gate = _sigmoid(o_ref[:, h * DV:(h + 1) * DV].astype(F32))
        hm_ref[:, h * DV:(h + 1) * DV] = (_rms(hval, gm_ref[:, h * DV:(h + 1) * DV]) * gate).astype(hm_ref.dtype)
    bprev_ref[0:1, :] = Gc[L - 1:L, :]

    @pl.when(c == pl.num_programs(1) - 1)
    def _():
        mout_ref[...] = m_ref[...]


def _mlstm_prompt(qm, vm, kT, om, G, Gc, GT, GcT, g_m, *, B, T, H, DK, DV, L, c_ig, c_lf):
    M = B * T
    NC = T // L
    assert T % L == 0
    rowb = lambda b, c: (b * NC + c, 0)
    colb = lambda b, c: (0, b * NC + c)
    kern = functools.partial(_mlstm_prompt_kernel, H=H, DK=DK, DV=DV, L=L, c_ig=c_ig, c_lf=c_lf)
    return pl.pallas_call(
        kern, grid=(B, NC),
        in_specs=[pl.BlockSpec((L, H * DK), rowb), pl.BlockSpec((L, H * DV), rowb),
                  pl.BlockSpec((H * DK, L), colb), pl.BlockSpec((L, H * DV), rowb),
                  pl.BlockSpec((L, LANES), rowb), pl.BlockSpec((L, LANES), rowb),
                  pl.BlockSpec((GATE_ROWS, L), colb), pl.BlockSpec((GATE_ROWS, L), colb),
                  pl.BlockSpec((1, H * DV), lambda b, c: (0, 0))],
        out_specs=[pl.BlockSpec((L, H * DV), rowb),
                   pl.BlockSpec((None, H, DK, DV + LANES), lambda b, c: (b, 0, 0, 0)),
                   pl.BlockSpec((None, 8, LANES), lambda b, c: (b, 0, 0))],
        out_shape=[jax.ShapeDtypeStruct((M, H * DV), BF16),
                   jax.ShapeDtypeStruct((B, H, DK, DV + LANES), F32),
                   jax.ShapeDtypeStruct((B, 8, LANES), F32)],
        scratch_shapes=[pltpu.VMEM((8, LANES), F32), pltpu.VMEM((8, LANES), F32)],
        compiler_params=_cparams(("parallel", "arbitrary")),
        name="mlstm_prompt")(qm, vm, kT, om, G, Gc, GT, GcT, g_m)


def _fox_prompt_kernel(ii_ref, jj_ref, q_ref, k_ref, v_ref, gf_ref, o_ref, m_sc, acc_sc, *, TB, HD, G, RQ, KS):
    p = pl.program_id(2)
    i = ii_ref[p]
    j = jj_ref[p]
    FULL, MASKED = 0, 1

    @pl.when(j == 0)
    def _():
        m_sc[...] = jnp.full_like(m_sc, -jnp.inf)
        acc_sc[...] = jnp.zeros_like(acc_sc)

    def update(modes):
        items = [(c, g, r) for c in range(len(modes)) for g in range(G) for r in range(TB // RQ)]

        def qk(c, g, r):
            q = q_ref[r * RQ:(r + 1) * RQ, g * LANES:(g + 1) * LANES]
            k = k_ref[c * TB:(c + 1) * TB, g * LANES:(g + 1) * LANES]
            return lax.dot_general(q, k, (((1,), (1,)), ((), ())), preferred_element_type=F32)

        s_next = qk(*items[0])
        for n, (c, g, r) in enumerate(items):
            s = s_next
            if n + 1 < len(items):
                s_next = qk(*items[n + 1])
            rows = slice(r * RQ, (r + 1) * RQ)
            v = v_ref[c * TB:(c + 1) * TB, g * LANES:(g + 1) * LANES]
            if modes[c] == MASKED:
                rr = lax.broadcasted_iota(jnp.int32, (RQ, TB), 0) + r * RQ
                cc = lax.broadcasted_iota(jnp.int32, (RQ, TB), 1)
                s = jnp.where(rr >= cc, s, -jnp.inf)
            m_old = m_sc[g, rows, :]
            m_new = jnp.maximum(m_old, jnp.max(s, axis=1, keepdims=True))
            alpha = jnp.exp2(m_old - m_new)
            pr = jnp.concatenate([jnp.exp2(s[:, cl * LANES:(cl + 1) * LANES] - m_new).astype(BF16)
                                  for cl in range(TB // LANES)], axis=1)
            acc_sc[g, rows, :] = alpha * acc_sc[g, rows, :] + jnp.dot(pr, v, preferred_element_type=F32)
            m_sc[g, rows, :] = m_new

    last = i // KS

    @pl.when(j < last)
    def _():
        update([FULL] * KS)

    for d in range(KS):
        @pl.when((j == last) & (i % KS == d))
        def _():
            update([FULL] * d + [MASKED])

    @pl.when(j == last)
    def _():
        lane = lax.broadcasted_iota(jnp.int32, (1, LANES), 1)
        lrow = lax.broadcasted_iota(jnp.int32, (LANES, LANES), 0)
        wmat = jnp.where(lrow < HD, 1.0 / HD, jnp.where(lrow == HD, EPS, 0.0))
        for gp in range(G // 2):
            outs = []
            for g in (2 * gp, 2 * gp + 1):
                a = acc_sc[g]
                outs.append(a * lax.rsqrt(jnp.dot(a * a, wmat, preferred_element_type=F32)))
            out = jnp.where(lane < HD, outs[0], pltpu.roll(outs[1], HD, 1))
            o_ref[:, gp * LANES:(gp + 1) * LANES] = (out * gf_ref[:, gp * LANES:(gp + 1) * LANES]).astype(o_ref.dtype)


def _fox_prompt(qp, kp, vp, g_f, *, B, T, HF, HD, TB, G, RQ, KS):
    M = B * T
    TB = min(TB, T)
    RQ = min(RQ, TB)
    KS = min(KS, T // TB)
    TK = KS * TB
    assert T % TK == 0 and TB % RQ == 0 and HF % G == 0 and G % 2 == 0 and 2 * HD == LANES
    nq = T // TB
    nkv = T // TK
    ii = np.concatenate([np.full(i // KS + 1, i, np.int32) for i in range(nq)])
    jj = np.concatenate([np.arange(i // KS + 1, dtype=np.int32) for i in range(nq)])
    qmap = lambda b, hg, p, ii, jj: (b * nq + ii[p], hg)
    kmap = lambda b, hg, p, ii, jj: (b * nkv + jj[p], hg)
    kern = functools.partial(_fox_prompt_kernel, TB=TB, HD=HD, G=G, RQ=RQ, KS=KS)
    grid_spec = pltpu.PrefetchScalarGridSpec(
        num_scalar_prefetch=2, grid=(B, HF // G, len(ii)),
        in_specs=[pl.BlockSpec((TB, G * LANES), qmap), pl.BlockSpec((TK, G * LANES), kmap),
                  pl.BlockSpec((TK, G * LANES), kmap),
                  pl.BlockSpec((1, G * HD), lambda b, hg, p, ii, jj: (0, hg))],
        out_specs=pl.BlockSpec((TB, G * HD), qmap),
        scratch_shapes=[pltpu.VMEM((G, TB, LANES), F32)] * 2)
    return pl.pallas_call(
        kern, grid_spec=grid_spec, out_shape=jax.ShapeDtypeStruct((M, HF * HD), BF16),
        compiler_params=_cparams(("parallel", "parallel", "arbitrary")),
        name="fox_prompt")(jnp.asarray(ii), jnp.asarray(jj), qp, kp, vp, g_f)


def _outproj_kernel(x_ref, hm_ref, hf_ref, wm_ref, wf_ref, o_ref):
    y = jnp.dot(hm_ref[...].astype(BF16), wm_ref[...], preferred_element_type=F32)
    y = y + jnp.dot(hf_ref[...].astype(BF16), wf_ref[...], preferred_element_type=F32)
    o_ref[...] = x_ref[...] + y


def _outproj(x, hm, hf, w_m, w_f, *, tm):
    M, D = x.shape
    tm = min(tm, M)
    assert M % tm == 0
    row = lambda i: (i, 0)
    const = lambda i: (0, 0)
    return pl.pallas_call(
        _outproj_kernel, grid=(M // tm,),
        in_specs=[pl.BlockSpec((tm, D), row), pl.BlockSpec((tm, hm.shape[1]), row),
                  pl.BlockSpec((tm, hf.shape[1]), row),
                  pl.BlockSpec(w_m.shape, const), pl.BlockSpec(w_f.shape, const)],
        out_specs=pl.BlockSpec((tm, D), row), out_shape=jax.ShapeDtypeStruct((M, D), F32),
        compiler_params=_cparams(("parallel",)), name="outproj")(x, hm, hf, w_m, w_f)


def _ffn_kernel(x_ref, g_ref, wgu_ref, wd_ref, *refs, moe, n_experts, final, cap):
    refs = list(refs)
    if moe:
        wr_ref, br_ref = refs[:2]
        refs = refs[2:]
    if final:
        gfin_ref = refs[0]
        refs = refs[1:]
    if moe:
        o_ref, hn_sc, acc_sc, comb_sc, rk_sc, rkT_sc, cnt_sc = refs
    else:
        o_ref, hn_sc, acc_sc = refs
    e = pl.program_id(1)
    tm = hn_sc.shape[0]
    chunk = wd_ref.shape[0]

    def swiglu(h):
        z = jnp.dot(h, wgu_ref[...], preferred_element_type=F32)
        a = z[:, :chunk]
        act = (a * _sigmoid(a) * z[:, chunk:]).astype(BF16)
        return jnp.dot(act, wd_ref[...], preferred_element_type=F32)

    @pl.when(e == 0)
    def _():
        hn = _rms(x_ref[...], g_ref[...])
        hn_sc[...] = hn.astype(BF16)
        acc_sc[...] = jnp.zeros_like(acc_sc)
        if moe:
            lane = lax.broadcasted_iota(jnp.int32, (tm, LANES), 1)
            lanef = lane.astype(F32)
            hn_hi = hn.astype(BF16)
            hn_lo = (hn - hn_hi.astype(F32)).astype(BF16)
            wr = wr_ref[...]
            wr_hi = wr.astype(BF16)
            wr_lo = (wr - wr_hi.astype(F32)).astype(BF16)
            lg = (jnp.dot(hn_hi, wr_hi, preferred_element_type=F32) + jnp.dot(hn_hi, wr_lo, preferred_element_type=F32)
                  + jnp.dot(hn_lo, wr_hi, preferred_element_type=F32)) + br_ref[...]
            lg = jnp.where(lane < n_experts, lg, -jnp.inf)
            m1 = jnp.max(lg, axis=1, keepdims=True)
            i1 = jnp.min(jnp.where(lg == m1, lanef, float(LANES)), axis=1, keepdims=True)
            lg2 = jnp.where(lanef == i1, -jnp.inf, lg)
            m2 = jnp.max(lg2, axis=1, keepdims=True)
            i2 = jnp.min(jnp.where(lg2 == m2, lanef, float(LANES)), axis=1, keepdims=True)
            e2 = jnp.exp(m2 - m1)
            g1 = 1.0 / (1.0 + e2)
            g2 = e2 / (1.0 + e2)
            comb_sc[...] = jnp.where(lanef == i1, g1, 0.0) + jnp.where(lanef == i2, g2, 0.0)
            routed = jnp.where((lanef == i1) | (lanef == i2), 1.0, 0.0)
            incl = _cumsum_rows(routed)
            rk = jnp.where(routed > 0.0, incl - 1.0, -1.0)
            rk_sc[...] = rk
            rkT_sc[...] = jnp.transpose(rk)
            cnt_sc[...] = jnp.broadcast_to(incl[tm - 1:tm, :], cnt_sc.shape)

    if not moe:
        acc_sc[...] += swiglu(hn_sc[...])
    else:
        lane = lax.broadcasted_iota(jnp.int32, (tm, LANES), 1)
        ce = jnp.sum(jnp.where(lane == e, comb_sc[...], 0.0), axis=1, keepdims=True)
        lane_row = lax.broadcasted_iota(jnp.int32, (1, LANES), 1)
        count = jnp.sum(jnp.where(lane_row == e, cnt_sc[0:1, :], 0.0))

        @pl.when(count > cap)
        def _():
            rb = min(tm, FULL_TILE_ROWS)
            for r in range(tm // rb):
                rows = slice(r * rb, (r + 1) * rb)
                acc_sc[rows, :] += ce[rows, :] * swiglu(hn_sc[rows, :])

        @pl.when(count <= cap)
        def _():
            rk_row = rkT_sc[pl.ds(e, 1), :]
            slot = lax.broadcasted_iota(jnp.int32, (cap, tm), 0).astype(F32)
            pick = jnp.where(rk_row == slot, 1.0, 0.0).astype(BF16)
            xs = jnp.dot(pick, hn_sc[...], preferred_element_type=F32).astype(BF16)
            y = swiglu(xs).astype(BF16)
            rk_col = jnp.sum(jnp.where(lane == e, rk_sc[...], 0.0), axis=1, keepdims=True)
            slot_l = lax.broadcasted_iota(jnp.int32, (tm, cap), 1).astype(F32)
            place = jnp.where(rk_col == slot_l, 1.0, 0.0).astype(BF16)
            acc_sc[...] += ce * jnp.dot(place, y, preferred_element_type=F32)

    @pl.when(e == pl.num_programs(1) - 1)
    def _():
        xo = x_ref[...] + acc_sc[...]
        if final:
            xo = _rms(xo, gfin_ref[...])
        o_ref[...] = xo


def _expert_capacity(tm, n_experts):
    mean = tm * TOP_K / n_experts
    return min(tm, max(64, int(-(-1.5 * mean // 64)) * 64))


def _ffn(x, g, wgu, wd, router, g_final, *, tm):
    M, D = x.shape
    tm = min(tm, M)
    assert M % tm == 0
    moe = router is not None
    final = g_final is not None
    E, chunk = wd.shape[0], wd.shape[1]
    row = lambda i, e: (i, 0)
    const = lambda i, e: (0, 0)
    args = [x, g, wgu, wd]
    in_specs = [pl.BlockSpec((tm, D), row), pl.BlockSpec((1, D), const),
                pl.BlockSpec((None, D, 2 * chunk), lambda i, e: (e, 0, 0)),
                pl.BlockSpec((None, chunk, D), lambda i, e: (e, 0, 0))]
    scratch = [pltpu.VMEM((tm, D), BF16), pltpu.VMEM((tm, D), F32)]
    if moe:
        args += list(router)
        in_specs += [pl.BlockSpec((D, LANES), const), pl.BlockSpec((1, LANES), const)]
        scratch += [pltpu.VMEM((tm, LANES), F32), pltpu.VMEM((tm, LANES), F32), pltpu.VMEM((LANES, tm), F32),
                    pltpu.VMEM((8, LANES), F32)]
    if final:
        args.append(g_final)
        in_specs.append(pl.BlockSpec((1, D), const))
    kern = functools.partial(_ffn_kernel, moe=moe, n_experts=E, final=final, cap=_expert_capacity(tm, E))
    return pl.pallas_call(
        kern, grid=(M // tm, E), in_specs=in_specs,
        out_specs=pl.BlockSpec((tm, D), row), out_shape=jax.ShapeDtypeStruct((M, D), F32),
        scratch_shapes=scratch, compiler_params=_cparams(("parallel", "arbitrary")),
        name="ffn_moe" if moe else "ffn_dense")(*args)


def _mlstm_step_kernel(q_ref, k_ref, v_ref, o_ref, kc_ref, G_ref, C_ref, n_ref, m_ref, gm_ref, *refs,
                       NB, H, DK, DV, c_ig, c_lf):
    hm_ref, Cn_ref, nn_ref, mn_ref = refs[-4:]
    sub = lax.broadcasted_iota(jnp.int32, (NB, DV), 0)
    lane = lax.broadcasted_iota(jnp.int32, (NB, LANES), 1)
    m_out = jnp.zeros((NB, LANES), F32)
    for h in range(H):
        q = q_ref[:, h * DK:(h + 1) * DK]
        k = k_ref[:, h * DK:(h + 1) * DK]
        v = v_ref[:, h * DV:(h + 1) * DV]
        n = n_ref[:, h * DK:(h + 1) * DK]
        ig = G_ref[:, c_ig + h:c_ig + h + 1]
        lf = G_ref[:, c_lf + h:c_lf + h + 1]
        m0 = m_ref[:, h:h + 1]
        inter = lf + m0
        m_new = jnp.maximum(inter, ig)
        w_inter = jnp.exp(inter - m_new)
        w_intra = jnp.exp(ig - m_new)
        s = jnp.sum(q * k, axis=1, keepdims=True) * w_intra
        qn = jnp.sum(q * n, axis=1, keepdims=True)
        vw = v * w_intra
        kc = kc_ref[h * DK:(h + 1) * DK, :]
        qC = jnp.zeros((NB, DV), F32)
        for bb in range(NB):
            C = C_ref[bb, h]
            qC = jnp.where(sub == bb, jnp.dot(q, C, preferred_element_type=F32), qC)
            vsel = jnp.concatenate([jnp.where(sub == bb, vw, 0.0), jnp.zeros((LANES - NB, DV), F32)], axis=0)
            Cn_ref[bb, h] = w_inter[bb:bb + 1, :] * C + jnp.dot(kc, vsel, preferred_element_type=F32)
        num = w_inter * qC + s * v
        den = w_inter * qn + s
        hval = num / jnp.maximum(jnp.abs(den), jnp.exp(-m_new))
        nn_ref[:, h * DK:(h + 1) * DK] = w_inter * n + w_intra * k
        m_out = jnp.where(lane == h, m_new, m_out)
        gate = _sigmoid(o_ref[:, h * DV:(h + 1) * DV])
        hm_ref[:, h * DV:(h + 1) * DV] = _rms(hval, gm_ref[:, h * DV:(h + 1) * DV]) * gate
    mn_ref[...] = m_out


def _mlstm_step(q, k, v, o, G, state_C, state_n, state_m, layer, g_m, *, NB, H, DK, DV, c_ig, c_lf, carried=()):
    DB = q.shape[0]
    depth = state_C.shape[0]
    NB = min(NB, DB)
    assert DB % NB == 0 and NB <= LANES
    S = DB // NB
    kc = jnp.pad(k.reshape(S, NB, H * DK).transpose(0, 2, 1), ((0, 0), (0, 0), (0, LANES - NB)))
    row = lambda s: (s, 0)
    args = [q, k, v, o, kc, G, state_C, state_n, state_m, g_m]
    in_specs = [pl.BlockSpec((NB, H * DK), row), pl.BlockSpec((NB, H * DK), row),
                pl.BlockSpec((NB, H * DV), row), pl.BlockSpec((NB, H * DV), row),
                pl.BlockSpec((None, H * DK, LANES), lambda s: (s, 0, 0)),
                pl.BlockSpec((NB, LANES), row),
                pl.BlockSpec((None, NB, H, DK, DV), lambda s: (layer, s, 0, 0, 0)),
                pl.BlockSpec((None, NB, H * DK), lambda s: (layer, s, 0)),
                pl.BlockSpec((None, NB, H), lambda s: (layer, s, 0)),
                pl.BlockSpec((1, H * DV), lambda s: (0, 0))]
    aliases = {}
    for buf in carried:
        aliases[len(args)] = 1
        args.append(buf)
        in_specs.append(pl.BlockSpec(memory_space=pl.ANY))
    kern = functools.partial(_mlstm_step_kernel, NB=NB, H=H, DK=DK, DV=DV, c_ig=c_ig, c_lf=c_lf)
    return pl.pallas_call(
        kern, grid=(S,), in_specs=in_specs,
        out_specs=[pl.BlockSpec((NB, H * DV), row),
                   pl.BlockSpec((None, NB, H, DK, DV), lambda s: (layer, s, 0, 0, 0)),
                   pl.BlockSpec((NB, H * DK), row), pl.BlockSpec((NB, LANES), row)],
        out_shape=[jax.ShapeDtypeStruct((DB, H * DV), F32), jax.ShapeDtypeStruct((depth, DB, H, DK, DV), F32),
                   jax.ShapeDtypeStruct((DB, H * DK), F32), jax.ShapeDtypeStruct((DB, LANES), F32)],
        input_output_aliases=aliases, compiler_params=_cparams(("parallel",)),
        name="mlstm_step")(*args)


def _fox_step_kernel(pt_ref, q_ref, kc_ref, vc_ref, G_ref, gf_ref, *refs, NP, HF, HD, PAGE):
    k_refs = refs[:NP]
    v_refs = refs[NP:2 * NP]
    lf_refs = refs[2 * NP:3 * NP]
    o_ref = refs[3 * NP]
    DF = HF * HD
    sub = lax.broadcasted_iota(jnp.int32, (HF, DF), 0)
    lane = lax.broadcasted_iota(jnp.int32, (HF, DF), 1)
    diag = lane // HD == sub
    q = q_ref[...]
    qbd = jnp.where(diag, jnp.broadcast_to(q, (HF, DF)), 0.0)
    qbd16 = qbd.astype(BF16)
    sub_g = lax.broadcasted_iota(jnp.int32, (HF, LANES), 0)
    lane_g = lax.broadcasted_iota(jnp.int32, (HF, LANES), 1)
    carry = jnp.sum(jnp.where(lane_g == sub_g, jnp.broadcast_to(G_ref[...], (HF, LANES)), 0.0),
                    axis=1, keepdims=True)
    ti = lax.broadcasted_iota(jnp.int32, (PAGE, PAGE), 0)
    tj = lax.broadcasted_iota(jnp.int32, (PAGE, PAGE), 1)
    later = jnp.where(ti > tj, 1.0, 0.0)
    s_cur = jnp.sum(qbd * kc_ref[...], axis=1, keepdims=True)
    scores = [None] * NP
    mx = s_cur
    for p in range(NP - 1, -1, -1):
        lfT = lf_refs[p][...]
        suf = jnp.dot(lfT, later, preferred_element_type=F32, precision=lax.Precision.HIGHEST) + carry
        carry = carry + jnp.sum(lfT, axis=1, keepdims=True)
        kT = k_refs[p][...].astype(BF16)
        s = jnp.dot(qbd16, kT, preferred_element_type=F32) + suf
        scores[p] = s
        mx = jnp.maximum(mx, jnp.max(s, axis=1, keepdims=True))
    p_cur = jnp.exp(s_cur - mx)
    l = p_cur
    acc = p_cur * vc_ref[...]
    for p in range(NP):
        pr = jnp.exp(scores[p] - mx)
        l = l + jnp.sum(pr, axis=1, keepdims=True)
        vT = v_refs[p][...].astype(BF16)
        acc = acc + lax.dot_general(pr.astype(BF16), vT, (((1,), (1,)), ((), ())), preferred_element_type=F32)
    o = acc / l
    ms = jnp.sum(jnp.where(diag, o * o, 0.0), axis=1, keepdims=True) * (1.0 / HD)
    on = jnp.where(diag, o * lax.rsqrt(ms + EPS), 0.0)
    o_ref[...] = jnp.sum(on, axis=0, keepdims=True) * gf_ref[...]


def _fox_step(q, k_cur, v_cur, lf_cur, cache_kT, cache_vT, cache_lfT, page_table, layer, g_f, *, HF, HD):
    DB = q.shape[0]
    NP = page_table.shape[1]
    PAGE = cache_kT.shape[3]
    DF = HF * HD
    pt = page_table.reshape(-1)
    r3 = lambda a: a.reshape(DB, 1, a.shape[-1])
    cur = lambda b, pt: (b, 0, 0)

    def page_map(p):
        return lambda b, pt: (layer, pt[b * NP + p], 0, 0)

    kv_specs = [pl.BlockSpec((None, None, DF, PAGE), page_map(p)) for p in range(NP)]
    lf_specs = [pl.BlockSpec((None, None, HF, PAGE), page_map(p)) for p in range(NP)]
    grid_spec = pltpu.PrefetchScalarGridSpec(
        num_scalar_prefetch=1, grid=(DB,),
        in_specs=[pl.BlockSpec((None, 1, DF), cur)] * 3 + [pl.BlockSpec((None, 1, LANES), cur),
                                                           pl.BlockSpec((1, DF), lambda b, pt: (0, 0))]
                 + kv_specs + kv_specs + lf_specs,
        out_specs=pl.BlockSpec((None, 1, DF), cur))
    kern = functools.partial(_fox_step_kernel, NP=NP, HF=HF, HD=HD, PAGE=PAGE)
    out = pl.pallas_call(
        kern, grid_spec=grid_spec, out_shape=jax.ShapeDtypeStruct((DB, 1, DF), F32),
        compiler_params=_cparams(("arbitrary",)),
        name="fox_step")(pt, r3(q), r3(k_cur), r3(v_cur), r3(lf_cur), g_f,
                         *([cache_kT] * NP), *([cache_vT] * NP), *([cache_lfT] * NP))
    return out.reshape(DB, DF)


def kernel(x_prompt, x_sample, state_C, state_n, state_m, cache_k, cache_v, cache_logf, page_table, g_mix_norm, w_in, b_ig, b_fg, b_ff, g_out_m, g_out_f, w_out, g_ffn_norm, w_ffn_gate, w_ffn_up, w_ffn_down, w_router, b_router, w_moe_gate, w_moe_up, w_moe_down, g_final):
    B, T, D = x_prompt.shape
    DB = x_sample.shape[0]
    assert x_sample.shape[1] == 1
    depth = w_in.shape[0]
    HM, DK, DV = state_C.shape[2:]
    PAGE, HF, HD = cache_k.shape[2:]
    n_pool = cache_k.shape[1]
    dq, dv, df = HM * DK, HM * DV, HF * HD
    assert dq % LANES == 0 and dv % LANES == 0 and df % LANES == 0 and LANES % HD == 0
    assert HF + 2 * HM <= GATE_ROWS
    dims = (dq, dv, df, HF, HM, DK, HD)
    c_ig, c_lf = HF, HF + HM
    E = w_router.shape[-1]
    chunk = w_moe_gate.shape[-1]
    M = B * T

    o = np.cumsum([0, dq, dq, dv, dv, HM, HM, df, df, df, HF]).tolist()
    w_m = w_in[:, :, o[0]:o[4]].astype(BF16)
    w_f = w_in[:, :, o[6]:o[9]].astype(BF16)
    scat = np.zeros((N_SPLIT, LANES, HF * LANES), np.float32)
    for t in range(N_SPLIT):
        scat[t, np.arange(HF), np.arange(HF) * LANES + HD + t] = 1.0
    scat = jnp.asarray(scat, BF16)
    n_gate = HF + 2 * HM
    w_gate = jnp.concatenate([w_in[:, :, o[9]:o[10]], w_in[:, :, o[4]:o[6]],
                              jnp.zeros((depth, D, LANES - n_gate), w_in.dtype)], axis=-1).astype(BF16)
    b_gate = jnp.concatenate([b_ff, b_ig, b_fg, jnp.zeros((depth, LANES - n_gate), F32)],
                             axis=-1).astype(F32).reshape(depth, 1, LANES)
    w_out16 = w_out.astype(BF16)
    n_dense, _, d_ff = w_ffn_gate.shape
    assert d_ff % chunk == 0
    nc = d_ff // chunk
    split = lambda w: w.astype(BF16).reshape(n_dense, D, nc, chunk)
    wfgu = jnp.concatenate([split(w_ffn_gate), split(w_ffn_up)], axis=-1).transpose(0, 2, 1, 3)
    wfd = w_ffn_down.astype(BF16).reshape(n_dense, nc, chunk, D)
    wmgu = jnp.concatenate([w_moe_gate.astype(BF16), w_moe_up.astype(BF16)], axis=-1)
    wmd = w_moe_down.astype(BF16)
    w_r = jnp.concatenate([w_router, jnp.zeros(w_router.shape[:2] + (LANES - E,), F32)], axis=-1)
    b_r = jnp.concatenate([b_router, jnp.zeros((b_router.shape[0], LANES - E), F32)],
                          axis=-1).reshape(-1, 1, LANES)
    ckT = jnp.transpose(cache_k, (0, 1, 3, 4, 2)).reshape(depth, n_pool, df, PAGE)
    cvT = jnp.transpose(cache_v, (0, 1, 3, 4, 2)).reshape(depth, n_pool, df, PAGE)
    clfT = jnp.swapaxes(cache_logf, 2, 3)
    sn = state_n.reshape(depth, DB, dq)

    xp = x_prompt.reshape(M, D)
    xs = x_sample.reshape(DB, D)
    pC, pn, pm, plf = [], [], [], []
    kv_acc = ()
    sn_o, sm, sk, sv, slf = [], [], [], [], []
    sC_acc = ()
    for l in range(depth):
        gmix = g_mix_norm[l].reshape(1, D)
        gm = g_out_m[l].reshape(1, dv)
        gf = g_out_f[l].reshape(1, df)
        w_om, w_of = w_out16[l, :dv], w_out16[l, dv:]
        (qm, vm, kT, om, qp, kp, vp, kf, vf, G, Gc, GT, GcT) = _inproj(
            xp, gmix, w_m[l], w_f[l], w_gate[l], b_gate[l], scat, dims, prompt=True, tm=512, seq_len=T,
            layer=l, depth=depth, carried=kv_acc)
        kv_acc = (kf, vf)
        hm, caug, mout = _mlstm_prompt(qm, vm, kT, om, G, Gc, GT, GcT, gm, B=B, T=T, H=HM, DK=DK, DV=DV,
                                       L=min(512, T), c_ig=c_ig, c_lf=c_lf)
        hf = _fox_prompt(qp, kp, vp, gf, B=B, T=T, HF=HF, HD=HD, TB=512, G=HF, RQ=256, KS=2)
        xp = _outproj(xp, hm, hf, w_om, w_of, tm=512)
        pC.append(caug[..., :DV]); pn.append(caug[..., DV]); pm.append(mout[:, :HM, 0])
        plf.append(G[:, :HF].reshape(B, T, HF))
        (qm, km, vm, om, qf, kf, vf, G) = _inproj(
            xs, gmix, w_m[l], w_f[l], w_gate[l], b_gate[l], None, dims, prompt=False, tm=DB, seq_len=1)
        hm, Cn, nn, mn = _mlstm_step(qm, km, vm, om, G, state_C, sn, state_m, l, gm,
                                     NB=8, H=HM, DK=DK, DV=DV, c_ig=c_ig, c_lf=c_lf, carried=sC_acc)
        sC_acc = (Cn,)
        hf = _fox_step(qf, kf, vf, G, ckT, cvT, clfT, page_table, l, gf, HF=HF, HD=HD)
        xs = _outproj(xs, hm, hf, w_om, w_of, tm=DB)
        sn_o.append(nn.reshape(DB, HM, DK)); sm.append(mn[:, :HM])
        sk.append(kf.reshape(DB, 1, HF, HD)); sv.append(vf.reshape(DB, 1, HF, HD))
        slf.append(G[:, :HF].reshape(DB, 1, HF))
        gffn = g_ffn_norm[l].reshape(1, D)
        gfin = g_final.reshape(1, D) if l == depth - 1 else None
        j = l // 2
        if l % 2 == 0:
            args = (wfgu[j], wfd[j], None)
        else:
            args = (wmgu[j], wmd[j], (w_r[j], b_r[j]))
        xp = _ffn(xp, gffn, *args, gfin, tm=512 if l % 2 == 0 else 1024)
        xs = _ffn(xs, gffn, *args, gfin, tm=DB)

    tok_major = lambda a: a.reshape(depth, B, HF, HD, T).transpose(0, 1, 4, 2, 3)
    return (xp.reshape(B, T, D), xs.reshape(DB, 1, D),
            jnp.stack(pC), jnp.stack(pn), jnp.stack(pm),
            tok_major(kv_acc[0]), tok_major(kv_acc[1]), jnp.stack(plf),
            sC_acc[0], jnp.stack(sn_o), jnp.stack(sm),
            jnp.stack(sk), jnp.stack(sv), jnp.stack(slf))
```

```python
import functools

import numpy as np
import jax
import jax.numpy as jnp
from jax import lax
from jax.experimental import pallas as pl
from jax.experimental.pallas import tpu as pltpu

F32 = jnp.float32
BF16 = jnp.bfloat16
EPS = 1e-6
NEG_INIT = -1e30
TOP_K = 2
LANES = 128
GATE_ROWS = 16
VMEM_LIMIT = 56 * 1024 * 1024
LOG2E = 1.4426950408889634
N_SPLIT = 3
FULL_TILE_ROWS = 512


def _cparams(sem):
    return pltpu.CompilerParams(dimension_semantics=sem, vmem_limit_bytes=VMEM_LIMIT)


def _rms(x, g):
    var = jnp.mean(x * x, axis=-1, keepdims=True)
    return x * lax.rsqrt(var + EPS) * g


def _log_sigmoid(x):
    return jnp.minimum(x, 0.0) - jnp.log(1.0 + jnp.exp(-jnp.abs(x)))


def _sigmoid(x):
    return 1.0 / (1.0 + jnp.exp(-x))


def _cumsum_rows(a):
    n = a.shape[0]
    row = lax.broadcasted_iota(jnp.int32, a.shape, 0)
    k = 1
    while k < n:
        a = a + jnp.where(row >= k, pltpu.roll(a, k, 0), 0.0)
        k *= 2
    return a


def _spread_heads(z, hd, tail):
    lane = lax.broadcasted_iota(jnp.int32, (1, LANES), 1)
    blocks = []
    for p in range(z.shape[1] // LANES):
        a = z[:, p * LANES:(p + 1) * LANES]
        blocks.append(jnp.where(lane < hd, a, tail(2 * p)))
        blocks.append(jnp.where(lane < hd, pltpu.roll(a, hd, 1), tail(2 * p + 1)))
    return jnp.concatenate(blocks, axis=1)


def _inproj_kernel(x_ref, g_ref, wm_ref, wf_ref, wg_ref, bg_ref, *refs, dq, dv, df, hf, hm, hd, kscale, qscale,
                   prompt, tiles_per_seq, n_carried):
    if prompt:
        (scat_ref, qm_ref, vm_ref, kT_ref, om_ref, qp_ref, kp_ref, vp_ref, kf_ref, vf_ref,
         G_ref, Gc_ref, GT_ref, GcT_ref, carry_ref) = refs[:1] + refs[1 + n_carried:]
    else:
        (qm_ref, km_ref, vm_ref, om_ref, qf_ref, kf_ref, vf_ref, G_ref) = refs
    hn = _rms(x_ref[...], g_ref[...]).astype(BF16)

    def proj(w_ref, off, width):
        return jnp.dot(hn, w_ref[:, off:off + width], preferred_element_type=F32)

    zg = jnp.dot(hn, wg_ref[...], preferred_element_type=F32) + bg_ref[...]
    col = lax.broadcasted_iota(jnp.int32, zg.shape, 1)
    is_log = (col < hf) | ((col >= hf + hm) & (col < hf + 2 * hm))
    is_lin = (col >= hf) & (col < hf + hm)
    G = jnp.where(is_log, _log_sigmoid(zg), jnp.where(is_lin, zg, 0.0))
    G_ref[...] = G

    qm_ref[...] = proj(wm_ref, 0, dq).astype(qm_ref.dtype)
    km = proj(wm_ref, dq, dq) * kscale
    if prompt:
        kT_ref[...] = jnp.transpose(km).astype(kT_ref.dtype)
    else:
        km_ref[...] = km
    vm_ref[...] = proj(wm_ref, 2 * dq, dv).astype(vm_ref.dtype)
    om_ref[...] = proj(wm_ref, 2 * dq + dv, dv).astype(om_ref.dtype)
    if not prompt:
        qf_ref[...] = proj(wf_ref, 0, df) * qscale
        kf_ref[...] = proj(wf_ref, df, df)
        vf_ref[...] = proj(wf_ref, 2 * df, df)
        return

    i = pl.program_id(0)

    @pl.when(i % tiles_per_seq == 0)
    def _():
        carry_ref[...] = jnp.zeros_like(carry_ref)

    Gc = _cumsum_rows(G) + carry_ref[0:1, :]
    carry_ref[0:1, :] = Gc[-1:, :]
    Gc_ref[...] = Gc
    GT_ref[...] = jnp.transpose(G)[:GATE_ROWS, :]
    GcT_ref[...] = jnp.transpose(Gc)[:GATE_ROWS, :]

    lane = lax.broadcasted_iota(jnp.int32, (1, LANES), 1)
    zq = proj(wf_ref, 0, df) * (qscale * LOG2E)
    zk = proj(wf_ref, df, df)
    zv = proj(wf_ref, 2 * df, df)
    kf_ref[...] = jnp.transpose(zk)
    vf_ref[...] = jnp.transpose(zv)
    rem = Gc * (-LOG2E)
    bias = jnp.zeros((zk.shape[0], hf * LANES), F32)
    for t in range(N_SPLIT):
        piece = rem.astype(BF16)
        rem = rem - piece.astype(F32)
        bias = bias + jnp.dot(piece, scat_ref[t], preferred_element_type=F32)
    q_tail = jnp.where((lane >= hd) & (lane < hd + N_SPLIT), 1.0, 0.0)
    v_tail = jnp.where(lane == hd, 1.0, 0.0)
    qp_ref[...] = _spread_heads(zq, hd, lambda h: q_tail).astype(BF16)
    kp_ref[...] = _spread_heads(zk, hd, lambda h: bias[:, h * LANES:(h + 1) * LANES]).astype(BF16)
    vp_ref[...] = _spread_heads(zv, hd, lambda h: v_tail).astype(BF16)


def _inproj(x, g, w_m, w_f, w_gate, b_gate, scat, dims, *, prompt, tm, seq_len, layer=0, depth=1, carried=()):
    M, D = x.shape
    dq, dv, df, hf, hm, dk, hd = dims
    dfp = hf * LANES
    tm = min(tm, M)
    assert M % tm == 0
    if prompt:
        assert seq_len % tm == 0
    row = lambda i: (i, 0)
    colb = lambda i: (0, i)
    const = lambda i: (0, 0)
    tps = max(seq_len // tm, 1)
    seqb = lambda i: (layer, i // tps, 0, i % tps)
    aliases = {}
    args = [x, g, w_m, w_f, w_gate, b_gate]
    in_specs = [pl.BlockSpec((tm, D), row), pl.BlockSpec((1, D), const),
                pl.BlockSpec(w_m.shape, const), pl.BlockSpec(w_f.shape, const),
                pl.BlockSpec((D, LANES), const), pl.BlockSpec((1, LANES), const)]
    if prompt:
        args.append(scat)
        in_specs.append(pl.BlockSpec(scat.shape, lambda i: (0, 0, 0)))
        kv_out = 7
        for n, buf in enumerate(carried):
            aliases[len(args)] = kv_out + n
            args.append(buf)
            in_specs.append(pl.BlockSpec(memory_space=pl.ANY))
        out_shape = [jax.ShapeDtypeStruct((M, dq), BF16), jax.ShapeDtypeStruct((M, dv), BF16),
                     jax.ShapeDtypeStruct((dq, M), BF16), jax.ShapeDtypeStruct((M, dv), BF16),
                     jax.ShapeDtypeStruct((M, dfp), BF16), jax.ShapeDtypeStruct((M, dfp), BF16),
                     jax.ShapeDtypeStruct((M, dfp), BF16),
                     jax.ShapeDtypeStruct((depth, M // seq_len, df, seq_len), F32),
                     jax.ShapeDtypeStruct((depth, M // seq_len, df, seq_len), F32),
                     jax.ShapeDtypeStruct((M, LANES), F32), jax.ShapeDtypeStruct((M, LANES), F32),
                     jax.ShapeDtypeStruct((GATE_ROWS, M), F32), jax.ShapeDtypeStruct((GATE_ROWS, M), F32)]
        out_specs = [pl.BlockSpec((tm, dq), row), pl.BlockSpec((tm, dv), row),
                     pl.BlockSpec((dq, tm), colb), pl.BlockSpec((tm, dv), row),
                     pl.BlockSpec((tm, dfp), row), pl.BlockSpec((tm, dfp), row), pl.BlockSpec((tm, dfp), row),
                     pl.BlockSpec((None, None, df, tm), seqb), pl.BlockSpec((None, None, df, tm), seqb),
                     pl.BlockSpec((tm, LANES), row), pl.BlockSpec((tm, LANES), row),
                     pl.BlockSpec((GATE_ROWS, tm), colb), pl.BlockSpec((GATE_ROWS, tm), colb)]
        scratch = [pltpu.VMEM((8, LANES), F32)]
    else:
        out_shape = ([jax.ShapeDtypeStruct((M, dq), F32)] * 2 + [jax.ShapeDtypeStruct((M, dv), F32)] * 2
                     + [jax.ShapeDtypeStruct((M, df), F32)] * 3 + [jax.ShapeDtypeStruct((M, LANES), F32)])
        out_specs = ([pl.BlockSpec((tm, dq), row)] * 2 + [pl.BlockSpec((tm, dv), row)] * 2
                     + [pl.BlockSpec((tm, df), row)] * 3 + [pl.BlockSpec((tm, LANES), row)])
        scratch = []
    kern = functools.partial(_inproj_kernel, dq=dq, dv=dv, df=df, hf=hf, hm=hm, hd=hd,
                             kscale=float(dk) ** -0.5, qscale=float(hd) ** -0.5,
                             prompt=prompt, tiles_per_seq=max(seq_len // tm, 1), n_carried=len(carried))
    return pl.pallas_call(
        kern, grid=(M // tm,), in_specs=in_specs, out_specs=out_specs, out_shape=out_shape,
        input_output_aliases=aliases, scratch_shapes=scratch, compiler_params=_cparams(("arbitrary",)),
        name="inproj_prompt" if prompt else "inproj_sample")(*args)


def _mlstm_prompt_kernel(q_ref, v_ref, kT_ref, o_ref, G_ref, Gc_ref, GT_ref, GcT_ref, gm_ref,
                         hm_ref, caug_ref, mout_ref, bprev_ref, m_ref, *, H, DK, DV, L, c_ig, c_lf):
    c = pl.program_id(1)

    @pl.when(c == 0)
    def _():
        caug_ref[...] = jnp.zeros_like(caug_ref)
        m_ref[...] = jnp.full_like(m_ref, NEG_INIT)
        bprev_ref[...] = jnp.zeros_like(bprev_ref)

    row = lax.broadcasted_iota(jnp.int32, (L, L), 0)
    colL = lax.broadcasted_iota(jnp.int32, (L, L), 1)
    causal = row >= colL
    lane = lax.broadcasted_iota(jnp.int32, (L, LANES), 1)
    ones_col = jnp.where(lane == 0, 1.0, 0.0)
    G = G_ref[...]
    Gc = Gc_ref[...]
    GT = GT_ref[...]
    GcT = GcT_ref[...]
    for h in range(H):
        bp = bprev_ref[0:1, c_lf + h:c_lf + h + 1]
        b_col = Gc[:, c_lf + h:c_lf + h + 1] - bp
        b_row = GcT[c_lf + h:c_lf + h + 1, :] - bp
        i_row = GT[c_ig + h:c_ig + h + 1, :]
        i_col = G[:, c_ig + h:c_ig + h + 1]
        m_prev = m_ref[h:h + 1, 0:1]
        dm = jnp.where(causal, b_col - b_row + i_row, -jnp.inf)
        inter = b_col + m_prev
        m_row = jnp.maximum(inter, jnp.max(dm, axis=-1, keepdims=True))
        w_intra = jnp.exp(dm - m_row)
        w_inter = jnp.exp(inter - m_row)
        qh = q_ref[:, h * DK:(h + 1) * DK]
        kTh = kT_ref[h * DK:(h + 1) * DK, :]
        vh = v_ref[:, h * DV:(h + 1) * DV].astype(F32)
        vaug = jnp.concatenate([vh, ones_col], axis=1)
        s = jnp.dot(qh, kTh, preferred_element_type=F32) * w_intra
        intra = jnp.dot(s.astype(BF16), vaug.astype(BF16), preferred_element_type=F32)
        cst = caug_ref[h]
        inter_qc = jnp.dot(qh, cst.astype(BF16), preferred_element_type=F32)
        r = w_inter * inter_qc + intra
        num = r[:, :DV]
        den = r[:, DV:DV + 1]
        hval = num / jnp.maximum(jnp.abs(den), jnp.exp(-m_row))
        m_new = m_row[L - 1:L, :]
        decay = jnp.exp(inter[L - 1:L, :] - m_new)
        w_k = jnp.exp(b_col[L - 1:L, :] - b_col + i_col - m_new)
        upd = jnp.dot(kTh, (vaug * w_k).astype(BF16), preferred_element_type=F32)
        caug_ref[h] = decay * cst + upd
        m_ref[h:h + 1, :] = jnp.broadcast_to(m_new, (1, LANES))
        gate = _sigmoid(o_ref[:, h * DV:(h + 1) * DV].astype(F32))
        hm_ref[:, h * DV:(h + 1) * DV] = (_rms(hval, gm_ref[:, h * DV:(h + 1) * DV]) * gate).astype(hm_ref.dtype)
    bprev_ref[0:1, :] = Gc[L - 1:L, :]

    @pl.when(c == pl.num_programs(1) - 1)
    def _():
        mout_ref[...] = m_ref[...]


def _mlstm_prompt(qm, vm, kT, om, G, Gc, GT, GcT, g_m, *, B, T, H, DK, DV, L, c_ig, c_lf):
    M = B * T
    NC = T // L
    assert T % L == 0
    rowb = lambda b, c: (b * NC + c, 0)
    colb = lambda b, c: (0, b * NC + c)
    kern = functools.partial(_mlstm_prompt_kernel, H=H, DK=DK, DV=DV, L=L, c_ig=c_ig, c_lf=c_lf)
    return pl.pallas_call(
        kern, grid=(B, NC),
        in_specs=[pl.BlockSpec((L, H * DK), rowb), pl.BlockSpec((L, H * DV), rowb),
                  pl.BlockSpec((H * DK, L), colb), pl.BlockSpec((L, H * DV), rowb),
                  pl.BlockSpec((L, LANES), rowb), pl.BlockSpec((L, LANES), rowb),
                  pl.BlockSpec((GATE_ROWS, L), colb), pl.BlockSpec((GATE_ROWS, L), colb),
                  pl.BlockSpec((1, H * DV), lambda b, c: (0, 0))],
        out_specs=[pl.BlockSpec((L, H * DV), rowb),
                   pl.BlockSpec((None, H, DK, DV + LANES), lambda b, c: (b, 0, 0, 0)),
                   pl.BlockSpec((None, 8, LANES), lambda b, c: (b, 0, 0))],
        out_shape=[jax.ShapeDtypeStruct((M, H * DV), BF16),
                   jax.ShapeDtypeStruct((B, H, DK, DV + LANES), F32),
                   jax.ShapeDtypeStruct((B, 8, LANES), F32)],
        scratch_shapes=[pltpu.VMEM((8, LANES), F32), pltpu.VMEM((8, LANES), F32)],
        compiler_params=_cparams(("parallel", "arbitrary")),
        name="mlstm_prompt")(qm, vm, kT, om, G, Gc, GT, GcT, g_m)


def _fox_prompt_kernel(ii_ref, jj_ref, q_ref, k_ref, v_ref, gf_ref, o_ref, m_sc, acc_sc, *, TB, HD, G, RQ, KS):
    p = pl.program_id(2)
    i = ii_ref[p]
    j = jj_ref[p]
    FULL, MASKED = 0, 1

    @pl.when(j == 0)
    def _():
        m_sc[...] = jnp.full_like(m_sc, -jnp.inf)
        acc_sc[...] = jnp.zeros_like(acc_sc)

    def update(modes):
        items = [(c, g, r) for c in range(len(modes)) for g in range(G) for r in range(TB // RQ)]

        def qk(c, g, r):
            q = q_ref[r * RQ:(r + 1) * RQ, g * LANES:(g + 1) * LANES]
            k = k_ref[c * TB:(c + 1) * TB, g * LANES:(g + 1) * LANES]
            return lax.dot_general(q, k, (((1,), (1,)), ((), ())), preferred_element_type=F32)

        s_next = qk(*items[0])
        for n, (c, g, r) in enumerate(items):
            s = s_next
            if n + 1 < len(items):
                s_next = qk(*items[n + 1])
            rows = slice(r * RQ, (r + 1) * RQ)
            v = v_ref[c * TB:(c + 1) * TB, g * LANES:(g + 1) * LANES]
            if modes[c] == MASKED:
                rr = lax.broadcasted_iota(jnp.int32, (RQ, TB), 0) + r * RQ
                cc = lax.broadcasted_iota(jnp.int32, (RQ, TB), 1)
                s = jnp.where(rr >= cc, s, -jnp.inf)
            m_old = m_sc[g, rows, :]
            m_new = jnp.maximum(m_old, jnp.max(s, axis=1, keepdims=True))
            alpha = jnp.exp2(m_old - m_new)
            pr = jnp.concatenate([jnp.exp2(s[:, cl * LANES:(cl + 1) * LANES] - m_new).astype(BF16)
                                  for cl in range(TB // LANES)], axis=1)
            acc_sc[g, rows, :] = alpha * acc_sc[g, rows, :] + jnp.dot(pr, v, preferred_element_type=F32)
            m_sc[g, rows, :] = m_new

    last = i // KS

    @pl.when(j < last)
    def _():
        update([FULL] * KS)

    for d in range(KS):
        @pl.when((j == last) & (i % KS == d))
        def _():
            update([FULL] * d + [MASKED])

    @pl.when(j == last)
    def _():
        lane = lax.broadcasted_iota(jnp.int32, (1, LANES), 1)
        lrow = lax.broadcasted_iota(jnp.int32, (LANES, LANES), 0)
        wmat = jnp.where(lrow < HD, 1.0 / HD, jnp.where(lrow == HD, EPS, 0.0))
        for gp in range(G // 2):
            outs = []
            for g in (2 * gp, 2 * gp + 1):
                a = acc_sc[g]
                outs.append(a * lax.rsqrt(jnp.dot(a * a, wmat, preferred_element_type=F32)))
            out = jnp.where(lane < HD, outs[0], pltpu.roll(outs[1], HD, 1))
            o_ref[:, gp * LANES:(gp + 1) * LANES] = (out * gf_ref[:, gp * LANES:(gp + 1) * LANES]).astype(o_ref.dtype)


def _fox_prompt(qp, kp, vp, g_f, *, B, T, HF, HD, TB, G, RQ, KS):
    M = B * T
    TB = min(TB, T)
    RQ = min(RQ, TB)
    KS = min(KS, T // TB)
    TK = KS * TB
    assert T % TK == 0 and TB % RQ == 0 and HF % G == 0 and G % 2 == 0 and 2 * HD == LANES
    nq = T // TB
    nkv = T // TK
    ii = np.concatenate([np.full(i // KS + 1, i, np.int32) for i in range(nq)])
    jj = np.concatenate([np.arange(i // KS + 1, dtype=np.int32) for i in range(nq)])
    qmap = lambda b, hg, p, ii, jj: (b * nq + ii[p], hg)
    kmap = lambda b, hg, p, ii, jj: (b * nkv + jj[p], hg)
    kern = functools.partial(_fox_prompt_kernel, TB=TB, HD=HD, G=G, RQ=RQ, KS=KS)
    grid_spec = pltpu.PrefetchScalarGridSpec(
        num_scalar_prefetch=2, grid=(B, HF // G, len(ii)),
        in_specs=[pl.BlockSpec((TB, G * LANES), qmap), pl.BlockSpec((TK, G * LANES), kmap),
                  pl.BlockSpec((TK, G * LANES), kmap),
                  pl.BlockSpec((1, G * HD), lambda b, hg, p, ii, jj: (0, hg))],
        out_specs=pl.BlockSpec((TB, G * HD), qmap),
        scratch_shapes=[pltpu.VMEM((G, TB, LANES), F32)] * 2)
    return pl.pallas_call(
        kern, grid_spec=grid_spec, out_shape=jax.ShapeDtypeStruct((M, HF * HD), BF16),
        compiler_params=_cparams(("parallel", "parallel", "arbitrary")),
        name="fox_prompt")(jnp.asarray(ii), jnp.asarray(jj), qp, kp, vp, g_f)


def _outproj_kernel(x_ref, hm_ref, hf_ref, wm_ref, wf_ref, o_ref):
    y = jnp.dot(hm_ref[...].astype(BF16), wm_ref[...], preferred_element_type=F32)
    y = y + jnp.dot(hf_ref[...].astype(BF16), wf_ref[...], preferred_element_type=F32)
    o_ref[...] = x_ref[...] + y


def _outproj(x, hm, hf, w_m, w_f, *, tm):
    M, D = x.shape
    tm = min(tm, M)
    assert M % tm == 0
    row = lambda i: (i, 0)
    const = lambda i: (0, 0)
    return pl.pallas_call(
        _outproj_kernel, grid=(M // tm,),
        in_specs=[pl.BlockSpec((tm, D), row), pl.BlockSpec((tm, hm.shape[1]), row),
                  pl.BlockSpec((tm, hf.shape[1]), row),
                  pl.BlockSpec(w_m.shape, const), pl.BlockSpec(w_f.shape, const)],
        out_specs=pl.BlockSpec((tm, D), row), out_shape=jax.ShapeDtypeStruct((M, D), F32),
        compiler_params=_cparams(("parallel",)), name="outproj")(x, hm, hf, w_m, w_f)


def _ffn_kernel(x_ref, g_ref, wgu_ref, wd_ref, *refs, moe, n_experts, final, cap):
    refs = list(refs)
    if moe:
        wr_ref, br_ref = refs[:2]
        refs = refs[2:]
    if final:
        gfin_ref = refs[0]
        refs = refs[1:]
    if moe:
        o_ref, hn_sc, acc_sc, comb_sc, rk_sc, rkT_sc, cnt_sc = refs
    else:
        o_ref, hn_sc, acc_sc = refs
    e = pl.program_id(1)
    tm = hn_sc.shape[0]
    chunk = wd_ref.shape[0]

    def swiglu(h):
        z = jnp.dot(h, wgu_ref[...], preferred_element_type=F32)
        a = z[:, :chunk]
        act = (a * _sigmoid(a) * z[:, chunk:]).astype(BF16)
        return jnp.dot(act, wd_ref[...], preferred_element_type=F32)

    @pl.when(e == 0)
    def _():
        hn = _rms(x_ref[...], g_ref[...])
        hn_sc[...] = hn.astype(BF16)
        acc_sc[...] = jnp.zeros_like(acc_sc)
        if moe:
            lane = lax.broadcasted_iota(jnp.int32, (tm, LANES), 1)
            lanef = lane.astype(F32)
            hn_hi = hn.astype(BF16)
            hn_lo = (hn - hn_hi.astype(F32)).astype(BF16)
            wr = wr_ref[...]
            wr_hi = wr.astype(BF16)
            wr_lo = (wr - wr_hi.astype(F32)).astype(BF16)
            lg = (jnp.dot(hn_hi, wr_hi, preferred_element_type=F32) + jnp.dot(hn_hi, wr_lo, preferred_element_type=F32)
                  + jnp.dot(hn_lo, wr_hi, preferred_element_type=F32)) + br_ref[...]
            lg = jnp.where(lane < n_experts, lg, -jnp.inf)
            m1 = jnp.max(lg, axis=1, keepdims=True)
            i1 = jnp.min(jnp.where(lg == m1, lanef, float(LANES)), axis=1, keepdims=True)
            lg2 = jnp.where(lanef == i1, -jnp.inf, lg)
            m2 = jnp.max(lg2, axis=1, keepdims=True)
            i2 = jnp.min(jnp.where(lg2 == m2, lanef, float(LANES)), axis=1, keepdims=True)
            e2 = jnp.exp(m2 - m1)
            g1 = 1.0 / (1.0 + e2)
            g2 = e2 / (1.0 + e2)
            comb_sc[...] = jnp.where(lanef == i1, g1, 0.0) + jnp.where(lanef == i2, g2, 0.0)
            routed = jnp.where((lanef == i1) | (lanef == i2), 1.0, 0.0)
            incl = _cumsum_rows(routed)
            rk = jnp.where(routed > 0.0, incl - 1.0, -1.0)
            rk_sc[...] = rk
            rkT_sc[...] = jnp.transpose(rk)
            cnt_sc[...] = jnp.broadcast_to(incl[tm - 1:tm, :], cnt_sc.shape)

    if not moe:
        acc_sc[...] += swiglu(hn_sc[...])
    else:
        lane = lax.broadcasted_iota(jnp.int32, (tm, LANES), 1)
        ce = jnp.sum(jnp.where(lane == e, comb_sc[...], 0.0), axis=1, keepdims=True)
        lane_row = lax.broadcasted_iota(jnp.int32, (1, LANES), 1)
        count = jnp.sum(jnp.where(lane_row == e, cnt_sc[0:1, :], 0.0))

        @pl.when(count > cap)
        def _():
            rb = min(tm, FULL_TILE_ROWS)
            for r in range(tm // rb):
                rows = slice(r * rb, (r + 1) * rb)
                acc_sc[rows, :] += ce[rows, :] * swiglu(hn_sc[rows, :])

        @pl.when(count <= cap)
        def _():
            rk_row = rkT_sc[pl.ds(e, 1), :]
            slot = lax.broadcasted_iota(jnp.int32, (cap, tm), 0).astype(F32)
            pick = jnp.where(rk_row == slot, 1.0, 0.0).astype(BF16)
            xs = jnp.dot(pick, hn_sc[...], preferred_element_type=F32).astype(BF16)
            y = swiglu(xs).astype(BF16)
            rk_col = jnp.sum(jnp.where(lane == e, rk_sc[...], 0.0), axis=1, keepdims=True)
            slot_l = lax.broadcasted_iota(jnp.int32, (tm, cap), 1).astype(F32)
            place = jnp.where(rk_col == slot_l, 1.0, 0.0).astype(BF16)
            acc_sc[...] += ce * jnp.dot(place, y, preferred_element_type=F32)

    @pl.when(e == pl.num_programs(1) - 1)
    def _():
        xo = x_ref[...] + acc_sc[...]
        if final:
            xo = _rms(xo, gfin_ref[...])
        o_ref[...] = xo


def _expert_capacity(tm, n_experts):
    mean = tm * TOP_K / n_experts
    return min(tm, max(64, int(-(-1.5 * mean // 64)) * 64))


def _ffn(x, g, wgu, wd, router, g_final, *, tm):
    M, D = x.shape
    tm = min(tm, M)
    assert M % tm == 0
    moe = router is not None
    final = g_final is not None
    E, chunk = wd.shape[0], wd.shape[1]
    row = lambda i, e: (i, 0)
    const = lambda i, e: (0, 0)
    args = [x, g, wgu, wd]
    in_specs = [pl.BlockSpec((tm, D), row), pl.BlockSpec((1, D), const),
                pl.BlockSpec((None, D, 2 * chunk), lambda i, e: (e, 0, 0)),
                pl.BlockSpec((None, chunk, D), lambda i, e: (e, 0, 0))]
    scratch = [pltpu.VMEM((tm, D), BF16), pltpu.VMEM((tm, D), F32)]
    if moe:
        args += list(router)
        in_specs += [pl.BlockSpec((D, LANES), const), pl.BlockSpec((1, LANES), const)]
        scratch += [pltpu.VMEM((tm, LANES), F32), pltpu.VMEM((tm, LANES), F32), pltpu.VMEM((LANES, tm), F32),
                    pltpu.VMEM((8, LANES), F32)]
    if final:
        args.append(g_final)
        in_specs.append(pl.BlockSpec((1, D), const))
    kern = functools.partial(_ffn_kernel, moe=moe, n_experts=E, final=final, cap=_expert_capacity(tm, E))
    return pl.pallas_call(
        kern, grid=(M // tm, E), in_specs=in_specs,
        out_specs=pl.BlockSpec((tm, D), row), out_shape=jax.ShapeDtypeStruct((M, D), F32),
        scratch_shapes=scratch, compiler_params=_cparams(("parallel", "arbitrary")),
        name="ffn_moe" if moe else "ffn_dense")(*args)


def _mlstm_step_kernel(q_ref, k_ref, v_ref, o_ref, kc_ref, G_ref, C_ref, n_ref, m_ref, gm_ref, *refs,
                       NB, H, DK, DV, c_ig, c_lf):
    hm_ref, Cn_ref, nn_ref, mn_ref = refs[-4:]
    sub = lax.broadcasted_iota(jnp.int32, (NB, DV), 0)
    lane = lax.broadcasted_iota(jnp.int32, (NB, LANES), 1)
    m_out = jnp.zeros((NB, LANES), F32)
    for h in range(H):
        q = q_ref[:, h * DK:(h + 1) * DK]
        k = k_ref[:, h * DK:(h + 1) * DK]
        v = v_ref[:, h * DV:(h + 1) * DV]
        n = n_ref[:, h * DK:(h + 1) * DK]
        ig = G_ref[:, c_ig + h:c_ig + h + 1]
        lf = G_ref[:, c_lf + h:c_lf + h + 1]
        m0 = m_ref[:, h:h + 1]
        inter = lf + m0
        m_new = jnp.maximum(inter, ig)
        w_inter = jnp.exp(inter - m_new)
        w_intra = jnp.exp(ig - m_new)
        s = jnp.sum(q * k, axis=1, keepdims=True) * w_intra
        qn = jnp.sum(q * n, axis=1, keepdims=True)
        vw = v * w_intra
        kc = kc_ref[h * DK:(h + 1) * DK, :]
        qC = jnp.zeros((NB, DV), F32)
        for bb in range(NB):
            C = C_ref[bb, h]
            qC = jnp.where(sub == bb, jnp.dot(q, C, preferred_element_type=F32), qC)
            vsel = jnp.concatenate([jnp.where(sub == bb, vw, 0.0), jnp.zeros((LANES - NB, DV), F32)], axis=0)
            Cn_ref[bb, h] = w_inter[bb:bb + 1, :] * C + jnp.dot(kc, vsel, preferred_element_type=F32)
        num = w_inter * qC + s * v
        den = w_inter * qn + s
        hval = num / jnp.maximum(jnp.abs(den), jnp.exp(-m_new))
        nn_ref[:, h * DK:(h + 1) * DK] = w_inter * n + w_intra * k
        m_out = jnp.where(lane == h, m_new, m_out)
        gate = _sigmoid(o_ref[:, h * DV:(h + 1) * DV])
        hm_ref[:, h * DV:(h + 1) * DV] = _rms(hval, gm_ref[:, h * DV:(h + 1) * DV]) * gate
    mn_ref[...] = m_out


def _mlstm_step(q, k, v, o, G, state_C, state_n, state_m, layer, g_m, *, NB, H, DK, DV, c_ig, c_lf, carried=()):
    DB = q.shape[0]
    depth = state_C.shape[0]
    NB = min(NB, DB)
    assert DB % NB == 0 and NB <= LANES
    S = DB // NB
    kc = jnp.pad(k.reshape(S, NB, H * DK).transpose(0, 2, 1), ((0, 0), (0, 0), (0, LANES - NB)))
    row = lambda s: (s, 0)
    args = [q, k, v, o, kc, G, state_C, state_n, state_m, g_m]
    in_specs = [pl.BlockSpec((NB, H * DK), row), pl.BlockSpec((NB, H * DK), row),
                pl.BlockSpec((NB, H * DV), row), pl.BlockSpec((NB, H * DV), row),
                pl.BlockSpec((None, H * DK, LANES), lambda s: (s, 0, 0)),
                pl.BlockSpec((NB, LANES), row),
                pl.BlockSpec((None, NB, H, DK, DV), lambda s: (layer, s, 0, 0, 0)),
                pl.BlockSpec((None, NB, H * DK), lambda s: (layer, s, 0)),
                pl.BlockSpec((None, NB, H), lambda s: (layer, s, 0)),
                pl.BlockSpec((1, H * DV), lambda s: (0, 0))]
    aliases = {}
    for buf in carried:
        aliases[len(args)] = 1
        args.append(buf)
        in_specs.append(pl.BlockSpec(memory_space=pl.ANY))
    kern = functools.partial(_mlstm_step_kernel, NB=NB, H=H, DK=DK, DV=DV, c_ig=c_ig, c_lf=c_lf)
    return pl.pallas_call(
        kern, grid=(S,), in_specs=in_specs,
        out_specs=[pl.BlockSpec((NB, H * DV), row),
                   pl.BlockSpec((None, NB, H, DK, DV), lambda s: (layer, s, 0, 0, 0)),
                   pl.BlockSpec((NB, H * DK), row), pl.BlockSpec((NB, LANES), row)],
        out_shape=[jax.ShapeDtypeStruct((DB, H * DV), F32), jax.ShapeDtypeStruct((depth, DB, H, DK, DV), F32),
                   jax.ShapeDtypeStruct((DB, H * DK), F32), jax.ShapeDtypeStruct((DB, LANES), F32)],
        input_output_aliases=aliases, compiler_params=_cparams(("parallel",)),
        name="mlstm_step")(*args)


def _fox_step_kernel(pt_ref, q_ref, kc_ref, vc_ref, G_ref, gf_ref, *refs, NP, HF, HD, PAGE):
    k_refs = refs[:NP]
    v_refs = refs[NP:2 * NP]
    lf_ref = refs[2 * NP]
    o_ref = refs[2 * NP + 1]
    b = pl.program_id(0)
    DF = HF * HD
    sub = lax.broadcasted_iota(jnp.int32, (HF, DF), 0)
    lane = lax.broadcasted_iota(jnp.int32, (HF, DF), 1)
    diag = lane // HD == sub
    q = q_ref[...]
    qbd = jnp.where(diag, jnp.broadcast_to(q, (HF, DF)), 0.0)
    qbd16 = qbd.astype(BF16)
    sub_g = lax.broadcasted_iota(jnp.int32, (HF, LANES), 0)
    lane_g = lax.broadcasted_iota(jnp.int32, (HF, LANES), 1)
    carry = jnp.sum(jnp.where(lane_g == sub_g, jnp.broadcast_to(G_ref[...], (HF, LANES)), 0.0),
                    axis=1, keepdims=True)
    ti = lax.broadcasted_iota(jnp.int32, (PAGE, PAGE), 0)
    tj = lax.broadcasted_iota(jnp.int32, (PAGE, PAGE), 1)
    later = jnp.where(ti > tj, 1.0, 0.0)
    s_cur = jnp.sum(qbd * kc_ref[...], axis=1, keepdims=True)
    scores = [None] * NP
    mx = s_cur
    for p in range(NP - 1, -1, -1):
        lfT = lf_ref[pt_ref[b * NP + p]]
        suf = jnp.dot(lfT, later, preferred_element_type=F32, precision=lax.Precision.HIGHEST) + carry
        carry = carry + jnp.sum(lfT, axis=1, keepdims=True)
        kT = k_refs[p][...].astype(BF16)
        s = jnp.dot(qbd16, kT, preferred_element_type=F32) + suf
        scores[p] = s
        mx = jnp.maximum(mx, jnp.max(s, axis=1, keepdims=True))
    p_cur = jnp.exp(s_cur - mx)
    l = p_cur
    acc = p_cur * vc_ref[...]
    for p in range(NP):
        pr = jnp.exp(scores[p] - mx)
        l = l + jnp.sum(pr, axis=1, keepdims=True)
        vT = v_refs[p][...].astype(BF16)
        acc = acc + lax.dot_general(pr.astype(BF16), vT, (((1,), (1,)), ((), ())), preferred_element_type=F32)
    o = acc / l
    ms = jnp.sum(jnp.where(diag, o * o, 0.0), axis=1, keepdims=True) * (1.0 / HD)
    on = jnp.where(diag, o * lax.rsqrt(ms + EPS), 0.0)
    o_ref[...] = jnp.sum(on, axis=0, keepdims=True) * gf_ref[...]


def _fox_step(q, k_cur, v_cur, lf_cur, cache_kT, cache_vT, cache_lfT, page_table, layer, g_f, *, HF, HD):
    DB = q.shape[0]
    NP = page_table.shape[1]
    PAGE = cache_kT.shape[3]
    DF = HF * HD
    pt = page_table.reshape(-1)
    r3 = lambda a: a.reshape(DB, 1, a.shape[-1])
    cur = lambda b, pt: (b, 0, 0)

    def page_map(p):
        return lambda b, pt: (layer, pt[b * NP + p], 0, 0)

    kv_specs = [pl.BlockSpec((None, None, DF, PAGE), page_map(p)) for p in range(NP)]
    lf_specs = [pl.BlockSpec((None,) + cache_lfT.shape[1:], lambda b, pt: (layer, 0, 0, 0))]
    grid_spec = pltpu.PrefetchScalarGridSpec(
        num_scalar_prefetch=1, grid=(DB,),
        in_specs=[pl.BlockSpec((None, 1, DF), cur)] * 3 + [pl.BlockSpec((None, 1, LANES), cur),
                                                           pl.BlockSpec((1, DF), lambda b, pt: (0, 0))]
                 + kv_specs + kv_specs + lf_specs,
        out_specs=pl.BlockSpec((None, 1, DF), cur))
    kern = functools.partial(_fox_step_kernel, NP=NP, HF=HF, HD=HD, PAGE=PAGE)
    out = pl.pallas_call(
        kern, grid_spec=grid_spec, out_shape=jax.ShapeDtypeStruct((DB, 1, DF), F32),
        compiler_params=_cparams(("arbitrary",)),
        name="fox_step")(pt, r3(q), r3(k_cur), r3(v_cur), r3(lf_cur), g_f,
                         *([cache_kT] * NP), *([cache_vT] * NP), cache_lfT)
    return out.reshape(DB, DF)


def kernel(x_prompt, x_sample, state_C, state_n, state_m, cache_k, cache_v, cache_logf, page_table, g_mix_norm, w_in, b_ig, b_fg, b_ff, g_out_m, g_out_f, w_out, g_ffn_norm, w_ffn_gate, w_ffn_up, w_ffn_down, w_router, b_router, w_moe_gate, w_moe_up, w_moe_down, g_final):
    B, T, D = x_prompt.shape
    DB = x_sample.shape[0]
    assert x_sample.shape[1] == 1
    depth = w_in.shape[0]
    HM, DK, DV = state_C.shape[2:]
    PAGE, HF, HD = cache_k.shape[2:]
    n_pool = cache_k.shape[1]
    dq, dv, df = HM * DK, HM * DV, HF * HD
    assert dq % LANES == 0 and dv % LANES == 0 and df % LANES == 0 and LANES % HD == 0
    assert HF + 2 * HM <= GATE_ROWS
    dims = (dq, dv, df, HF, HM, DK, HD)
    c_ig, c_lf = HF, HF + HM
    E = w_router.shape[-1]
    chunk = w_moe_gate.shape[-1]
    M = B * T

    o = np.cumsum([0, dq, dq, dv, dv, HM, HM, df, df, df, HF]).tolist()
    w_m = w_in[:, :, o[0]:o[4]].astype(BF16)
    w_f = w_in[:, :, o[6]:o[9]].astype(BF16)
    scat = np.zeros((N_SPLIT, LANES, HF * LANES), np.float32)
    for t in range(N_SPLIT):
        scat[t, np.arange(HF), np.arange(HF) * LANES + HD + t] = 1.0
    scat = jnp.asarray(scat, BF16)
    n_gate = HF + 2 * HM
    w_gate = jnp.concatenate([w_in[:, :, o[9]:o[10]], w_in[:, :, o[4]:o[6]],
                              jnp.zeros((depth, D, LANES - n_gate), w_in.dtype)], axis=-1).astype(BF16)
    b_gate = jnp.concatenate([b_ff, b_ig, b_fg, jnp.zeros((depth, LANES - n_gate), F32)],
                             axis=-1).astype(F32).reshape(depth, 1, LANES)
    w_out16 = w_out.astype(BF16)
    n_dense, _, d_ff = w_ffn_gate.shape
    assert d_ff % chunk == 0
    nc = d_ff // chunk
    split = lambda w: w.astype(BF16).reshape(n_dense, D, nc, chunk)
    wfgu = jnp.concatenate([split(w_ffn_gate), split(w_ffn_up)], axis=-1).transpose(0, 2, 1, 3)
    wfd = w_ffn_down.astype(BF16).reshape(n_dense, nc, chunk, D)
    wmgu = jnp.concatenate([w_moe_gate.astype(BF16), w_moe_up.astype(BF16)], axis=-1)
    wmd = w_moe_down.astype(BF16)
    w_r = jnp.concatenate([w_router, jnp.zeros(w_router.shape[:2] + (LANES - E,), F32)], axis=-1)
    b_r = jnp.concatenate([b_router, jnp.zeros((b_router.shape[0], LANES - E), F32)],
                          axis=-1).reshape(-1, 1, LANES)
    ckT = jnp.transpose(cache_k, (0, 1, 3, 4, 2)).reshape(depth, n_pool, df, PAGE)
    cvT = jnp.transpose(cache_v, (0, 1, 3, 4, 2)).reshape(depth, n_pool, df, PAGE)
    clfT = jnp.swapaxes(cache_logf, 2, 3)
    sn = state_n.reshape(depth, DB, dq)

    xp = x_prompt.reshape(M, D)
    xs = x_sample.reshape(DB, D)
    pC, pn, pm, plf = [], [], [], []
    kv_acc = ()
    sn_o, sm, sk, sv, slf = [], [], [], [], []
    sC_acc = ()
    for l in range(depth):
        gmix = g_mix_norm[l].reshape(1, D)
        gm = g_out_m[l].reshape(1, dv)
        gf = g_out_f[l].reshape(1, df)
        w_om, w_of = w_out16[l, :dv], w_out16[l, dv:]
        (qm, vm, kT, om, qp, kp, vp, kf, vf, G, Gc, GT, GcT) = _inproj(
            xp, gmix, w_m[l], w_f[l], w_gate[l], b_gate[l], scat, dims, prompt=True, tm=512, seq_len=T,
            layer=l, depth=depth, carried=kv_acc)
        kv_acc = (kf, vf)
        hm, caug, mout = _mlstm_prompt(qm, vm, kT, om, G, Gc, GT, GcT, gm, B=B, T=T, H=HM, DK=DK, DV=DV,
                                       L=min(512, T), c_ig=c_ig, c_lf=c_lf)
        hf = _fox_prompt(qp, kp, vp, gf, B=B, T=T, HF=HF, HD=HD, TB=512, G=HF, RQ=256, KS=2)
        xp = _outproj(xp, hm, hf, w_om, w_of, tm=512)
        pC.append(caug[..., :DV]); pn.append(caug[..., DV]); pm.append(mout[:, :HM, 0])
        plf.append(G[:, :HF].reshape(B, T, HF))
        (qm, km, vm, om, qf, kf, vf, G) = _inproj(
            xs, gmix, w_m[l], w_f[l], w_gate[l], b_gate[l], None, dims, prompt=False, tm=DB, seq_len=1)
        hm, Cn, nn, mn = _mlstm_step(qm, km, vm, om, G, state_C, sn, state_m, l, gm,
                                     NB=8, H=HM, DK=DK, DV=DV, c_ig=c_ig, c_lf=c_lf, carried=sC_acc)
        sC_acc = (Cn,)
        hf = _fox_step(qf, kf, vf, G, ckT, cvT, clfT, page_table, l, gf, HF=HF, HD=HD)
        xs = _outproj(xs, hm, hf, w_om, w_of, tm=DB)
        sn_o.append(nn.reshape(DB, HM, DK)); sm.append(mn[:, :HM])
        sk.append(kf.reshape(DB, 1, HF, HD)); sv.append(vf.reshape(DB, 1, HF, HD))
        slf.append(G[:, :HF].reshape(DB, 1, HF))
        gffn = g_ffn_norm[l].reshape(1, D)
        gfin = g_final.reshape(1, D) if l == depth - 1 else None
        j = l // 2
        if l % 2 == 0:
            args = (wfgu[j], wfd[j], None)
        else:
            args = (wmgu[j], wmd[j], (w_r[j], b_r[j]))
        xp = _ffn(xp, gffn, *args, gfin, tm=512 if l % 2 == 0 else 1024)
        xs = _ffn(xs, gffn, *args, gfin, tm=DB)

    tok_major = lambda a: a.reshape(depth, B, HF, HD, T).transpose(0, 1, 4, 2, 3)
    return (xp.reshape(B, T, D), xs.reshape(DB, 1, D),
            jnp.stack(pC), jnp.stack(pn), jnp.stack(pm),
            tok_major(kv_acc[0]), tok_major(kv_acc[1]), jnp.stack(plf),
            sC_acc[0], jnp.stack(sn_o), jnp.stack(sm),
            jnp.stack(sk), jnp.stack(sv), jnp.stack(slf))
```

```python
import functools

import numpy as np
import jax
import jax.numpy as jnp
from jax import lax
from jax.experimental import pallas as pl
from jax.experimental.pallas import tpu as pltpu

F32 = jnp.float32
BF16 = jnp.bfloat16
EPS = 1e-6
NEG_INIT = -1e30
TOP_K = 2
LANES = 128
GATE_ROWS = 16
VMEM_LIMIT = 56 * 1024 * 1024
LOG2E = 1.4426950408889634
N_SPLIT = 3
FULL_TILE_ROWS = 512


def _cparams(sem):
    return pltpu.CompilerParams(dimension_semantics=sem, vmem_limit_bytes=VMEM_LIMIT)


def _rms(x, g):
    var = jnp.mean(x * x, axis=-1, keepdims=True)
    return x * lax.rsqrt(var + EPS) * g


def _log_sigmoid(x):
    return jnp.minimum(x, 0.0) - jnp.log(1.0 + jnp.exp(-jnp.abs(x)))


def _sigmoid(x):
    return 1.0 / (1.0 + jnp.exp(-x))


def _cumsum_rows(a):
    n = a.shape[0]
    row = lax.broadcasted_iota(jnp.int32, a.shape, 0)
    k = 1
    while k < n:
        a = a + jnp.where(row >= k, pltpu.roll(a, k, 0), 0.0)
        k *= 2
    return a


def _spread_heads(z, hd, tail):
    lane = lax.broadcasted_iota(jnp.int32, (1, LANES), 1)
    blocks = []
    for p in range(z.shape[1] // LANES):
        a = z[:, p * LANES:(p + 1) * LANES]
        blocks.append(jnp.where(lane < hd, a, tail(2 * p)))
        blocks.append(jnp.where(lane < hd, pltpu.roll(a, hd, 1), tail(2 * p + 1)))
    return jnp.concatenate(blocks, axis=1)


def _inproj_kernel(x_ref, g_ref, wm_ref, wf_ref, wg_ref, bg_ref, *refs, dq, dv, df, hf, hm, hd, kscale, qscale,
                   prompt, tiles_per_seq, n_carried):
    if prompt:
        (scat_ref, qm_ref, vm_ref, kT_ref, om_ref, qp_ref, kp_ref, vp_ref, kf_ref, vf_ref,
         G_ref, Gc_ref, GT_ref, GcT_ref, carry_ref) = refs[:1] + refs[1 + n_carried:]
    else:
        (qm_ref, km_ref, vm_ref, om_ref, qf_ref, kf_ref, vf_ref, G_ref) = refs
    hn = _rms(x_ref[...], g_ref[...]).astype(BF16)

    def proj(w_ref, off, width):
        return jnp.dot(hn, w_ref[:, off:off + width], preferred_element_type=F32)

    zg = jnp.dot(hn, wg_ref[...], preferred_element_type=F32) + bg_ref[...]
    col = lax.broadcasted_iota(jnp.int32, zg.shape, 1)
    is_log = (col < hf) | ((col >= hf + hm) & (col < hf + 2 * hm))
    is_lin = (col >= hf) & (col < hf + hm)
    G = jnp.where(is_log, _log_sigmoid(zg), jnp.where(is_lin, zg, 0.0))
    G_ref[...] = G

    qm_ref[...] = proj(wm_ref, 0, dq).astype(qm_ref.dtype)
    km = proj(wm_ref, dq, dq) * kscale
    if prompt:
        kT_ref[...] = jnp.transpose(km).astype(kT_ref.dtype)
    else:
        km_ref[...] = km
    vm_ref[...] = proj(wm_ref, 2 * dq, dv).astype(vm_ref.dtype)
    om_ref[...] = proj(wm_ref, 2 * dq + dv, dv).astype(om_ref.dtype)
    if not prompt:
        qf_ref[...] = proj(wf_ref, 0, df) * qscale
        kf_ref[...] = proj(wf_ref, df, df)
        vf_ref[...] = proj(wf_ref, 2 * df, df)
        return

    i = pl.program_id(0)

    @pl.when(i % tiles_per_seq == 0)
    def _():
        carry_ref[...] = jnp.zeros_like(carry_ref)

    Gc = _cumsum_rows(G) + carry_ref[0:1, :]
    carry_ref[0:1, :] = Gc[-1:, :]
    Gc_ref[...] = Gc
    GT_ref[...] = jnp.transpose(G)[:GATE_ROWS, :]
    GcT_ref[...] = jnp.transpose(Gc)[:GATE_ROWS, :]

    lane = lax.broadcasted_iota(jnp.int32, (1, LANES), 1)
    zq = proj(wf_ref, 0, df) * (qscale * LOG2E)
    zk = proj(wf_ref, df, df)
    zv = proj(wf_ref, 2 * df, df)
    kf_ref[...] = jnp.transpose(zk)
    vf_ref[...] = jnp.transpose(zv)
    rem = Gc * (-LOG2E)
    bias = jnp.zeros((zk.shape[0], hf * LANES), F32)
    for t in range(N_SPLIT):
        piece = rem.astype(BF16)
        rem = rem - piece.astype(F32)
        bias = bias + jnp.dot(piece, scat_ref[t], preferred_element_type=F32)
    q_tail = jnp.where((lane >= hd) & (lane < hd + N_SPLIT), 1.0, 0.0)
    v_tail = jnp.where(lane == hd, 1.0, 0.0)
    qp_ref[...] = _spread_heads(zq, hd, lambda h: q_tail).astype(BF16)
    kp_ref[...] = _spread_heads(zk, hd, lambda h: bias[:, h * LANES:(h + 1) * LANES]).astype(BF16)
    vp_ref[...] = _spread_heads(zv, hd, lambda h: v_tail).astype(BF16)


def _inproj(x, g, w_m, w_f, w_gate, b_gate, scat, dims, *, prompt, tm, seq_len, layer=0, depth=1, carried=()):
    M, D = x.shape
    dq, dv, df, hf, hm, dk, hd = dims
    dfp = hf * LANES
    tm = min(tm, M)
    assert M % tm == 0
    if prompt:
        assert seq_len % tm == 0
    row = lambda i: (i, 0)
    colb = lambda i: (0, i)
    const = lambda i: (0, 0)
    tps = max(seq_len // tm, 1)
    seqb = lambda i: (layer, i // tps, 0, i % tps)
    aliases = {}
    args = [x, g, w_m, w_f, w_gate, b_gate]
    in_specs = [pl.BlockSpec((tm, D), row), pl.BlockSpec((1, D), const),
                pl.BlockSpec(w_m.shape, const), pl.BlockSpec(w_f.shape, const),
                pl.BlockSpec((D, LANES), const), pl.BlockSpec((1, LANES), const)]
    if prompt:
        args.append(scat)
        in_specs.append(pl.BlockSpec(scat.shape, lambda i: (0, 0, 0)))
        kv_out = 7
        for n, buf in enumerate(carried):
            aliases[len(args)] = kv_out + n
            args.append(buf)
            in_specs.append(pl.BlockSpec(memory_space=pl.ANY))
        out_shape = [jax.ShapeDtypeStruct((M, dq), BF16), jax.ShapeDtypeStruct((M, dv), BF16),
                     jax.ShapeDtypeStruct((dq, M), BF16), jax.ShapeDtypeStruct((M, dv), BF16),
                     jax.ShapeDtypeStruct((M, dfp), BF16), jax.ShapeDtypeStruct((M, dfp), BF16),
                     jax.ShapeDtypeStruct((M, dfp), BF16),
                     jax.ShapeDtypeStruct((depth, M // seq_len, df, seq_len), F32),
                     jax.ShapeDtypeStruct((depth, M // seq_len, df, seq_len), F32),
                     jax.ShapeDtypeStruct((M, LANES), F32), jax.ShapeDtypeStruct((M, LANES), F32),
                     jax.ShapeDtypeStruct((GATE_ROWS, M), F32), jax.ShapeDtypeStruct((GATE_ROWS, M), F32)]
        out_specs = [pl.BlockSpec((tm, dq), row), pl.BlockSpec((tm, dv), row),
                     pl.BlockSpec((dq, tm), colb), pl.BlockSpec((tm, dv), row),
                     pl.BlockSpec((tm, dfp), row), pl.BlockSpec((tm, dfp), row), pl.BlockSpec((tm, dfp), row),
                     pl.BlockSpec((None, None, df, tm), seqb), pl.BlockSpec((None, None, df, tm), seqb),
                     pl.BlockSpec((tm, LANES), row), pl.BlockSpec((tm, LANES), row),
                     pl.BlockSpec((GATE_ROWS, tm), colb), pl.BlockSpec((GATE_ROWS, tm), colb)]
        scratch = [pltpu.VMEM((8, LANES), F32)]
    else:
        out_shape = ([jax.ShapeDtypeStruct((M, dq), F32)] * 2 + [jax.ShapeDtypeStruct((M, dv), F32)] * 2
                     + [jax.ShapeDtypeStruct((M, df), F32)] * 3 + [jax.ShapeDtypeStruct((M, LANES), F32)])
        out_specs = ([pl.BlockSpec((tm, dq), row)] * 2 + [pl.BlockSpec((tm, dv), row)] * 2
                     + [pl.BlockSpec((tm, df), row)] * 3 + [pl.BlockSpec((tm, LANES), row)])
        scratch = []
    kern = functools.partial(_inproj_kernel, dq=dq, dv=dv, df=df, hf=hf, hm=hm, hd=hd,
                             kscale=float(dk) ** -0.5, qscale=float(hd) ** -0.5,
                             prompt=prompt, tiles_per_seq=max(seq_len // tm, 1), n_carried=len(carried))
    return pl.pallas_call(
        kern, grid=(M // tm,), in_specs=in_specs, out_specs=out_specs, out_shape=out_shape,
        input_output_aliases=aliases, scratch_shapes=scratch, compiler_params=_cparams(("arbitrary",)),
        name="inproj_prompt" if prompt else "inproj_sample")(*args)


def _mlstm_prompt_kernel(q_ref, v_ref, kT_ref, o_ref, G_ref, Gc_ref, GT_ref, GcT_ref, gm_ref,
                         hm_ref, caug_ref, mout_ref, bprev_ref, m_ref, *, H, DK, DV, L, c_ig, c_lf):
    c = pl.program_id(1)

    @pl.when(c == 0)
    def _():
        caug_ref[...] = jnp.zeros_like(caug_ref)
        m_ref[...] = jnp.full_like(m_ref, NEG_INIT)
        bprev_ref[...] = jnp.zeros_like(bprev_ref)

    row = lax.broadcasted_iota(jnp.int32, (L, L), 0)
    colL = lax.broadcasted_iota(jnp.int32, (L, L), 1)
    causal = row >= colL
    lane = lax.broadcasted_iota(jnp.int32, (L, LANES), 1)
    ones_col = jnp.where(lane == 0, 1.0, 0.0)
    G = G_ref[...]
    Gc = Gc_ref[...]
    GT = GT_ref[...]
    GcT = GcT_ref[...]
    for h in range(H):
        bp = bprev_ref[0:1, c_lf + h:c_lf + h + 1]
        b_col = Gc[:, c_lf + h:c_lf + h + 1] - bp
        b_row = GcT[c_lf + h:c_lf + h + 1, :] - bp
        i_row = GT[c_ig + h:c_ig + h + 1, :]
        i_col = G[:, c_ig + h:c_ig + h + 1]
        m_prev = m_ref[h:h + 1, 0:1]
        dm = jnp.where(causal, b_col - b_row + i_row, -jnp.inf)
        inter = b_col + m_prev
        m_row = jnp.maximum(inter, jnp.max(dm, axis=-1, keepdims=True))
        w_intra = jnp.exp(dm - m_row)
        w_inter = jnp.exp(inter - m_row)
        qh = q_ref[:, h * DK:(h + 1) * DK]
        kTh = kT_ref[h * DK:(h + 1) * DK, :]
        vh = v_ref[:, h * DV:(h + 1) * DV].astype(F32)
        vaug = jnp.concatenate([vh, ones_col], axis=1)
        s = jnp.dot(qh, kTh, preferred_element_type=F32) * w_intra
        intra = jnp.dot(s.astype(BF16), vaug.astype(BF16), preferred_element_type=F32)
        cst = caug_ref[h]
        inter_qc = jnp.dot(qh, cst.astype(BF16), preferred_element_type=F32)
        r = w_inter * inter_qc + intra
        num = r[:, :DV]
        den = r[:, DV:DV + 1]
        hval = num / jnp.maximum(jnp.abs(den), jnp.exp(-m_row))
        m_new = m_row[L - 1:L, :]
        decay = jnp.exp(inter[L - 1:L, :] - m_new)
        w_k = jnp.exp(b_col[L - 1:L, :] - b_col + i_col - m_new)
        upd = jnp.dot(kTh, (vaug * w_k).astype(BF16), preferred_element_type=F32)
        caug_ref[h] = decay * cst + upd
        m_ref[h:h + 1, :] = jnp.broadcast_to(m_new, (1, LANES))
        gate = _sigmoid(o_ref[:, h * DV:(h + 1) * DV].astype(F32))
        hm_ref[:, h * DV:(h + 1) * DV] = (_rms(hval, gm_ref[:, h * DV:(h + 1) * DV]) * gate).astype(hm_ref.dtype)
    bprev_ref[0:1, :] = Gc[L - 1:L, :]

    @pl.when(c == pl.num_programs(1) - 1)
    def _():
        mout_ref[...] = m_ref[...]


def _mlstm_prompt(qm, vm, kT, om, G, Gc, GT, GcT, g_m, *, B, T, H, DK, DV, L, c_ig, c_lf):
    M = B * T
    NC = T // L
    assert T % L == 0
    rowb = lambda b, c: (b * NC + c, 0)
    colb = lambda b, c: (0, b * NC + c)
    kern = functools.partial(_mlstm_prompt_kernel, H=H, DK=DK, DV=DV, L=L, c_ig=c_ig, c_lf=c_lf)
    return pl.pallas_call(
        kern, grid=(B, NC),
        in_specs=[pl.BlockSpec((L, H * DK), rowb), pl.BlockSpec((L, H * DV), rowb),
                  pl.BlockSpec((H * DK, L), colb), pl.BlockSpec((L, H * DV), rowb),
                  pl.BlockSpec((L, LANES), rowb), pl.BlockSpec((L, LANES), rowb),
                  pl.BlockSpec((GATE_ROWS, L), colb), pl.BlockSpec((GATE_ROWS, L), colb),
                  pl.BlockSpec((1, H * DV), lambda b, c: (0, 0))],
        out_specs=[pl.BlockSpec((L, H * DV), rowb),
                   pl.BlockSpec((None, H, DK, DV + LANES), lambda b, c: (b, 0, 0, 0)),
                   pl.BlockSpec((None, 8, LANES), lambda b, c: (b, 0, 0))],
        out_shape=[jax.ShapeDtypeStruct((M, H * DV), BF16),
                   jax.ShapeDtypeStruct((B, H, DK, DV + LANES), F32),
                   jax.ShapeDtypeStruct((B, 8, LANES), F32)],
        scratch_shapes=[pltpu.VMEM((8, LANES), F32), pltpu.VMEM((8, LANES), F32)],
        compiler_params=_cparams(("parallel", "arbitrary")),
        name="mlstm_prompt")(qm, vm, kT, om, G, Gc, GT, GcT, g_m)


def _fox_prompt_kernel(ii_ref, jj_ref, q_ref, k_ref, v_ref, gf_ref, o_ref, m_sc, acc_sc, *, TB, HD, G, RQ, KS):
    p = pl.program_id(2)
    i = ii_ref[p]
    j = jj_ref[p]
    FULL, MASKED = 0, 1

    @pl.when(j == 0)
    def _():
        m_sc[...] = jnp.full_like(m_sc, -jnp.inf)
        acc_sc[...] = jnp.zeros_like(acc_sc)

    def update(modes):
        items = [(c, g, r) for c in range(len(modes)) for g in range(G) for r in range(TB // RQ)]

        def qk(c, g, r):
            q = q_ref[r * RQ:(r + 1) * RQ, g * LANES:(g + 1) * LANES]
            k = k_ref[c * TB:(c + 1) * TB, g * LANES:(g + 1) * LANES]
            return lax.dot_general(q, k, (((1,), (1,)), ((), ())), preferred_element_type=F32)

        s_next = qk(*items[0])
        for n, (c, g, r) in enumerate(items):
            s = s_next
            if n + 1 < len(items):
                s_next = qk(*items[n + 1])
            rows = slice(r * RQ, (r + 1) * RQ)
            v = v_ref[c * TB:(c + 1) * TB, g * LANES:(g + 1) * LANES]
            if modes[c] == MASKED:
                rr = lax.broadcasted_iota(jnp.int32, (RQ, TB), 0) + r * RQ
                cc = lax.broadcasted_iota(jnp.int32, (RQ, TB), 1)
                s = jnp.where(rr >= cc, s, -jnp.inf)
            m_old = m_sc[g, rows, :]
            m_new = jnp.maximum(m_old, jnp.max(s, axis=1, keepdims=True))
            alpha = jnp.exp2(m_old - m_new)
            pr = jnp.concatenate([jnp.exp2(s[:, cl * LANES:(cl + 1) * LANES] - m_new).astype(BF16)
                                  for cl in range(TB // LANES)], axis=1)
            acc_sc[g, rows, :] = alpha * acc_sc[g, rows, :] + jnp.dot(pr, v, preferred_element_type=F32)
            m_sc[g, rows, :] = m_new

    last = i // KS

    @pl.when(j < last)
    def _():
        update([FULL] * KS)

    for d in range(KS):
        @pl.when((j == last) & (i % KS == d))
        def _():
            update([FULL] * d + [MASKED])

    @pl.when(j == last)
    def _():
        lane = lax.broadcasted_iota(jnp.int32, (1, LANES), 1)
        lrow = lax.broadcasted_iota(jnp.int32, (LANES, LANES), 0)
        wmat = jnp.where(lrow < HD, 1.0 / HD, jnp.where(lrow == HD, EPS, 0.0))
        for gp in range(G // 2):
            outs = []
            for g in (2 * gp, 2 * gp + 1):
                a = acc_sc[g]
                outs.append(a * lax.rsqrt(jnp.dot(a * a, wmat, preferred_element_type=F32)))
            out = jnp.where(lane < HD, outs[0], pltpu.roll(outs[1], HD, 1))
            o_ref[:, gp * LANES:(gp + 1) * LANES] = (out * gf_ref[:, gp * LANES:(gp + 1) * LANES]).astype(o_ref.dtype)


def _fox_prompt(qp, kp, vp, g_f, *, B, T, HF, HD, TB, G, RQ, KS):
    M = B * T
    TB = min(TB, T)
    RQ = min(RQ, TB)
    KS = min(KS, T // TB)
    TK = KS * TB
    assert T % TK == 0 and TB % RQ == 0 and HF % G == 0 and G % 2 == 0 and 2 * HD == LANES
    nq = T // TB
    nkv = T // TK
    ii = np.concatenate([np.full(i // KS + 1, i, np.int32) for i in range(nq)])
    jj = np.concatenate([np.arange(i // KS + 1, dtype=np.int32) for i in range(nq)])
    qmap = lambda b, hg, p, ii, jj: (b * nq + ii[p], hg)
    kmap = lambda b, hg, p, ii, jj: (b * nkv + jj[p], hg)
    kern = functools.partial(_fox_prompt_kernel, TB=TB, HD=HD, G=G, RQ=RQ, KS=KS)
    grid_spec = pltpu.PrefetchScalarGridSpec(
        num_scalar_prefetch=2, grid=(B, HF // G, len(ii)),
        in_specs=[pl.BlockSpec((TB, G * LANES), qmap), pl.BlockSpec((TK, G * LANES), kmap),
                  pl.BlockSpec((TK, G * LANES), kmap),
                  pl.BlockSpec((1, G * HD), lambda b, hg, p, ii, jj: (0, hg))],
        out_specs=pl.BlockSpec((TB, G * HD), qmap),
        scratch_shapes=[pltpu.VMEM((G, TB, LANES), F32)] * 2)
    return pl.pallas_call(
        kern, grid_spec=grid_spec, out_shape=jax.ShapeDtypeStruct((M, HF * HD), BF16),
        compiler_params=_cparams(("parallel", "parallel", "arbitrary")),
        name="fox_prompt")(jnp.asarray(ii), jnp.asarray(jj), qp, kp, vp, g_f)


def _outproj_kernel(x_ref, hm_ref, hf_ref, wm_ref, wf_ref, o_ref):
    y = jnp.dot(hm_ref[...].astype(BF16), wm_ref[...], preferred_element_type=F32)
    y = y + jnp.dot(hf_ref[...].astype(BF16), wf_ref[...], preferred_element_type=F32)
    o_ref[...] = x_ref[...] + y


def _outproj(x, hm, hf, w_m, w_f, *, tm):
    M, D = x.shape
    tm = min(tm, M)
    assert M % tm == 0
    row = lambda i: (i, 0)
    const = lambda i: (0, 0)
    return pl.pallas_call(
        _outproj_kernel, grid=(M // tm,),
        in_specs=[pl.BlockSpec((tm, D), row), pl.BlockSpec((tm, hm.shape[1]), row),
                  pl.BlockSpec((tm, hf.shape[1]), row),
                  pl.BlockSpec(w_m.shape, const), pl.BlockSpec(w_f.shape, const)],
        out_specs=pl.BlockSpec((tm, D), row), out_shape=jax.ShapeDtypeStruct((M, D), F32),
        compiler_params=_cparams(("parallel",)), name="outproj")(x, hm, hf, w_m, w_f)


def _ffn_kernel(x_ref, g_ref, wgu_ref, wd_ref, *refs, moe, n_experts, final, cap):
    refs = list(refs)
    if moe:
        wr_ref, br_ref = refs[:2]
        refs = refs[2:]
    if final:
        gfin_ref = refs[0]
        refs = refs[1:]
    if moe:
        o_ref, hn_sc, acc_sc, comb_sc, rk_sc, rkT_sc, cnt_sc = refs
    else:
        o_ref, hn_sc, acc_sc = refs
    e = pl.program_id(1)
    tm = hn_sc.shape[0]
    chunk = wd_ref.shape[0]

    def swiglu(h):
        z = jnp.dot(h, wgu_ref[...], preferred_element_type=F32)
        a = z[:, :chunk]
        act = (a * _sigmoid(a) * z[:, chunk:]).astype(BF16)
        return jnp.dot(act, wd_ref[...], preferred_element_type=F32)

    @pl.when(e == 0)
    def _():
        hn = _rms(x_ref[...], g_ref[...])
        hn_sc[...] = hn.astype(BF16)
        acc_sc[...] = jnp.zeros_like(acc_sc)
        if moe:
            lane = lax.broadcasted_iota(jnp.int32, (tm, LANES), 1)
            lanef = lane.astype(F32)
            hn_hi = hn.astype(BF16)
            hn_lo = (hn - hn_hi.astype(F32)).astype(BF16)
            wr = wr_ref[...]
            wr_hi = wr.astype(BF16)
            wr_lo = (wr - wr_hi.astype(F32)).astype(BF16)
            lg = (jnp.dot(hn_hi, wr_hi, preferred_element_type=F32) + jnp.dot(hn_hi, wr_lo, preferred_element_type=F32)
                  + jnp.dot(hn_lo, wr_hi, preferred_element_type=F32)) + br_ref[...]
            lg = jnp.where(lane < n_experts, lg, -jnp.inf)
            m1 = jnp.max(lg, axis=1, keepdims=True)
            i1 = jnp.min(jnp.where(lg == m1, lanef, float(LANES)), axis=1, keepdims=True)
            lg2 = jnp.where(lanef == i1, -jnp.inf, lg)
            m2 = jnp.max(lg2, axis=1, keepdims=True)
            i2 = jnp.min(jnp.where(lg2 == m2, lanef, float(LANES)), axis=1, keepdims=True)
            e2 = jnp.exp(m2 - m1)
            g1 = 1.0 / (1.0 + e2)
            g2 = e2 / (1.0 + e2)
            comb_sc[...] = jnp.where(lanef == i1, g1, 0.0) + jnp.where(lanef == i2, g2, 0.0)
            routed = jnp.where((lanef == i1) | (lanef == i2), 1.0, 0.0)
            incl = _cumsum_rows(routed)
            rk = jnp.where(routed > 0.0, incl - 1.0, -1.0)
            rk_sc[...] = rk
            rkT_sc[...] = jnp.transpose(rk)
            cnt_sc[...] = jnp.broadcast_to(incl[tm - 1:tm, :], cnt_sc.shape)

    if not moe:
        acc_sc[...] += swiglu(hn_sc[...])
    else:
        lane = lax.broadcasted_iota(jnp.int32, (tm, LANES), 1)
        ce = jnp.sum(jnp.where(lane == e, comb_sc[...], 0.0), axis=1, keepdims=True)
        lane_row = lax.broadcasted_iota(jnp.int32, (1, LANES), 1)
        count = jnp.sum(jnp.where(lane_row == e, cnt_sc[0:1, :], 0.0))

        @pl.when(count > cap)
        def _():
            rb = min(tm, FULL_TILE_ROWS)
            for r in range(tm // rb):
                rows = slice(r * rb, (r + 1) * rb)
                acc_sc[rows, :] += ce[rows, :] * swiglu(hn_sc[rows, :])

        @pl.when(count <= cap)
        def _():
            rk_row = rkT_sc[pl.ds(e, 1), :]
            slot = lax.broadcasted_iota(jnp.int32, (cap, tm), 0).astype(F32)
            pick = jnp.where(rk_row == slot, 1.0, 0.0).astype(BF16)
            xs = jnp.dot(pick, hn_sc[...], preferred_element_type=F32).astype(BF16)
            y = swiglu(xs).astype(BF16)
            rk_col = jnp.sum(jnp.where(lane == e, rk_sc[...], 0.0), axis=1, keepdims=True)
            slot_l = lax.broadcasted_iota(jnp.int32, (tm, cap), 1).astype(F32)
            place = jnp.where(rk_col == slot_l, 1.0, 0.0).astype(BF16)
            acc_sc[...] += ce * jnp.dot(place, y, preferred_element_type=F32)

    @pl.when(e == pl.num_programs(1) - 1)
    def _():
        xo = x_ref[...] + acc_sc[...]
        if final:
            xo = _rms(xo, gfin_ref[...])
        o_ref[...] = xo


def _expert_capacity(tm, n_experts):
    mean = tm * TOP_K / n_experts
    return min(tm, max(64, int(-(-1.25 * mean // 64)) * 64))


def _ffn(x, g, wgu, wd, router, g_final, *, tm):
    M, D = x.shape
    tm = min(tm, M)
    assert M % tm == 0
    moe = router is not None
    final = g_final is not None
    E, chunk = wd.shape[0], wd.shape[1]
    row = lambda i, e: (i, 0)
    const = lambda i, e: (0, 0)
    args = [x, g, wgu, wd]
    in_specs = [pl.BlockSpec((tm, D), row), pl.BlockSpec((1, D), const),
                pl.BlockSpec((None, D, 2 * chunk), lambda i, e: (e, 0, 0)),
                pl.BlockSpec((None, chunk, D), lambda i, e: (e, 0, 0))]
    scratch = [pltpu.VMEM((tm, D), BF16), pltpu.VMEM((tm, D), F32)]
    if moe:
        args += list(router)
        in_specs += [pl.BlockSpec((D, LANES), const), pl.BlockSpec((1, LANES), const)]
        scratch += [pltpu.VMEM((tm, LANES), F32), pltpu.VMEM((tm, LANES), F32), pltpu.VMEM((LANES, tm), F32),
                    pltpu.VMEM((8, LANES), F32)]
    if final:
        args.append(g_final)
        in_specs.append(pl.BlockSpec((1, D), const))
    kern = functools.partial(_ffn_kernel, moe=moe, n_experts=E, final=final, cap=_expert_capacity(tm, E))
    return pl.pallas_call(
        kern, grid=(M // tm, E), in_specs=in_specs,
        out_specs=pl.BlockSpec((tm, D), row), out_shape=jax.ShapeDtypeStruct((M, D), F32),
        scratch_shapes=scratch, compiler_params=_cparams(("parallel", "arbitrary")),
        name="ffn_moe" if moe else "ffn_dense")(*args)


def _mlstm_step_kernel(q_ref, k_ref, v_ref, o_ref, kc_ref, G_ref, C_ref, n_ref, m_ref, gm_ref, *refs,
                       NB, H, DK, DV, c_ig, c_lf):
    hm_ref, Cn_ref, nn_ref, mn_ref = refs[-4:]
    sub = lax.broadcasted_iota(jnp.int32, (NB, DV), 0)
    lane = lax.broadcasted_iota(jnp.int32, (NB, LANES), 1)
    m_out = jnp.zeros((NB, LANES), F32)
    for h in range(H):
        q = q_ref[:, h * DK:(h + 1) * DK]
        k = k_ref[:, h * DK:(h + 1) * DK]
        v = v_ref[:, h * DV:(h + 1) * DV]
        n = n_ref[:, h * DK:(h + 1) * DK]
        ig = G_ref[:, c_ig + h:c_ig + h + 1]
        lf = G_ref[:, c_lf + h:c_lf + h + 1]
        m0 = m_ref[:, h:h + 1]
        inter = lf + m0
        m_new = jnp.maximum(inter, ig)
        w_inter = jnp.exp(inter - m_new)
        w_intra = jnp.exp(ig - m_new)
        s = jnp.sum(q * k, axis=1, keepdims=True) * w_intra
        qn = jnp.sum(q * n, axis=1, keepdims=True)
        vw = v * w_intra
        kc = kc_ref[h * DK:(h + 1) * DK, :]
        qC = jnp.zeros((NB, DV), F32)
        for bb in range(NB):
            C = C_ref[bb, h]
            qC = jnp.where(sub == bb, jnp.dot(q, C, preferred_element_type=F32), qC)
            vsel = jnp.concatenate([jnp.where(sub == bb, vw, 0.0), jnp.zeros((LANES - NB, DV), F32)], axis=0)
            Cn_ref[bb, h] = w_inter[bb:bb + 1, :] * C + jnp.dot(kc, vsel, preferred_element_type=F32)
        num = w_inter * qC + s * v
        den = w_inter * qn + s
        hval = num / jnp.maximum(jnp.abs(den), jnp.exp(-m_new))
        nn_ref[:, h * DK:(h + 1) * DK] = w_inter * n + w_intra * k
        m_out = jnp.where(lane == h, m_new, m_out)
        gate = _sigmoid(o_ref[:, h * DV:(h + 1) * DV])
        hm_ref[:, h * DV:(h + 1) * DV] = _rms(hval, gm_ref[:, h * DV:(h + 1) * DV]) * gate
    mn_ref[...] = m_out


def _mlstm_step(q, k, v, o, G, state_C, state_n, state_m, layer, g_m, *, NB, H, DK, DV, c_ig, c_lf, carried=()):
    DB = q.shape[0]
    depth = state_C.shape[0]
    NB = min(NB, DB)
    assert DB % NB == 0 and NB <= LANES
    S = DB // NB
    kc = jnp.pad(k.reshape(S, NB, H * DK).transpose(0, 2, 1), ((0, 0), (0, 0), (0, LANES - NB)))
    row = lambda s: (s, 0)
    args = [q, k, v, o, kc, G, state_C, state_n, state_m, g_m]
    in_specs = [pl.BlockSpec((NB, H * DK), row), pl.BlockSpec((NB, H * DK), row),
                pl.BlockSpec((NB, H * DV), row), pl.BlockSpec((NB, H * DV), row),
                pl.BlockSpec((None, H * DK, LANES), lambda s: (s, 0, 0)),
                pl.BlockSpec((NB, LANES), row),
                pl.BlockSpec((None, NB, H, DK, DV), lambda s: (layer, s, 0, 0, 0)),
                pl.BlockSpec((None, NB, H * DK), lambda s: (layer, s, 0)),
                pl.BlockSpec((None, NB, H), lambda s: (layer, s, 0)),
                pl.BlockSpec((1, H * DV), lambda s: (0, 0))]
    aliases = {}
    for buf in carried:
        aliases[len(args)] = 1
        args.append(buf)
        in_specs.append(pl.BlockSpec(memory_space=pl.ANY))
    kern = functools.partial(_mlstm_step_kernel, NB=NB, H=H, DK=DK, DV=DV, c_ig=c_ig, c_lf=c_lf)
    return pl.pallas_call(
        kern, grid=(S,), in_specs=in_specs,
        out_specs=[pl.BlockSpec((NB, H * DV), row),
                   pl.BlockSpec((None, NB, H, DK, DV), lambda s: (layer, s, 0, 0, 0)),
                   pl.BlockSpec((NB, H * DK), row), pl.BlockSpec((NB, LANES), row)],
        out_shape=[jax.ShapeDtypeStruct((DB, H * DV), F32), jax.ShapeDtypeStruct((depth, DB, H, DK, DV), F32),
                   jax.ShapeDtypeStruct((DB, H * DK), F32), jax.ShapeDtypeStruct((DB, LANES), F32)],
        input_output_aliases=aliases, compiler_params=_cparams(("parallel",)),
        name="mlstm_step")(*args)


def _fox_step_kernel(pt_ref, q_ref, kc_ref, vc_ref, G_ref, gf_ref, *refs, NP, HF, HD, PAGE):
    k_refs = refs[:NP]
    v_refs = refs[NP:2 * NP]
    lf_ref = refs[2 * NP]
    o_ref = refs[2 * NP + 1]
    b = pl.program_id(0)
    DF = HF * HD
    sub = lax.broadcasted_iota(jnp.int32, (HF, DF), 0)
    lane = lax.broadcasted_iota(jnp.int32, (HF, DF), 1)
    diag = lane // HD == sub
    q = q_ref[...]
    qbd = jnp.where(diag, jnp.broadcast_to(q, (HF, DF)), 0.0)
    qbd16 = qbd.astype(BF16)
    sub_g = lax.broadcasted_iota(jnp.int32, (HF, LANES), 0)
    lane_g = lax.broadcasted_iota(jnp.int32, (HF, LANES), 1)
    carry = jnp.sum(jnp.where(lane_g == sub_g, jnp.broadcast_to(G_ref[...], (HF, LANES)), 0.0),
                    axis=1, keepdims=True)
    ti = lax.broadcasted_iota(jnp.int32, (PAGE, PAGE), 0)
    tj = lax.broadcasted_iota(jnp.int32, (PAGE, PAGE), 1)
    later = jnp.where(ti > tj, 1.0, 0.0)
    s_cur = jnp.sum(qbd * kc_ref[...], axis=1, keepdims=True)
    scores = [None] * NP
    mx = s_cur
    for p in range(NP - 1, -1, -1):
        lfT = lf_ref[pt_ref[b * NP + p]]
        suf = jnp.dot(lfT, later, preferred_element_type=F32, precision=lax.Precision.HIGHEST) + carry
        carry = carry + jnp.sum(lfT, axis=1, keepdims=True)
        kT = k_refs[p][...].astype(BF16)
        s = jnp.dot(qbd16, kT, preferred_element_type=F32) + suf
        scores[p] = s
        mx = jnp.maximum(mx, jnp.max(s, axis=1, keepdims=True))
    p_cur = jnp.exp(s_cur - mx)
    l = p_cur
    acc = p_cur * vc_ref[...]
    for p in range(NP):
        pr = jnp.exp(scores[p] - mx)
        l = l + jnp.sum(pr, axis=1, keepdims=True)
        vT = v_refs[p][...].astype(BF16)
        acc = acc + lax.dot_general(pr.astype(BF16), vT, (((1,), (1,)), ((), ())), preferred_element_type=F32)
    o = acc / l
    ms = jnp.sum(jnp.where(diag, o * o, 0.0), axis=1, keepdims=True) * (1.0 / HD)
    on = jnp.where(diag, o * lax.rsqrt(ms + EPS), 0.0)
    o_ref[...] = jnp.sum(on, axis=0, keepdims=True) * gf_ref[...]


def _fox_step(q, k_cur, v_cur, lf_cur, cache_kT, cache_vT, cache_lfT, page_table, layer, g_f, *, HF, HD):
    DB = q.shape[0]
    NP = page_table.shape[1]
    PAGE = cache_kT.shape[3]
    DF = HF * HD
    pt = page_table.reshape(-1)
    r3 = lambda a: a.reshape(DB, 1, a.shape[-1])
    cur = lambda b, pt: (b, 0, 0)

    def page_map(p):
        return lambda b, pt: (layer, pt[b * NP + p], 0, 0)

    kv_specs = [pl.BlockSpec((None, None, DF, PAGE), page_map(p)) for p in range(NP)]
    lf_specs = [pl.BlockSpec((None,) + cache_lfT.shape[1:], lambda b, pt: (layer, 0, 0, 0))]
    grid_spec = pltpu.PrefetchScalarGridSpec(
        num_scalar_prefetch=1, grid=(DB,),
        in_specs=[pl.BlockSpec((None, 1, DF), cur)] * 3 + [pl.BlockSpec((None, 1, LANES), cur),
                                                           pl.BlockSpec((1, DF), lambda b, pt: (0, 0))]
                 + kv_specs + kv_specs + lf_specs,
        out_specs=pl.BlockSpec((None, 1, DF), cur))
    kern = functools.partial(_fox_step_kernel, NP=NP, HF=HF, HD=HD, PAGE=PAGE)
    out = pl.pallas_call(
        kern, grid_spec=grid_spec, out_shape=jax.ShapeDtypeStruct((DB, 1, DF), F32),
        compiler_params=_cparams(("arbitrary",)),
        name="fox_step")(pt, r3(q), r3(k_cur), r3(v_cur), r3(lf_cur), g_f,
                         *([cache_kT] * NP), *([cache_vT] * NP), cache_lfT)
    return out.reshape(DB, DF)


def kernel(x_prompt, x_sample, state_C, state_n, state_m, cache_k, cache_v, cache_logf, page_table, g_mix_norm, w_in, b_ig, b_fg, b_ff, g_out_m, g_out_f, w_out, g_ffn_norm, w_ffn_gate, w_ffn_up, w_ffn_down, w_router, b_router, w_moe_gate, w_moe_up, w_moe_down, g_final):
    B, T, D = x_prompt.shape
    DB = x_sample.shape[0]
    assert x_sample.shape[1] == 1
    depth = w_in.shape[0]
    HM, DK, DV = state_C.shape[2:]
    PAGE, HF, HD = cache_k.shape[2:]
    n_pool = cache_k.shape[1]
    dq, dv, df = HM * DK, HM * DV, HF * HD
    assert dq % LANES == 0 and dv % LANES == 0 and df % LANES == 0 and LANES % HD == 0
    assert HF + 2 * HM <= GATE_ROWS
    dims = (dq, dv, df, HF, HM, DK, HD)
    c_ig, c_lf = HF, HF + HM
    E = w_router.shape[-1]
    chunk = w_moe_gate.shape[-1]
    M = B * T

    o = np.cumsum([0, dq, dq, dv, dv, HM, HM, df, df, df, HF]).tolist()
    w_m = w_in[:, :, o[0]:o[4]].astype(BF16)
    w_f = w_in[:, :, o[6]:o[9]].astype(BF16)
    scat = np.zeros((N_SPLIT, LANES, HF * LANES), np.float32)
    for t in range(N_SPLIT):
        scat[t, np.arange(HF), np.arange(HF) * LANES + HD + t] = 1.0
    scat = jnp.asarray(scat, BF16)
    n_gate = HF + 2 * HM
    w_gate = jnp.concatenate([w_in[:, :, o[9]:o[10]], w_in[:, :, o[4]:o[6]],
                              jnp.zeros((depth, D, LANES - n_gate), w_in.dtype)], axis=-1).astype(BF16)
    b_gate = jnp.concatenate([b_ff, b_ig, b_fg, jnp.zeros((depth, LANES - n_gate), F32)],
                             axis=-1).astype(F32).reshape(depth, 1, LANES)
    w_out16 = w_out.astype(BF16)
    n_dense, _, d_ff = w_ffn_gate.shape
    assert d_ff % chunk == 0
    nc = d_ff // chunk
    split = lambda w: w.astype(BF16).reshape(n_dense, D, nc, chunk)
    wfgu = jnp.concatenate([split(w_ffn_gate), split(w_ffn_up)], axis=-1).transpose(0, 2, 1, 3)
    wfd = w_ffn_down.astype(BF16).reshape(n_dense, nc, chunk, D)
    wmgu = jnp.concatenate([w_moe_gate.astype(BF16), w_moe_up.astype(BF16)], axis=-1)
    wmd = w_moe_down.astype(BF16)
    w_r = jnp.concatenate([w_router, jnp.zeros(w_router.shape[:2] + (LANES - E,), F32)], axis=-1)
    b_r = jnp.concatenate([b_router, jnp.zeros((b_router.shape[0], LANES - E), F32)],
                          axis=-1).reshape(-1, 1, LANES)
    ckT = jnp.transpose(cache_k, (0, 1, 3, 4, 2)).reshape(depth, n_pool, df, PAGE)
    cvT = jnp.transpose(cache_v, (0, 1, 3, 4, 2)).reshape(depth, n_pool, df, PAGE)
    clfT = jnp.swapaxes(cache_logf, 2, 3)
    sn = state_n.reshape(depth, DB, dq)

    xp = x_prompt.reshape(M, D)
    xs = x_sample.reshape(DB, D)
    pC, pn, pm, plf = [], [], [], []
    kv_acc = ()
    sn_o, sm, sk, sv, slf = [], [], [], [], []
    sC_acc = ()
    for l in range(depth):
        gmix = g_mix_norm[l].reshape(1, D)
        gm = g_out_m[l].reshape(1, dv)
        gf = g_out_f[l].reshape(1, df)
        w_om, w_of = w_out16[l, :dv], w_out16[l, dv:]
        (qm, vm, kT, om, qp, kp, vp, kf, vf, G, Gc, GT, GcT) = _inproj(
            xp, gmix, w_m[l], w_f[l], w_gate[l], b_gate[l], scat, dims, prompt=True, tm=512, seq_len=T,
            layer=l, depth=depth, carried=kv_acc)
        kv_acc = (kf, vf)
        hm, caug, mout = _mlstm_prompt(qm, vm, kT, om, G, Gc, GT, GcT, gm, B=B, T=T, H=HM, DK=DK, DV=DV,
                                       L=min(512, T), c_ig=c_ig, c_lf=c_lf)
        hf = _fox_prompt(qp, kp, vp, gf, B=B, T=T, HF=HF, HD=HD, TB=512, G=HF, RQ=256, KS=2)
        xp = _outproj(xp, hm, hf, w_om, w_of, tm=512)
        pC.append(caug[..., :DV]); pn.append(caug[..., DV]); pm.append(mout[:, :HM, 0])
        plf.append(G[:, :HF].reshape(B, T, HF))
        (qm, km, vm, om, qf, kf, vf, G) = _inproj(
            xs, gmix, w_m[l], w_f[l], w_gate[l], b_gate[l], None, dims, prompt=False, tm=DB, seq_len=1)
        hm, Cn, nn, mn = _mlstm_step(qm, km, vm, om, G, state_C, sn, state_m, l, gm,
                                     NB=8, H=HM, DK=DK, DV=DV, c_ig=c_ig, c_lf=c_lf, carried=sC_acc)
        sC_acc = (Cn,)
        hf = _fox_step(qf, kf, vf, G, ckT, cvT, clfT, page_table, l, gf, HF=HF, HD=HD)
        xs = _outproj(xs, hm, hf, w_om, w_of, tm=DB)
        sn_o.append(nn.reshape(DB, HM, DK)); sm.append(mn[:, :HM])
        sk.append(kf.reshape(DB, 1, HF, HD)); sv.append(vf.reshape(DB, 1, HF, HD))
        slf.append(G[:, :HF].reshape(DB, 1, HF))
        gffn = g_ffn_norm[l].reshape(1, D)
        gfin = g_final.reshape(1, D) if l == depth - 1 else None
        j = l // 2
        if l % 2 == 0:
            args = (wfgu[j], wfd[j], None)
        else:
            args = (wmgu[j], wmd[j], (w_r[j], b_r[j]))
        xp = _ffn(xp, gffn, *args, gfin, tm=512 if l % 2 == 0 else 1024)
        xs = _ffn(xs, gffn, *args, gfin, tm=DB)

    tok_major = lambda a: a.reshape(depth, B, HF, HD, T).transpose(0, 1, 4, 2, 3)
    return (xp.reshape(B, T, D), xs.reshape(DB, 1, D),
            jnp.stack(pC), jnp.stack(pn), jnp.stack(pm),
            tok_major(kv_acc[0]), tok_major(kv_acc[1]), jnp.stack(plf),
            sC_acc[0], jnp.stack(sn_o), jnp.stack(sm),
            jnp.stack(sk), jnp.stack(sv), jnp.stack(slf))
```

```python
import functools

import numpy as np
import jax
import jax.numpy as jnp
from jax import lax
from jax.experimental import pallas as pl
from jax.experimental.pallas import tpu as pltpu

F32 = jnp.float32
BF16 = jnp.bfloat16
EPS = 1e-6
NEG_INIT = -1e30
TOP_K = 2
LANES = 128
GATE_ROWS = 16
VMEM_LIMIT = 56 * 1024 * 1024
LOG2E = 1.4426950408889634
N_SPLIT = 3
FULL_TILE_ROWS = 512


def _cparams(sem):
    return pltpu.CompilerParams(dimension_semantics=sem, vmem_limit_bytes=VMEM_LIMIT)


def _rms(x, g):
    var = jnp.mean(x * x, axis=-1, keepdims=True)
    return x * lax.rsqrt(var + EPS) * g


def _log_sigmoid(x):
    return jnp.minimum(x, 0.0) - jnp.log(1.0 + jnp.exp(-jnp.abs(x)))


def _sigmoid(x):
    return 1.0 / (1.0 + jnp.exp(-x))


def _cumsum_rows(a):
    n = a.shape[0]
    row = lax.broadcasted_iota(jnp.int32, a.shape, 0)
    k = 1
    while k < n:
        a = a + jnp.where(row >= k, pltpu.roll(a, k, 0), 0.0)
        k *= 2
    return a


def _spread_heads(z, hd, tail):
    lane = lax.broadcasted_iota(jnp.int32, (1, LANES), 1)
    blocks = []
    for p in range(z.shape[1] // LANES):
        a = z[:, p * LANES:(p + 1) * LANES]
        blocks.append(jnp.where(lane < hd, a, tail(2 * p)))
        blocks.append(jnp.where(lane < hd, pltpu.roll(a, hd, 1), tail(2 * p + 1)))
    return jnp.concatenate(blocks, axis=1)


def _inproj_kernel(x_ref, g_ref, wm_ref, wf_ref, wg_ref, bg_ref, *refs, dq, dv, df, hf, hm, hd, kscale, qscale,
                   prompt, tiles_per_seq, n_carried):
    if prompt:
        (scat_ref, qm_ref, vm_ref, kT_ref, om_ref, qp_ref, kp_ref, vp_ref, kf_ref, vf_ref,
         G_ref, Gc_ref, GT_ref, GcT_ref, carry_ref) = refs[:1] + refs[1 + n_carried:]
    else:
        (qm_ref, km_ref, vm_ref, om_ref, qf_ref, kf_ref, vf_ref, G_ref) = refs
    hn = _rms(x_ref[...], g_ref[...]).astype(BF16)

    def proj(w_ref, off, width):
        return jnp.dot(hn, w_ref[:, off:off + width], preferred_element_type=F32)

    zg = jnp.dot(hn, wg_ref[...], preferred_element_type=F32) + bg_ref[...]
    col = lax.broadcasted_iota(jnp.int32, zg.shape, 1)
    is_log = (col < hf) | ((col >= hf + hm) & (col < hf + 2 * hm))
    is_lin = (col >= hf) & (col < hf + hm)
    G = jnp.where(is_log, _log_sigmoid(zg), jnp.where(is_lin, zg, 0.0))
    G_ref[...] = G

    qm_ref[...] = proj(wm_ref, 0, dq).astype(qm_ref.dtype)
    km = proj(wm_ref, dq, dq) * kscale
    if prompt:
        kT_ref[...] = jnp.transpose(km).astype(kT_ref.dtype)
    else:
        km_ref[...] = km
    vm_ref[...] = proj(wm_ref, 2 * dq, dv).astype(vm_ref.dtype)
    om_ref[...] = proj(wm_ref, 2 * dq + dv, dv).astype(om_ref.dtype)
    if not prompt:
        qf_ref[...] = proj(wf_ref, 0, df) * qscale
        kf_ref[...] = proj(wf_ref, df, df)
        vf_ref[...] = proj(wf_ref, 2 * df, df)
        return

    i = pl.program_id(0)

    @pl.when(i % tiles_per_seq == 0)
    def _():
        carry_ref[...] = jnp.zeros_like(carry_ref)

    Gc = _cumsum_rows(G) + carry_ref[0:1, :]
    carry_ref[0:1, :] = Gc[-1:, :]
    Gc_ref[...] = Gc
    GT_ref[...] = jnp.transpose(G)[:GATE_ROWS, :]
    GcT_ref[...] = jnp.transpose(Gc)[:GATE_ROWS, :]

    lane = lax.broadcasted_iota(jnp.int32, (1, LANES), 1)
    zq = proj(wf_ref, 0, df) * (qscale * LOG2E)
    zk = proj(wf_ref, df, df)
    zv = proj(wf_ref, 2 * df, df)
    kf_ref[...] = jnp.transpose(zk)
    vf_ref[...] = jnp.transpose(zv)
    rem = Gc * (-LOG2E)
    bias = jnp.zeros((zk.shape[0], hf * LANES), F32)
    for t in range(N_SPLIT):
        piece = rem.astype(BF16)
        rem = rem - piece.astype(F32)
        bias = bias + jnp.dot(piece, scat_ref[t], preferred_element_type=F32)
    q_tail = jnp.where((lane >= hd) & (lane < hd + N_SPLIT), 1.0, 0.0)
    v_tail = jnp.where(lane == hd, 1.0, 0.0)
    qp_ref[...] = _spread_heads(zq, hd, lambda h: q_tail).astype(BF16)
    kp_ref[...] = _spread_heads(zk, hd, lambda h: bias[:, h * LANES:(h + 1) * LANES]).astype(BF16)
    vp_ref[...] = _spread_heads(zv, hd, lambda h: v_tail).astype(BF16)


def _inproj(x, g, w_m, w_f, w_gate, b_gate, scat, dims, *, prompt, tm, seq_len, layer=0, depth=1, carried=()):
    M, D = x.shape
    dq, dv, df, hf, hm, dk, hd = dims
    dfp = hf * LANES
    tm = min(tm, M)
    assert M % tm == 0
    if prompt:
        assert seq_len % tm == 0
    row = lambda i: (i, 0)
    colb = lambda i: (0, i)
    const = lambda i: (0, 0)
    tps = max(seq_len // tm, 1)
    seqb = lambda i: (layer, i // tps, 0, i % tps)
    aliases = {}
    args = [x, g, w_m, w_f, w_gate, b_gate]
    in_specs = [pl.BlockSpec((tm, D), row), pl.BlockSpec((1, D), const),
                pl.BlockSpec(w_m.shape, const), pl.BlockSpec(w_f.shape, const),
                pl.BlockSpec((D, LANES), const), pl.BlockSpec((1, LANES), const)]
    if prompt:
        args.append(scat)
        in_specs.append(pl.BlockSpec(scat.shape, lambda i: (0, 0, 0)))
        kv_out = 7
        for n, buf in enumerate(carried):
            aliases[len(args)] = kv_out + n
            args.append(buf)
            in_specs.append(pl.BlockSpec(memory_space=pl.ANY))
        out_shape = [jax.ShapeDtypeStruct((M, dq), BF16), jax.ShapeDtypeStruct((M, dv), BF16),
                     jax.ShapeDtypeStruct((dq, M), BF16), jax.ShapeDtypeStruct((M, dv), BF16),
                     jax.ShapeDtypeStruct((M, dfp), BF16), jax.ShapeDtypeStruct((M, dfp), BF16),
                     jax.ShapeDtypeStruct((M, dfp), BF16),
                     jax.ShapeDtypeStruct((depth, M // seq_len, df, seq_len), F32),
                     jax.ShapeDtypeStruct((depth, M // seq_len, df, seq_len), F32),
                     jax.ShapeDtypeStruct((M, LANES), F32), jax.ShapeDtypeStruct((M, LANES), F32),
                     jax.ShapeDtypeStruct((GATE_ROWS, M), F32), jax.ShapeDtypeStruct((GATE_ROWS, M), F32)]
        out_specs = [pl.BlockSpec((tm, dq), row), pl.BlockSpec((tm, dv), row),
                     pl.BlockSpec((dq, tm), colb), pl.BlockSpec((tm, dv), row),
                     pl.BlockSpec((tm, dfp), row), pl.BlockSpec((tm, dfp), row), pl.BlockSpec((tm, dfp), row),
                     pl.BlockSpec((None, None, df, tm), seqb), pl.BlockSpec((None, None, df, tm), seqb),
                     pl.BlockSpec((tm, LANES), row), pl.BlockSpec((tm, LANES), row),
                     pl.BlockSpec((GATE_ROWS, tm), colb), pl.BlockSpec((GATE_ROWS, tm), colb)]
        scratch = [pltpu.VMEM((8, LANES), F32)]
    else:
        out_shape = ([jax.ShapeDtypeStruct((M, dq), F32)] * 2 + [jax.ShapeDtypeStruct((M, dv), F32)] * 2
                     + [jax.ShapeDtypeStruct((M, df), F32)] * 3 + [jax.ShapeDtypeStruct((M, LANES), F32)])
        out_specs = ([pl.BlockSpec((tm, dq), row)] * 2 + [pl.BlockSpec((tm, dv), row)] * 2
                     + [pl.BlockSpec((tm, df), row)] * 3 + [pl.BlockSpec((tm, LANES), row)])
        scratch = []
    kern = functools.partial(_inproj_kernel, dq=dq, dv=dv, df=df, hf=hf, hm=hm, hd=hd,
                             kscale=float(dk) ** -0.5, qscale=float(hd) ** -0.5,
                             prompt=prompt, tiles_per_seq=max(seq_len // tm, 1), n_carried=len(carried))
    return pl.pallas_call(
        kern, grid=(M // tm,), in_specs=in_specs, out_specs=out_specs, out_shape=out_shape,
        input_output_aliases=aliases, scratch_shapes=scratch, compiler_params=_cparams(("arbitrary",)),
        name="inproj_prompt" if prompt else "inproj_sample")(*args)


def _mlstm_prompt_kernel(q_ref, v_ref, kT_ref, o_ref, G_ref, Gc_ref, GT_ref, GcT_ref, gm_ref,
                         hm_ref, caug_ref, mout_ref, bprev_ref, m_ref, *, H, DK, DV, L, c_ig, c_lf):
    c = pl.program_id(1)

    @pl.when(c == 0)
    def _():
        caug_ref[...] = jnp.zeros_like(caug_ref)
        m_ref[...] = jnp.full_like(m_ref, NEG_INIT)
        bprev_ref[...] = jnp.zeros_like(bprev_ref)

    row = lax.broadcasted_iota(jnp.int32, (L, L), 0)
    colL = lax.broadcasted_iota(jnp.int32, (L, L), 1)
    causal = row >= colL
    lane = lax.broadcasted_iota(jnp.int32, (L, LANES), 1)
    ones_col = jnp.where(lane == 0, 1.0, 0.0)
    G = G_ref[...]
    Gc = Gc_ref[...]
    GT = GT_ref[...]
    GcT = GcT_ref[...]
    for h in range(H):
        bp = bprev_ref[0:1, c_lf + h:c_lf + h + 1]
        b_col = Gc[:, c_lf + h:c_lf + h + 1] - bp
        b_row = GcT[c_lf + h:c_lf + h + 1, :] - bp
        i_row = GT[c_ig + h:c_ig + h + 1, :]
        i_col = G[:, c_ig + h:c_ig + h + 1]
        m_prev = m_ref[h:h + 1, 0:1]
        dm = jnp.where(causal, b_col - b_row + i_row, -jnp.inf)
        inter = b_col + m_prev
        m_row = jnp.maximum(inter, jnp.max(dm, axis=-1, keepdims=True))
        w_intra = jnp.exp(dm - m_row)
        w_inter = jnp.exp(inter - m_row)
        qh = q_ref[:, h * DK:(h + 1) * DK]
        kTh = kT_ref[h * DK:(h + 1) * DK, :]
        vh = v_ref[:, h * DV:(h + 1) * DV].astype(F32)
        vaug = jnp.concatenate([vh, ones_col], axis=1)
        s = jnp.dot(qh, kTh, preferred_element_type=F32) * w_intra
        intra = jnp.dot(s.astype(BF16), vaug.astype(BF16), preferred_element_type=F32)
        cst = caug_ref[h]
        inter_qc = jnp.dot(qh, cst.astype(BF16), preferred_element_type=F32)
        r = w_inter * inter_qc + intra
        num = r[:, :DV]
        den = r[:, DV:DV + 1]
        hval = num / jnp.maximum(jnp.abs(den), jnp.exp(-m_row))
        m_new = m_row[L - 1:L, :]
        decay = jnp.exp(inter[L - 1:L, :] - m_new)
        w_k = jnp.exp(b_col[L - 1:L, :] - b_col + i_col - m_new)
        upd = jnp.dot(kTh, (vaug * w_k).astype(BF16), preferred_element_type=F32)
        caug_ref[h] = decay * cst + upd
        m_ref[h:h + 1, :] = jnp.broadcast_to(m_new, (1, LANES))
        gate = _sigmoid(o_ref[:, h * DV:(h + 1) * DV].astype(F32))
        hm_ref[:, h * DV:(h + 1) * DV] = (_rms(hval, gm_ref[:, h * DV:(h + 1) * DV]) * gate).astype(hm_ref.dtype)
    bprev_ref[0:1, :] = Gc[L - 1:L, :]

    @pl.when(c == pl.num_programs(1) - 1)
    def _():
        mout_ref[...] = m_ref[...]


def _mlstm_prompt(qm, vm, kT, om, G, Gc, GT, GcT, g_m, *, B, T, H, DK, DV, L, c_ig, c_lf):
    M = B * T
    NC = T // L
    assert T % L == 0
    rowb = lambda b, c: (b * NC + c, 0)
    colb = lambda b, c: (0, b * NC + c)
    kern = functools.partial(_mlstm_prompt_kernel, H=H, DK=DK, DV=DV, L=L, c_ig=c_ig, c_lf=c_lf)
    return pl.pallas_call(
        kern, grid=(B, NC),
        in_specs=[pl.BlockSpec((L, H * DK), rowb), pl.BlockSpec((L, H * DV), rowb),
                  pl.BlockSpec((H * DK, L), colb), pl.BlockSpec((L, H * DV), rowb),
                  pl.BlockSpec((L, LANES), rowb), pl.BlockSpec((L, LANES), rowb),
                  pl.BlockSpec((GATE_ROWS, L), colb), pl.BlockSpec((GATE_ROWS, L), colb),
                  pl.BlockSpec((1, H * DV), lambda b, c: (0, 0))],
        out_specs=[pl.BlockSpec((L, H * DV), rowb),
                   pl.BlockSpec((None, H, DK, DV + LANES), lambda b, c: (b, 0, 0, 0)),
                   pl.BlockSpec((None, 8, LANES), lambda b, c: (b, 0, 0))],
        out_shape=[jax.ShapeDtypeStruct((M, H * DV), BF16),
                   jax.ShapeDtypeStruct((B, H, DK, DV + LANES), F32),
                   jax.ShapeDtypeStruct((B, 8, LANES), F32)],
        scratch_shapes=[pltpu.VMEM((8, LANES), F32), pltpu.VMEM((8, LANES), F32)],
        compiler_params=_cparams(("parallel", "arbitrary")),
        name="mlstm_prompt")(qm, vm, kT, om, G, Gc, GT, GcT, g_m)


def _fox_prompt_kernel(ii_ref, jj_ref, q_ref, k_ref, v_ref, gf_ref, o_ref, m_sc, acc_sc, *, TB, HD, G, RQ, KS):
    p = pl.program_id(2)
    i = ii_ref[p]
    j = jj_ref[p]
    FULL, MASKED = 0, 1

    @pl.when(j == 0)
    def _():
        m_sc[...] = jnp.full_like(m_sc, -jnp.inf)
        acc_sc[...] = jnp.zeros_like(acc_sc)

    def update(modes):
        items = [(c, g, r) for c in range(len(modes)) for g in range(G) for r in range(TB // RQ)]

        def qk(c, g, r):
            q = q_ref[r * RQ:(r + 1) * RQ, g * LANES:(g + 1) * LANES]
            k = k_ref[c * TB:(c + 1) * TB, g * LANES:(g + 1) * LANES]
            return lax.dot_general(q, k, (((1,), (1,)), ((), ())), preferred_element_type=F32)

        s_next = qk(*items[0])
        for n, (c, g, r) in enumerate(items):
            s = s_next
            if n + 1 < len(items):
                s_next = qk(*items[n + 1])
            rows = slice(r * RQ, (r + 1) * RQ)
            v = v_ref[c * TB:(c + 1) * TB, g * LANES:(g + 1) * LANES]
            if modes[c] == MASKED:
                rr = lax.broadcasted_iota(jnp.int32, (RQ, TB), 0) + r * RQ
                cc = lax.broadcasted_iota(jnp.int32, (RQ, TB), 1)
                s = jnp.where(rr >= cc, s, -jnp.inf)
            m_old = m_sc[g, rows, :]
            m_new = jnp.maximum(m_old, jnp.max(s, axis=1, keepdims=True))
            alpha = jnp.exp2(m_old - m_new)
            pr = jnp.concatenate([jnp.exp2(s[:, cl * LANES:(cl + 1) * LANES] - m_new).astype(BF16)
                                  for cl in range(TB // LANES)], axis=1)
            acc_sc[g, rows, :] = alpha * acc_sc[g, rows, :] + jnp.dot(pr, v, preferred_element_type=F32)
            m_sc[g, rows, :] = m_new

    last = i // KS

    @pl.when(j < last)
    def _():
        update([FULL] * KS)

    for d in range(KS):
        @pl.when((j == last) & (i % KS == d))
        def _():
            update([FULL] * d + [MASKED])

    @pl.when(j == last)
    def _():
        lane = lax.broadcasted_iota(jnp.int32, (1, LANES), 1)
        lrow = lax.broadcasted_iota(jnp.int32, (LANES, LANES), 0)
        wmat = jnp.where(lrow < HD, 1.0 / HD, jnp.where(lrow == HD, EPS, 0.0))
        for gp in range(G // 2):
            outs = []
            for g in (2 * gp, 2 * gp + 1):
                a = acc_sc[g]
                outs.append(a * lax.rsqrt(jnp.dot(a * a, wmat, preferred_element_type=F32)))
            out = jnp.where(lane < HD, outs[0], pltpu.roll(outs[1], HD, 1))
            o_ref[:, gp * LANES:(gp + 1) * LANES] = (out * gf_ref[:, gp * LANES:(gp + 1) * LANES]).astype(o_ref.dtype)


def _fox_prompt(qp, kp, vp, g_f, *, B, T, HF, HD, TB, G, RQ, KS):
    M = B * T
    TB = min(TB, T)
    RQ = min(RQ, TB)
    KS = min(KS, T // TB)
    TK = KS * TB
    assert T % TK == 0 and TB % RQ == 0 and HF % G == 0 and G % 2 == 0 and 2 * HD == LANES
    nq = T // TB
    nkv = T // TK
    ii = np.concatenate([np.full(i // KS + 1, i, np.int32) for i in range(nq)])
    jj = np.concatenate([np.arange(i // KS + 1, dtype=np.int32) for i in range(nq)])
    qmap = lambda b, hg, p, ii, jj: (b * nq + ii[p], hg)
    kmap = lambda b, hg, p, ii, jj: (b * nkv + jj[p], hg)
    kern = functools.partial(_fox_prompt_kernel, TB=TB, HD=HD, G=G, RQ=RQ, KS=KS)
    grid_spec = pltpu.PrefetchScalarGridSpec(
        num_scalar_prefetch=2, grid=(B, HF // G, len(ii)),
        in_specs=[pl.BlockSpec((TB, G * LANES), qmap), pl.BlockSpec((TK, G * LANES), kmap),
                  pl.BlockSpec((TK, G * LANES), kmap),
                  pl.BlockSpec((1, G * HD), lambda b, hg, p, ii, jj: (0, hg))],
        out_specs=pl.BlockSpec((TB, G * HD), qmap),
        scratch_shapes=[pltpu.VMEM((G, TB, LANES), F32)] * 2)
    return pl.pallas_call(
        kern, grid_spec=grid_spec, out_shape=jax.ShapeDtypeStruct((M, HF * HD), BF16),
        compiler_params=_cparams(("parallel", "parallel", "arbitrary")),
        name="fox_prompt")(jnp.asarray(ii), jnp.asarray(jj), qp, kp, vp, g_f)


def _outproj_kernel(x_ref, hm_ref, hf_ref, wm_ref, wf_ref, o_ref):
    y = jnp.dot(hm_ref[...].astype(BF16), wm_ref[...], preferred_element_type=F32)
    y = y + jnp.dot(hf_ref[...].astype(BF16), wf_ref[...], preferred_element_type=F32)
    o_ref[...] = x_ref[...] + y


def _outproj(x, hm, hf, w_m, w_f, *, tm):
    M, D = x.shape
    tm = min(tm, M)
    assert M % tm == 0
    row = lambda i: (i, 0)
    const = lambda i: (0, 0)
    return pl.pallas_call(
        _outproj_kernel, grid=(M // tm,),
        in_specs=[pl.BlockSpec((tm, D), row), pl.BlockSpec((tm, hm.shape[1]), row),
                  pl.BlockSpec((tm, hf.shape[1]), row),
                  pl.BlockSpec(w_m.shape, const), pl.BlockSpec(w_f.shape, const)],
        out_specs=pl.BlockSpec((tm, D), row), out_shape=jax.ShapeDtypeStruct((M, D), F32),
        compiler_params=_cparams(("parallel",)), name="outproj")(x, hm, hf, w_m, w_f)


def _ffn_kernel(x_ref, g_ref, wgu_ref, wd_ref, *refs, moe, n_experts, final, cap):
    refs = list(refs)
    if moe:
        wr_ref, br_ref = refs[:2]
        refs = refs[2:]
    if final:
        gfin_ref = refs[0]
        refs = refs[1:]
    if moe:
        o_ref, hn_sc, acc_sc, comb_sc, rk_sc, rkT_sc, cnt_sc = refs
    else:
        o_ref, hn_sc, acc_sc = refs
    e = pl.program_id(1)
    tm = hn_sc.shape[0]
    chunk = wd_ref.shape[0]

    def swiglu(h):
        cut = min(chunk, -(-chunk // 512) * 256)
        y = None
        for c0, c1 in ((0, cut), (cut, chunk)):
            if c0 == c1:
                continue
            a = jnp.dot(h, wgu_ref[:, c0:c1], preferred_element_type=F32)
            u = jnp.dot(h, wgu_ref[:, chunk + c0:chunk + c1], preferred_element_type=F32)
            act = (a * _sigmoid(a) * u).astype(BF16)
            part = jnp.dot(act, wd_ref[c0:c1, :], preferred_element_type=F32)
            y = part if y is None else y + part
        return y

    @pl.when(e == 0)
    def _():
        hn = _rms(x_ref[...], g_ref[...])
        hn_sc[...] = hn.astype(BF16)
        acc_sc[...] = jnp.zeros_like(acc_sc)
        if moe:
            lane = lax.broadcasted_iota(jnp.int32, (tm, LANES), 1)
            lanef = lane.astype(F32)
            hn_hi = hn.astype(BF16)
            hn_lo = (hn - hn_hi.astype(F32)).astype(BF16)
            wr = wr_ref[...]
            wr_hi = wr.astype(BF16)
            wr_lo = (wr - wr_hi.astype(F32)).astype(BF16)
            lg = (jnp.dot(hn_hi, wr_hi, preferred_element_type=F32) + jnp.dot(hn_hi, wr_lo, preferred_element_type=F32)
                  + jnp.dot(hn_lo, wr_hi, preferred_element_type=F32)) + br_ref[...]
            lg = jnp.where(lane < n_experts, lg, -jnp.inf)
            m1 = jnp.max(lg, axis=1, keepdims=True)
            i1 = jnp.min(jnp.where(lg == m1, lanef, float(LANES)), axis=1, keepdims=True)
            lg2 = jnp.where(lanef == i1, -jnp.inf, lg)
            m2 = jnp.max(lg2, axis=1, keepdims=True)
            i2 = jnp.min(jnp.where(lg2 == m2, lanef, float(LANES)), axis=1, keepdims=True)
            e2 = jnp.exp(m2 - m1)
            g1 = 1.0 / (1.0 + e2)
            g2 = e2 / (1.0 + e2)
            comb_sc[...] = jnp.where(lanef == i1, g1, 0.0) + jnp.where(lanef == i2, g2, 0.0)
            routed = jnp.where((lanef == i1) | (lanef == i2), 1.0, 0.0)
            incl = _cumsum_rows(routed)
            rk = jnp.where(routed > 0.0, incl - 1.0, -1.0)
            rk_sc[...] = rk
            rkT_sc[...] = jnp.transpose(rk)
            cnt_sc[...] = jnp.broadcast_to(incl[tm - 1:tm, :], cnt_sc.shape)

    if not moe:
        acc_sc[...] += swiglu(hn_sc[...])
    else:
        lane = lax.broadcasted_iota(jnp.int32, (tm, LANES), 1)
        ce = jnp.sum(jnp.where(lane == e, comb_sc[...], 0.0), axis=1, keepdims=True)
        lane_row = lax.broadcasted_iota(jnp.int32, (1, LANES), 1)
        count = jnp.sum(jnp.where(lane_row == e, cnt_sc[0:1, :], 0.0))

        @pl.when(count > cap)
        def _():
            rb = min(tm, FULL_TILE_ROWS)
            for r in range(tm // rb):
                rows = slice(r * rb, (r + 1) * rb)
                acc_sc[rows, :] += ce[rows, :] * swiglu(hn_sc[rows, :])

        @pl.when(count <= cap)
        def _():
            rk_row = rkT_sc[pl.ds(e, 1), :]
            slot = lax.broadcasted_iota(jnp.int32, (cap, tm), 0).astype(F32)
            pick = jnp.where(rk_row == slot, 1.0, 0.0).astype(BF16)
            xs = jnp.dot(pick, hn_sc[...], preferred_element_type=F32).astype(BF16)
            y = swiglu(xs).astype(BF16)
            rk_col = jnp.sum(jnp.where(lane == e, rk_sc[...], 0.0), axis=1, keepdims=True)
            slot_l = lax.broadcasted_iota(jnp.int32, (tm, cap), 1).astype(F32)
            place = jnp.where(rk_col == slot_l, 1.0, 0.0).astype(BF16)
            acc_sc[...] += ce * jnp.dot(place, y, preferred_element_type=F32)

    @pl.when(e == pl.num_programs(1) - 1)
    def _():
        xo = x_ref[...] + acc_sc[...]
        if final:
            xo = _rms(xo, gfin_ref[...])
        o_ref[...] = xo


def _expert_capacity(tm, n_experts):
    mean = tm * TOP_K / n_experts
    return min(tm, max(64, int(-(-1.25 * mean // 64)) * 64))


def _ffn(x, g, wgu, wd, router, g_final, *, tm):
    M, D = x.shape
    tm = min(tm, M)
    assert M % tm == 0
    moe = router is not None
    final = g_final is not None
    E, chunk = wd.shape[0], wd.shape[1]
    row = lambda i, e: (i, 0)
    const = lambda i, e: (0, 0)
    args = [x, g, wgu, wd]
    in_specs = [pl.BlockSpec((tm, D), row), pl.BlockSpec((1, D), const),
                pl.BlockSpec((None, D, 2 * chunk), lambda i, e: (e, 0, 0)),
                pl.BlockSpec((None, chunk, D), lambda i, e: (e, 0, 0))]
    scratch = [pltpu.VMEM((tm, D), BF16), pltpu.VMEM((tm, D), F32)]
    if moe:
        args += list(router)
        in_specs += [pl.BlockSpec((D, LANES), const), pl.BlockSpec((1, LANES), const)]
        scratch += [pltpu.VMEM((tm, LANES), F32), pltpu.VMEM((tm, LANES), F32), pltpu.VMEM((LANES, tm), F32),
                    pltpu.VMEM((8, LANES), F32)]
    if final:
        args.append(g_final)
        in_specs.append(pl.BlockSpec((1, D), const))
    kern = functools.partial(_ffn_kernel, moe=moe, n_experts=E, final=final, cap=_expert_capacity(tm, E))
    return pl.pallas_call(
        kern, grid=(M // tm, E), in_specs=in_specs,
        out_specs=pl.BlockSpec((tm, D), row), out_shape=jax.ShapeDtypeStruct((M, D), F32),
        scratch_shapes=scratch, compiler_params=_cparams(("parallel", "arbitrary")),
        name="ffn_moe" if moe else "ffn_dense")(*args)


def _mlstm_step_kernel(q_ref, k_ref, v_ref, o_ref, kc_ref, G_ref, C_ref, n_ref, m_ref, gm_ref, *refs,
                       NB, H, DK, DV, c_ig, c_lf):
    hm_ref, Cn_ref, nn_ref, mn_ref = refs[-4:]
    sub = lax.broadcasted_iota(jnp.int32, (NB, DV), 0)
    lane = lax.broadcasted_iota(jnp.int32, (NB, LANES), 1)
    m_out = jnp.zeros((NB, LANES), F32)
    for h in range(H):
        q = q_ref[:, h * DK:(h + 1) * DK]
        k = k_ref[:, h * DK:(h + 1) * DK]
        v = v_ref[:, h * DV:(h + 1) * DV]
        n = n_ref[:, h * DK:(h + 1) * DK]
        ig = G_ref[:, c_ig + h:c_ig + h + 1]
        lf = G_ref[:, c_lf + h:c_lf + h + 1]
        m0 = m_ref[:, h:h + 1]
        inter = lf + m0
        m_new = jnp.maximum(inter, ig)
        w_inter = jnp.exp(inter - m_new)
        w_intra = jnp.exp(ig - m_new)
        s = jnp.sum(q * k, axis=1, keepdims=True) * w_intra
        qn = jnp.sum(q * n, axis=1, keepdims=True)
        vw = v * w_intra
        kc = kc_ref[h * DK:(h + 1) * DK, :]
        qC = jnp.zeros((NB, DV), F32)
        for bb in range(NB):
            C = C_ref[bb, h]
            qC = jnp.where(sub == bb, jnp.dot(q, C, preferred_element_type=F32), qC)
            vsel = jnp.concatenate([jnp.where(sub == bb, vw, 0.0), jnp.zeros((LANES - NB, DV), F32)], axis=0)
            Cn_ref[bb, h] = w_inter[bb:bb + 1, :] * C + jnp.dot(kc, vsel, preferred_element_type=F32)
        num = w_inter * qC + s * v
        den = w_inter * qn + s
        hval = num / jnp.maximum(jnp.abs(den), jnp.exp(-m_new))
        nn_ref[:, h * DK:(h + 1) * DK] = w_inter * n + w_intra * k
        m_out = jnp.where(lane == h, m_new, m_out)
        gate = _sigmoid(o_ref[:, h * DV:(h + 1) * DV])
        hm_ref[:, h * DV:(h + 1) * DV] = _rms(hval, gm_ref[:, h * DV:(h + 1) * DV]) * gate
    mn_ref[...] = m_out


def _mlstm_step(q, k, v, o, G, state_C, state_n, state_m, layer, g_m, *, NB, H, DK, DV, c_ig, c_lf, carried=()):
    DB = q.shape[0]
    depth = state_C.shape[0]
    NB = min(NB, DB)
    assert DB % NB == 0 and NB <= LANES
    S = DB // NB
    kc = jnp.pad(k.reshape(S, NB, H * DK).transpose(0, 2, 1), ((0, 0), (0, 0), (0, LANES - NB)))
    row = lambda s: (s, 0)
    args = [q, k, v, o, kc, G, state_C, state_n, state_m, g_m]
    in_specs = [pl.BlockSpec((NB, H * DK), row), pl.BlockSpec((NB, H * DK), row),
                pl.BlockSpec((NB, H * DV), row), pl.BlockSpec((NB, H * DV), row),
                pl.BlockSpec((None, H * DK, LANES), lambda s: (s, 0, 0)),
                pl.BlockSpec((NB, LANES), row),
                pl.BlockSpec((None, NB, H, DK, DV), lambda s: (layer, s, 0, 0, 0)),
                pl.BlockSpec((None, NB, H * DK), lambda s: (layer, s, 0)),
                pl.BlockSpec((None, NB, H), lambda s: (layer, s, 0)),
                pl.BlockSpec((1, H * DV), lambda s: (0, 0))]
    aliases = {}
    for buf in carried:
        aliases[len(args)] = 1
        args.append(buf)
        in_specs.append(pl.BlockSpec(memory_space=pl.ANY))
    kern = functools.partial(_mlstm_step_kernel, NB=NB, H=H, DK=DK, DV=DV, c_ig=c_ig, c_lf=c_lf)
    return pl.pallas_call(
        kern, grid=(S,), in_specs=in_specs,
        out_specs=[pl.BlockSpec((NB, H * DV), row),
                   pl.BlockSpec((None, NB, H, DK, DV), lambda s: (layer, s, 0, 0, 0)),
                   pl.BlockSpec((NB, H * DK), row), pl.BlockSpec((NB, LANES), row)],
        out_shape=[jax.ShapeDtypeStruct((DB, H * DV), F32), jax.ShapeDtypeStruct((depth, DB, H, DK, DV), F32),
                   jax.ShapeDtypeStruct((DB, H * DK), F32), jax.ShapeDtypeStruct((DB, LANES), F32)],
        input_output_aliases=aliases, compiler_params=_cparams(("parallel",)),
        name="mlstm_step")(*args)


def _fox_step_kernel(pt_ref, q_ref, kc_ref, vc_ref, G_ref, gf_ref, *refs, NP, HF, HD, PAGE):
    k_refs = refs[:NP]
    v_refs = refs[NP:2 * NP]
    lf_ref = refs[2 * NP]
    o_ref = refs[2 * NP + 1]
    b = pl.program_id(0)
    DF = HF * HD
    sub = lax.broadcasted_iota(jnp.int32, (HF, DF), 0)
    lane = lax.broadcasted_iota(jnp.int32, (HF, DF), 1)
    diag = lane // HD == sub
    q = q_ref[...]
    qbd = jnp.where(diag, jnp.broadcast_to(q, (HF, DF)), 0.0)
    qbd16 = qbd.astype(BF16)
    sub_g = lax.broadcasted_iota(jnp.int32, (HF, LANES), 0)
    lane_g = lax.broadcasted_iota(jnp.int32, (HF, LANES), 1)
    carry = jnp.sum(jnp.where(lane_g == sub_g, jnp.broadcast_to(G_ref[...], (HF, LANES)), 0.0),
                    axis=1, keepdims=True)
    ti = lax.broadcasted_iota(jnp.int32, (PAGE, PAGE), 0)
    tj = lax.broadcasted_iota(jnp.int32, (PAGE, PAGE), 1)
    later = jnp.where(ti > tj, 1.0, 0.0)
    s_cur = jnp.sum(qbd * kc_ref[...], axis=1, keepdims=True)
    scores = [None] * NP
    mx = s_cur
    for p in range(NP - 1, -1, -1):
        lfT = lf_ref[pt_ref[b * NP + p]]
        suf = jnp.dot(lfT, later, preferred_element_type=F32, precision=lax.Precision.HIGHEST) + carry
        carry = carry + jnp.sum(lfT, axis=1, keepdims=True)
        kT = k_refs[p][...].astype(BF16)
        s = jnp.dot(qbd16, kT, preferred_element_type=F32) + suf
        scores[p] = s
        mx = jnp.maximum(mx, jnp.max(s, axis=1, keepdims=True))
    p_cur = jnp.exp(s_cur - mx)
    l = p_cur
    acc = p_cur * vc_ref[...]
    for p in range(NP):
        pr = jnp.exp(scores[p] - mx)
        l = l + jnp.sum(pr, axis=1, keepdims=True)
        vT = v_refs[p][...].astype(BF16)
        acc = acc + lax.dot_general(pr.astype(BF16), vT, (((1,), (1,)), ((), ())), preferred_element_type=F32)
    o = acc / l
    ms = jnp.sum(jnp.where(diag, o * o, 0.0), axis=1, keepdims=True) * (1.0 / HD)
    on = jnp.where(diag, o * lax.rsqrt(ms + EPS), 0.0)
    o_ref[...] = jnp.sum(on, axis=0, keepdims=True) * gf_ref[...]


def _fox_step(q, k_cur, v_cur, lf_cur, cache_kT, cache_vT, cache_lfT, page_table, layer, g_f, *, HF, HD):
    DB = q.shape[0]
    NP = page_table.shape[1]
    PAGE = cache_kT.shape[3]
    DF = HF * HD
    pt = page_table.reshape(-1)
    r3 = lambda a: a.reshape(DB, 1, a.shape[-1])
    cur = lambda b, pt: (b, 0, 0)

    def page_map(p):
        return lambda b, pt: (layer, pt[b * NP + p], 0, 0)

    kv_specs = [pl.BlockSpec((None, None, DF, PAGE), page_map(p)) for p in range(NP)]
    lf_specs = [pl.BlockSpec((None,) + cache_lfT.shape[1:], lambda b, pt: (layer, 0, 0, 0))]
    grid_spec = pltpu.PrefetchScalarGridSpec(
        num_scalar_prefetch=1, grid=(DB,),
        in_specs=[pl.BlockSpec((None, 1, DF), cur)] * 3 + [pl.BlockSpec((None, 1, LANES), cur),
                                                           pl.BlockSpec((1, DF), lambda b, pt: (0, 0))]
                 + kv_specs + kv_specs + lf_specs,
        out_specs=pl.BlockSpec((None, 1, DF), cur))
    kern = functools.partial(_fox_step_kernel, NP=NP, HF=HF, HD=HD, PAGE=PAGE)
    out = pl.pallas_call(
        kern, grid_spec=grid_spec, out_shape=jax.ShapeDtypeStruct((DB, 1, DF), F32),
        compiler_params=_cparams(("arbitrary",)),
        name="fox_step")(pt, r3(q), r3(k_cur), r3(v_cur), r3(lf_cur), g_f,
                         *([cache_kT] * NP), *([cache_vT] * NP), cache_lfT)
    return out.reshape(DB, DF)


def kernel(x_prompt, x_sample, state_C, state_n, state_m, cache_k, cache_v, cache_logf, page_table, g_mix_norm, w_in, b_ig, b_fg, b_ff, g_out_m, g_out_f, w_out, g_ffn_norm, w_ffn_gate, w_ffn_up, w_ffn_down, w_router, b_router, w_moe_gate, w_moe_up, w_moe_down, g_final):
    B, T, D = x_prompt.shape
    DB = x_sample.shape[0]
    assert x_sample.shape[1] == 1
    depth = w_in.shape[0]
    HM, DK, DV = state_C.shape[2:]
    PAGE, HF, HD = cache_k.shape[2:]
    n_pool = cache_k.shape[1]
    dq, dv, df = HM * DK, HM * DV, HF * HD
    assert dq % LANES == 0 and dv % LANES == 0 and df % LANES == 0 and LANES % HD == 0
    assert HF + 2 * HM <= GATE_ROWS
    dims = (dq, dv, df, HF, HM, DK, HD)
    c_ig, c_lf = HF, HF + HM
    E = w_router.shape[-1]
    chunk = w_moe_gate.shape[-1]
    M = B * T

    o = np.cumsum([0, dq, dq, dv, dv, HM, HM, df, df, df, HF]).tolist()
    w_m = w_in[:, :, o[0]:o[4]].astype(BF16)
    w_f = w_in[:, :, o[6]:o[9]].astype(BF16)
    scat = np.zeros((N_SPLIT, LANES, HF * LANES), np.float32)
    for t in range(N_SPLIT):
        scat[t, np.arange(HF), np.arange(HF) * LANES + HD + t] = 1.0
    scat = jnp.asarray(scat, BF16)
    n_gate = HF + 2 * HM
    w_gate = jnp.concatenate([w_in[:, :, o[9]:o[10]], w_in[:, :, o[4]:o[6]],
                              jnp.zeros((depth, D, LANES - n_gate), w_in.dtype)], axis=-1).astype(BF16)
    b_gate = jnp.concatenate([b_ff, b_ig, b_fg, jnp.zeros((depth, LANES - n_gate), F32)],
                             axis=-1).astype(F32).reshape(depth, 1, LANES)
    w_out16 = w_out.astype(BF16)
    n_dense, _, d_ff = w_ffn_gate.shape
    assert d_ff % chunk == 0
    nc = d_ff // chunk
    split = lambda w: w.astype(BF16).reshape(n_dense, D, nc, chunk)
    wfgu = jnp.concatenate([split(w_ffn_gate), split(w_ffn_up)], axis=-1).transpose(0, 2, 1, 3)
    wfd = w_ffn_down.astype(BF16).reshape(n_dense, nc, chunk, D)
    wmgu = jnp.concatenate([w_moe_gate.astype(BF16), w_moe_up.astype(BF16)], axis=-1)
    wmd = w_moe_down.astype(BF16)
    w_r = jnp.concatenate([w_router, jnp.zeros(w_router.shape[:2] + (LANES - E,), F32)], axis=-1)
    b_r = jnp.concatenate([b_router, jnp.zeros((b_router.shape[0], LANES - E), F32)],
                          axis=-1).reshape(-1, 1, LANES)
    ckT = jnp.transpose(cache_k, (0, 1, 3, 4, 2)).reshape(depth, n_pool, df, PAGE)
    cvT = jnp.transpose(cache_v, (0, 1, 3, 4, 2)).reshape(depth, n_pool, df, PAGE)
    clfT = jnp.swapaxes(cache_logf, 2, 3)
    sn = state_n.reshape(depth, DB, dq)

    xp = x_prompt.reshape(M, D)
    xs = x_sample.reshape(DB, D)
    pC, pn, pm, plf = [], [], [], []
    kv_acc = ()
    sn_o, sm, sk, sv, slf = [], [], [], [], []
    sC_acc = ()
    for l in range(depth):
        gmix = g_mix_norm[l].reshape(1, D)
        gm = g_out_m[l].reshape(1, dv)
        gf = g_out_f[l].reshape(1, df)
        w_om, w_of = w_out16[l, :dv], w_out16[l, dv:]
        (qm, vm, kT, om, qp, kp, vp, kf, vf, G, Gc, GT, GcT) = _inproj(
            xp, gmix, w_m[l], w_f[l], w_gate[l], b_gate[l], scat, dims, prompt=True, tm=512, seq_len=T,
            layer=l, depth=depth, carried=kv_acc)
        kv_acc = (kf, vf)
        hm, caug, mout = _mlstm_prompt(qm, vm, kT, om, G, Gc, GT, GcT, gm, B=B, T=T, H=HM, DK=DK, DV=DV,
                                       L=min(512, T), c_ig=c_ig, c_lf=c_lf)
        hf = _fox_prompt(qp, kp, vp, gf, B=B, T=T, HF=HF, HD=HD, TB=512, G=HF, RQ=256, KS=2)
        xp = _outproj(xp, hm, hf, w_om, w_of, tm=512)
        pC.append(caug[..., :DV]); pn.append(caug[..., DV]); pm.append(mout[:, :HM, 0])
        plf.append(G[:, :HF].reshape(B, T, HF))
        (qm, km, vm, om, qf, kf, vf, G) = _inproj(
            xs, gmix, w_m[l], w_f[l], w_gate[l], b_gate[l], None, dims, prompt=False, tm=DB, seq_len=1)
        hm, Cn, nn, mn = _mlstm_step(qm, km, vm, om, G, state_C, sn, state_m, l, gm,
                                     NB=8, H=HM, DK=DK, DV=DV, c_ig=c_ig, c_lf=c_lf, carried=sC_acc)
        sC_acc = (Cn,)
        hf = _fox_step(qf, kf, vf, G, ckT, cvT, clfT, page_table, l, gf, HF=HF, HD=HD)
        xs = _outproj(xs, hm, hf, w_om, w_of, tm=DB)
        sn_o.append(nn.reshape(DB, HM, DK)); sm.append(mn[:, :HM])
        sk.append(kf.reshape(DB, 1, HF, HD)); sv.append(vf.reshape(DB, 1, HF, HD))
        slf.append(G[:, :HF].reshape(DB, 1, HF))
        gffn = g_ffn_norm[l].reshape(1, D)
        gfin = g_final.reshape(1, D) if l == depth - 1 else None
        j = l // 2
        if l % 2 == 0:
            args = (wfgu[j], wfd[j], None)
        else:
            args = (wmgu[j], wmd[j], (w_r[j], b_r[j]))
        xp = _ffn(xp, gffn, *args, gfin, tm=512 if l % 2 == 0 else 1024)
        xs = _ffn(xs, gffn, *args, gfin, tm=DB)

    tok_major = lambda a: a.reshape(depth, B, HF, HD, T).transpose(0, 1, 4, 2, 3)
    return (xp.reshape(B, T, D), xs.reshape(DB, 1, D),
            jnp.stack(pC), jnp.stack(pn), jnp.stack(pm),
            tok_major(kv_acc[0]), tok_major(kv_acc[1]), jnp.stack(plf),
            sC_acc[0], jnp.stack(sn_o), jnp.stack(sm),
            jnp.stack(sk), jnp.stack(sv), jnp.stack(slf))
```
